```python
import math
import jax
import jax.numpy as jnp
from jax import lax
import numpy as np

D_MODEL = 2048
BATCH = 1
SEQ = 16384
DEPTH = 2

GRID_W = 64
CTX_LEN = 256
HEAD_DIM = 128
H_DN = D_MODEL // (4 * HEAD_DIM)
H_AT = D_MODEL // (2 * HEAD_DIM)
H_KV = H_AT // 4
H_ML = D_MODEL // (4 * HEAD_DIM)
W_DN = H_DN * HEAD_DIM
W_AT = H_AT * HEAD_DIM
W_KV = H_KV * HEAD_DIM
W_ML = H_ML * HEAD_DIM
IN_SPLITS = (3 * W_DN, W_DN, 2 * H_DN, 2 * H_DN, W_AT, W_KV, W_KV, W_ML, W_ML, W_ML, W_ML, 2 * H_ML, 2 * H_ML)
IN_COLS = sum(IN_SPLITS)
CONV_W = 5
DN_CHUNK = 64
ML_CHUNK = 64
Q_BLOCK = 128
ROPE_THETA = 10000.0
ROPE_PAIRS = HEAD_DIM // 4
QK_SCALE = HEAD_DIM ** -0.5
N_EXPERTS = 16
N_GROUPS = 4
EXPERTS_PER_GROUP = N_EXPERTS // N_GROUPS
GROUP_SCORE_TOPK = 2
TOPK_GROUPS = 1
TOP_K = 2
D_EXPERT = D_MODEL // 2
MOE_BLOCK = 256
EPS = 1e-6
M_INIT = -1e30

kernel_name = 'hybrid_dit_gdn_gqa_mlstm_moe'


def rms_norm(x, g):
    xf = x.astype(jnp.float32)
    y = xf * lax.rsqrt(jnp.mean(xf * xf, axis=-1, keepdims=True) + EPS)
    return (y * g.astype(jnp.float32)).astype(x.dtype)


def l2_norm(x):
    xf = x.astype(jnp.float32)
    return xf * lax.rsqrt(jnp.sum(xf * xf, axis=-1, keepdims=True) + EPS)


def modulate(h, shift, scale):
    return h * (1 + scale) + shift


def heads(t, n):
    return t.reshape(t.shape[:2] + (n, HEAD_DIM))


def short_conv(x, w):
    return lax.conv_general_dilated(x, w[:, None, :].astype(x.dtype), window_strides=(1,),
                                    padding=[(CONV_W // 2, CONV_W // 2)],
                                    dimension_numbers=('NWC', 'WIO', 'NWC'),
                                    feature_group_count=x.shape[-1])


def rope_angles(n_tokens):
    n_rows = n_tokens // GRID_W
    rows = jnp.repeat(jnp.arange(n_rows, dtype=jnp.float32), GRID_W)
    cols = jnp.tile(jnp.arange(GRID_W, dtype=jnp.float32), n_rows)
    inv_freq = jnp.power(ROPE_THETA, -jnp.arange(ROPE_PAIRS, dtype=jnp.float32) / ROPE_PAIRS)
    return rows[:, None] * inv_freq, cols[:, None] * inv_freq


def apply_rope_2d(x, ang_r, ang_c):
    def rot(xs, ang):
        cos = jnp.cos(ang)[None, :, None, :]
        sin = jnp.sin(ang)[None, :, None, :]
        x1, x2 = jnp.split(xs, 2, axis=-1)
        return jnp.concatenate([x1 * cos - x2 * sin, x1 * sin + x2 * cos], axis=-1)
    xr, xc = jnp.split(x.astype(jnp.float32), 2, axis=-1)
    return jnp.concatenate([rot(xr, ang_r), rot(xc, ang_c)], axis=-1).astype(x.dtype)


def to_chunks(t, size):
    b, l, h = t.shape[:3]
    t = t.reshape((b, l // size, size, h) + t.shape[3:])
    return t.transpose((1, 0, 3, 2) + tuple(range(4, t.ndim)))


def from_chunks(o):
    n, b, h, c, d = o.shape
    return o.transpose(1, 0, 3, 2, 4).reshape(b, n * c, h, d)


def gdn_scan(q, k, v, beta, g, s0, emit):
    f32 = jnp.float32
    qc, kc, vc = to_chunks(q, DN_CHUNK), to_chunks(k, DN_CHUNK), to_chunks(v, DN_CHUNK)
    bc, gc = to_chunks(beta, DN_CHUNK), to_chunks(g, DN_CHUNK)
    cum = jnp.cumsum(gc, axis=-1)
    incl = jnp.tril(jnp.ones((DN_CHUNK, DN_CHUNK), bool))
    strict = jnp.tril(jnp.ones((DN_CHUNK, DN_CHUNK), bool), -1)
    decay = jnp.exp(jnp.where(incl, cum[..., :, None] - cum[..., None, :], -jnp.inf))
    kk = jnp.einsum('nbhcd,nbhsd->nbhcs', kc, kc)
    a = jnp.eye(DN_CHUNK, dtype=f32) + jnp.where(strict, bc[..., :, None] * kk * decay, 0.0)
    w = lax.linalg.triangular_solve(a, (bc * jnp.exp(cum))[..., None] * kc,
                                    left_side=True, lower=True, unit_diagonal=True)
    u0 = lax.linalg.triangular_solve(a, bc[..., None] * vc, left_side=True, lower=True, unit_diagonal=True)
    k_dec = kc * jnp.exp(cum[..., -1:] - cum)[..., None]
    g_last = jnp.exp(cum[..., -1])
    xs = (w, u0, k_dec, g_last)
    if emit:
        xs = xs + (qc * jnp.exp(cum)[..., None], jnp.einsum('nbhcd,nbhsd->nbhcs', qc, kc) * decay)

    def step(s, xs_):
        w_, u0_, k_dec_, g_last_ = xs_[:4]
        u = u0_ - jnp.einsum('bhck,bhkv->bhcv', w_, s)
        o = None
        if emit:
            o = jnp.einsum('bhck,bhkv->bhcv', xs_[4], s) + jnp.einsum('bhcs,bhsv->bhcv', xs_[5], u)
        s = g_last_[..., None, None] * s + jnp.einsum('bhck,bhcv->bhkv', k_dec_, u)
        return s, o

    s, o = lax.scan(step, s0, xs)
    return (from_chunks(o) if emit else None), s


def mlstm_scan(q, k, v, i_pre, log_f, state, emit):
    qc, kc, vc = to_chunks(q, ML_CHUNK), to_chunks(k, ML_CHUNK), to_chunks(v, ML_CHUNK)
    ic, fc = to_chunks(i_pre, ML_CHUNK), to_chunks(log_f, ML_CHUNK)
    b = jnp.cumsum(fc, axis=-1)
    w_log = b[..., -1:] - b + ic
    m_st = jnp.max(w_log, axis=-1)
    e_w = jnp.exp(w_log - m_st[..., None])
    kv = jnp.einsum('nbhck,nbhc,nbhcv->nbhkv', kc, e_w, vc)
    nk = jnp.einsum('nbhck,nbhc->nbhk', kc, e_w)
    xs = (b[..., -1], m_st, kv, nk)
    if emit:
        incl = jnp.tril(jnp.ones((ML_CHUNK, ML_CHUNK), bool))
        d_log = jnp.where(incl, b[..., :, None] - b[..., None, :] + ic[..., None, :], -jnp.inf)
        m_loc = jnp.max(d_log, axis=-1)
        p = jnp.exp(d_log - m_loc[..., None]) * jnp.einsum('nbhck,nbhsk->nbhcs', qc, kc)
        xs = xs + (qc, b, m_loc, jnp.einsum('nbhcs,nbhsv->nbhcv', p, vc), jnp.sum(p, axis=-1))

    def step(carry, xs_):
        c_mat, n_vec, m = carry
        b_last_, m_st_, kv_, nk_ = xs_[:4]
        h = None
        if emit:
            q_, b_, m_loc_, num_loc_, den_loc_ = xs_[4:]
            inter = b_ + m[..., None]
            m_r = jnp.maximum(inter, m_loc_)
            a_in = jnp.exp(inter - m_r)
            a_lo = jnp.exp(m_loc_ - m_r)
            num = a_in[..., None] * jnp.einsum('bhck,bhkv->bhcv', q_, c_mat) + a_lo[..., None] * num_loc_
            den = a_in * jnp.einsum('bhck,bhk->bhc', q_, n_vec) + a_lo * den_loc_
            h = num / jnp.maximum(jnp.abs(den), jnp.exp(-m_r))[..., None]
        m_new = jnp.maximum(b_last_ + m, m_st_)
        s_old = jnp.exp(b_last_ + m - m_new)
        s_new = jnp.exp(m_st_ - m_new)
        c_mat = s_old[..., None, None] * c_mat + s_new[..., None, None] * kv_
        n_vec = s_old[..., None] * n_vec + s_new[..., None] * nk_
        return (c_mat, n_vec, m_new), h

    state, h = lax.scan(step, state, xs)
    return (from_chunks(h) if emit else None), state


def bidirectional(scan_fn, init, ctx_in, lat_in, need_ctx):
    outs_c, outs_l = [], []
    for d in range(2):
        flip = (lambda t: t) if d == 0 else (lambda t: jnp.flip(t, axis=1))
        args_c = [flip(t) for t in ctx_in[0]] + [flip(t[:, :, d]) for t in ctx_in[1]]
        args_l = [flip(t) for t in lat_in[0]] + [flip(t[:, :, d]) for t in lat_in[1]]
        o_c, state = scan_fn(*args_c, init, need_ctx)
        o_l, _ = scan_fn(*args_l, state, True)
        outs_l.append(flip(o_l))
        if need_ctx:
            outs_c.append(flip(o_c))
    y_c = outs_c[0] + outs_c[1] if need_ctx else None
    return y_c, outs_l[0] + outs_l[1]


def block_attention(q, k, v):
    b, l = q.shape[:2]
    n_blk = l // Q_BLOCK
    grp = H_AT // H_KV
    qb = q.reshape(b, n_blk, Q_BLOCK, H_KV, grp, HEAD_DIM).transpose(1, 0, 3, 4, 2, 5)
    kt = k.transpose(0, 2, 1, 3)
    vt = v.transpose(0, 2, 1, 3)

    def one_block(q_blk):
        s = jnp.einsum('bkgqd,bksd->bkgqs', q_blk, kt).astype(jnp.float32) * QK_SCALE
        p = jax.nn.softmax(s, axis=-1).astype(vt.dtype)
        return jnp.einsum('bkgqs,bksd->bkgqd', p, vt)

    o = lax.map(one_block, qb)
    return o.transpose(1, 0, 4, 2, 3, 5).reshape(b, l, W_AT)


def token_mixers(h_c, h_l, w_in, dn_conv, dn_a_log, dn_dt_bias, dn_norm_g, q_norm_g, k_norm_g,
                 ml_i_bias, ml_f_bias, ml_norm_g, ang_r, ang_c, need_ctx):
    f32 = jnp.float32
    cuts = [int(o) for o in np.cumsum(IN_SPLITS)[:-1]]
    parts_c = jnp.split(h_c @ w_in, cuts, axis=-1)
    parts_l = jnp.split(h_l @ w_in, cuts, axis=-1)

    def dir_gates(t, n):
        return t.astype(f32).reshape(t.shape[:2] + (2, n))

    def dn_inputs(p):
        q, k, v = jnp.split(jax.nn.silu(short_conv(p[0], dn_conv)), 3, axis=-1)
        q = l2_norm(heads(q, H_DN)) * QK_SCALE
        k = l2_norm(heads(k, H_DN))
        v = heads(v, H_DN).astype(f32)
        beta = jax.nn.sigmoid(dir_gates(p[2], H_DN))
        g = -jnp.exp(dn_a_log.astype(f32)) * jax.nn.softplus(dir_gates(p[3], H_DN) + dn_dt_bias.astype(f32))
        return (q, k, v), (beta, g)

    def ml_inputs(p):
        q = heads(p[7], H_ML).astype(f32)
        k = heads(p[8], H_ML).astype(f32) * QK_SCALE
        v = heads(p[9], H_ML).astype(f32)
        i_pre = dir_gates(p[11], H_ML) + ml_i_bias.astype(f32)
        log_f = jax.nn.log_sigmoid(dir_gates(p[12], H_ML) + ml_f_bias.astype(f32))
        return (q, k, v), (i_pre, log_f)

    def at_q(p, rotate):
        q = rms_norm(heads(p[4], H_AT), q_norm_g)
        return apply_rope_2d(q, ang_r, ang_c) if rotate else q

    def at_kv(p, rotate):
        k = rms_norm(heads(p[5], H_KV), k_norm_g)
        k = apply_rope_2d(k, ang_r, ang_c) if rotate else k
        return k, heads(p[6], H_KV)

    batch = h_l.shape[0]
    dn_init = jnp.zeros((batch, H_DN, HEAD_DIM, HEAD_DIM), f32)
    dn_c, dn_l = bidirectional(gdn_scan, dn_init, dn_inputs(parts_c), dn_inputs(parts_l), need_ctx)
    ml_init = (jnp.zeros((batch, H_ML, HEAD_DIM, HEAD_DIM), f32), jnp.zeros((batch, H_ML, HEAD_DIM), f32),
               jnp.full((batch, H_ML), M_INIT, f32))
    ml_c, ml_l = bidirectional(mlstm_scan, ml_init, ml_inputs(parts_c), ml_inputs(parts_l), need_ctx)
    k_c, v_c = at_kv(parts_c, False)
    k_l, v_l = at_kv(parts_l, True)
    at_l = block_attention(at_q(parts_l, True), jnp.concatenate([k_c, k_l], axis=1),
                           jnp.concatenate([v_c, v_l], axis=1))

    def merge(p, dn_o, at_o, ml_o):
        dt = p[0].dtype
        dn_o = rms_norm(dn_o.astype(dt), dn_norm_g) * jax.nn.silu(heads(p[1], H_DN))
        ml_o = rms_norm(ml_o.astype(dt), ml_norm_g) * jax.nn.sigmoid(heads(p[10], H_ML))
        b_, l_ = dn_o.shape[:2]
        return jnp.concatenate([dn_o.reshape(b_, l_, W_DN), at_o, ml_o.reshape(b_, l_, W_ML)], axis=-1)

    y_l = merge(parts_l, dn_l, at_l, ml_l)
    y_c = merge(parts_c, dn_c, block_attention(at_q(parts_c, False), k_c, v_c), ml_c) if need_ctx else None
    return y_c, y_l


def moe(h, router_w, router_bias, w_gate, w_up, w_down):
    f32 = jnp.float32
    n_tok = h.shape[0]
    scores = jax.nn.sigmoid(h.astype(f32) @ router_w.astype(f32))
    biased = scores + router_bias.astype(f32)
    group_score = jnp.sum(lax.top_k(biased.reshape(n_tok, N_GROUPS, EXPERTS_PER_GROUP), GROUP_SCORE_TOPK)[0], axis=-1)
    _, g_idx = lax.top_k(group_score, TOPK_GROUPS)
    g_mask = jnp.any(g_idx[:, :, None] == jnp.arange(N_GROUPS)[None, None, :], axis=1)
    e_mask = jnp.repeat(g_mask, EXPERTS_PER_GROUP, axis=1)
    _, e_idx = lax.top_k(jnp.where(e_mask, biased, -jnp.inf), TOP_K)
    gates = jnp.take_along_axis(scores, e_idx, axis=1)
    gates = gates / jnp.sum(gates, axis=-1, keepdims=True)

    n_assign = n_tok * TOP_K
    flat_e = e_idx.reshape(n_assign)
    flat_t = jnp.arange(n_assign, dtype=jnp.int32) // TOP_K
    flat_w = gates.reshape(n_assign)
    order = jnp.argsort(flat_e)
    sorted_e = flat_e[order]
    counts = jnp.bincount(flat_e, length=N_EXPERTS)
    starts = jnp.cumsum(counts) - counts
    padded = (counts + MOE_BLOCK - 1) // MOE_BLOCK * MOE_BLOCK
    p_ends = jnp.cumsum(padded)
    p_starts = p_ends - padded
    dest = p_starts[sorted_e] + jnp.arange(n_assign, dtype=jnp.int32) - starts[sorted_e]
    n_blocks = (n_assign + N_EXPERTS * (MOE_BLOCK - 1) + MOE_BLOCK - 1) // MOE_BLOCK
    n_slots = n_blocks * MOE_BLOCK
    slot_tok = jnp.zeros((n_slots,), jnp.int32).at[dest].set(flat_t[order])
    slot_w = jnp.zeros((n_slots,), h.dtype).at[dest].set(flat_w[order].astype(h.dtype))
    blk_e = jnp.minimum(jnp.searchsorted(p_ends, jnp.arange(n_blocks) * MOE_BLOCK, side='right'), N_EXPERTS - 1)

    def expert_block(args):
        e, tok, wt = args
        xb = h[tok]
        y = (jax.nn.silu(xb @ w_gate[e]) * (xb @ w_up[e])) @ w_down[e]
        return y * wt[:, None]

    ys = lax.map(expert_block, (blk_e, slot_tok.reshape(n_blocks, MOE_BLOCK), slot_w.reshape(n_blocks, MOE_BLOCK)))
    return jnp.zeros_like(h).at[slot_tok].add(ys.reshape(n_slots, -1))


def setup_inputs(seed: int = 0) -> dict:
    key = jax.random.key(seed)
    ks = jax.random.split(key, 25)
    f32 = jnp.float32

    def normal(k, shape, scale):
        return jax.random.normal(k, shape, f32) * scale

    def gain(k, shape):
        return 1.0 + normal(k, shape, 0.02)

    dt = jnp.exp(jax.random.uniform(ks[10], (DEPTH, 2, H_DN), f32, math.log(1e-3), math.log(1e-1)))
    return {
        'x': normal(ks[0], (BATCH, SEQ, D_MODEL), 1.0),
        'c': normal(ks[1], (BATCH, D_MODEL), 1.0),
        'ctx': normal(ks[2], (BATCH, CTX_LEN, D_MODEL), 1.0),
        'c_ctx': normal(ks[3], (D_MODEL,), 1.0),
        'w_mod': normal(ks[4], (DEPTH, D_MODEL, 6 * D_MODEL), 0.5 * D_MODEL ** -0.5),
        'b_mod': normal(ks[5], (DEPTH, 6 * D_MODEL), 0.02),
        'norm1_g': gain(ks[6], (DEPTH, D_MODEL)),
        'norm2_g': gain(ks[7], (DEPTH, D_MODEL)),
        'w_in': normal(ks[8], (DEPTH, D_MODEL, IN_COLS), D_MODEL ** -0.5),
        'dn_conv': normal(ks[9], (DEPTH, CONV_W, 3 * W_DN), CONV_W ** -0.5),
        'dn_a_log': jnp.log(jax.random.uniform(ks[11], (DEPTH, 2, H_DN), f32, 1.0, 16.0)),
        'dn_dt_bias': dt + jnp.log(-jnp.expm1(-dt)),
        'dn_norm_g': gain(ks[12], (DEPTH, HEAD_DIM)),
        'q_norm_g': gain(ks[13], (DEPTH, HEAD_DIM)),
        'k_norm_g': gain(ks[14], (DEPTH, HEAD_DIM)),
        'ml_i_bias': normal(ks[15], (DEPTH, 2, H_ML), 0.1),
        'ml_f_bias': jax.random.uniform(ks[16], (DEPTH, 2, H_ML), f32, 3.0, 6.0),
        'ml_norm_g': gain(ks[17], (DEPTH, HEAD_DIM)),
        'w_out': normal(ks[18], (DEPTH, D_MODEL, D_MODEL), D_MODEL ** -0.5),
        'router_w': normal(ks[19], (D_MODEL, N_EXPERTS), D_MODEL ** -0.5),
        'router_bias': normal(ks[20], (N_EXPERTS,), 0.01),
        'w_gate': normal(ks[21], (DEPTH, N_EXPERTS, D_MODEL, D_EXPERT), D_MODEL ** -0.5),
        'w_up': normal(ks[22], (DEPTH, N_EXPERTS, D_MODEL, D_EXPERT), D_MODEL ** -0.5),
        'w_down': normal(ks[23], (DEPTH, N_EXPERTS, D_EXPERT, D_MODEL), D_EXPERT ** -0.5),
        'final_norm_g': gain(ks[24], (D_MODEL,)),
    }


def reference(x, c, ctx, c_ctx, w_mod, b_mod, norm1_g, norm2_g, w_in, dn_conv, dn_a_log, dn_dt_bias,
              dn_norm_g, q_norm_g, k_norm_g, ml_i_bias, ml_f_bias, ml_norm_g, w_out, router_w, router_bias,
              w_gate, w_up, w_down, final_norm_g):
    ang_r, ang_c = rope_angles(x.shape[1])
    x_l, x_c = x, ctx
    s_c, s_ctx = jax.nn.silu(c), jax.nn.silu(c_ctx)
    for layer in range(DEPTH):
        last = layer == DEPTH - 1
        n_ctx_mod = 2 if last else 6
        mod_l = jnp.split((s_c @ w_mod[layer] + b_mod[layer])[:, None, :], 6, axis=-1)
        mod_c = jnp.split((s_ctx @ w_mod[layer][:, :n_ctx_mod * D_MODEL]
                           + b_mod[layer][:n_ctx_mod * D_MODEL])[None, None, :], n_ctx_mod, axis=-1)
        h_l = modulate(rms_norm(x_l, norm1_g[layer]), mod_l[0], mod_l[1])
        h_c = modulate(rms_norm(x_c, norm1_g[layer]), mod_c[0], mod_c[1])
        y_c, y_l = token_mixers(h_c, h_l, w_in[layer], dn_conv[layer], dn_a_log[layer], dn_dt_bias[layer],
                                dn_norm_g[layer], q_norm_g[layer], k_norm_g[layer], ml_i_bias[layer],
                                ml_f_bias[layer], ml_norm_g[layer], ang_r, ang_c, not last)
        x_l = x_l + mod_l[2] * (y_l @ w_out[layer])
        h2_l = modulate(rms_norm(x_l, norm2_g[layer]), mod_l[3], mod_l[4])
        experts = (router_w, router_bias, w_gate[layer], w_up[layer], w_down[layer])
        if last:
            x_l = x_l + mod_l[5] * moe(h2_l.reshape(-1, D_MODEL), *experts).reshape(x_l.shape)
        else:
            x_c = x_c + mod_c[2] * (y_c @ w_out[layer])
            h2_c = modulate(rms_norm(x_c, norm2_g[layer]), mod_c[3], mod_c[4])
            n_c = h2_c.shape[0] * h2_c.shape[1]
            f = moe(jnp.concatenate([h2_c.reshape(-1, D_MODEL), h2_l.reshape(-1, D_MODEL)], axis=0), *experts)
            x_c = x_c + mod_c[5] * f[:n_c].reshape(x_c.shape)
            x_l = x_l + mod_l[5] * f[n_c:].reshape(x_l.shape)
    return rms_norm(x_l, final_norm_g)
```

```python
import functools

import jax
import jax.numpy as jnp
from jax import lax
from jax.experimental import pallas as pl
from jax.experimental.pallas import tpu as pltpu

f32 = jnp.float32
bf16 = jnp.bfloat16

D = 2048
HD = 128
N_HEADS_SCAN = 4
H_AT = 8
H_KV = 2
CHUNK = 64
BLK = 256
CPB = BLK // CHUNK
GRID_W = 64
ROPE_THETA = 10000.0
QK_SCALE = HD ** -0.5
N_EXPERTS = 16
EXPERTS_PER_GROUP = 4
D_EXPERT = D // 2
MOE_BM = 256
EPS = 1e-6
NEG = -1e30
CONV_W = 5

C_DNQKV, C_DNZ, C_ATQ, C_ATK, C_ATV, C_MLQ, C_MLK, C_MLV, C_MLO, P_COLS = (
    0, 1536, 2048, 3072, 3328, 3584, 4096, 4608, 5120, 5632)

V7X_VMEM_LIMIT_MB = 56


def _cparams(sems, vmem_mb=None):
    return pltpu.CompilerParams(
        dimension_semantics=sems,
        vmem_limit_bytes=None if vmem_mb is None else vmem_mb << 20)


def _mm(a, b):
    return jnp.dot(a.astype(bf16), b.astype(bf16), preferred_element_type=f32)


def _mm_nt(a, b):
    return lax.dot_general(a.astype(bf16), b.astype(bf16), (((1,), (1,)), ((), ())),
                           preferred_element_type=f32)


def _mm_tn(a, b):
    return lax.dot_general(a.astype(bf16), b.astype(bf16), (((0,), (0,)), ((), ())),
                           preferred_element_type=f32)


def _split3(x):
    x1 = x.astype(bf16)
    r1 = x - x1.astype(f32)
    x2 = r1.astype(bf16)
    x3 = (r1 - x2.astype(f32)).astype(bf16)
    return x1, x2, x3


def _sigmoid(x):
    return 1.0 / (1.0 + jnp.exp(-x))


def _softplus(x):
    return jnp.maximum(x, 0.0) + jnp.log(1.0 + jnp.exp(-jnp.abs(x)))


def _mod_kernel(s_ref, w_ref, b_ref, o_ref):
    s = s_ref[...]
    s = s * _sigmoid(s)
    o_ref[0] = jnp.dot(s, w_ref[0], preferred_element_type=f32,
                       precision=lax.Precision.HIGHEST) + b_ref[0]


def _mods(c, c_ctx, w_mod, b_mod):
    depth, d, n6 = w_mod.shape
    s = jnp.zeros((8, d), f32).at[0].set(c[0]).at[1].set(c_ctx)
    tn = 1024
    return pl.pallas_call(
        _mod_kernel,
        grid=(depth, n6 // tn),
        in_specs=[pl.BlockSpec((8, d), lambda l, j: (0, 0)),
                  pl.BlockSpec((1, d, tn), lambda l, j: (l, 0, j)),
                  pl.BlockSpec((1, 1, tn), lambda l, j: (l, 0, j))],
        out_specs=pl.BlockSpec((1, 8, tn), lambda l, j: (l, 0, j)),
        out_shape=jax.ShapeDtypeStruct((depth, 8, n6), f32),
        compiler_params=_cparams(("parallel", "parallel"), 40),
        name="mod_vectors",
    )(s, w_mod, b_mod.reshape(depth, 1, n6))


def _mod_rows(mod_ref, k, is_ctx):
    lat = mod_ref[0:1, k * D:(k + 1) * D]
    ctx = mod_ref[1:2, k * D:(k + 1) * D]
    return jnp.where(is_ctx, ctx, lat)


def _norm1_kernel(has_moe, tm, *refs):
    if has_moe:
        x_ref, y0_ref, y1_ref, modp_ref, mod_ref, g_ref, wg1_ref, wg2_ref, xo_ref, h_ref, gr_ref = refs
    else:
        x_ref, mod_ref, g_ref, wg1_ref, wg2_ref, h_ref, gr_ref = refs
    i = pl.program_id(0)
    rows = i * tm + lax.broadcasted_iota(jnp.int32, (tm, 1), 0)
    is_ctx = rows < BLK
    x = x_ref[...]
    if has_moe:
        x = x + _mod_rows(modp_ref, 5, is_ctx) * (y0_ref[...] + y1_ref[...])
        xo_ref[...] = x
    ms = jnp.mean(x * x, axis=-1, keepdims=True)
    y = x * lax.rsqrt(ms + EPS) * g_ref[...]
    h = y * (1.0 + _mod_rows(mod_ref, 1, is_ctx)) + _mod_rows(mod_ref, 0, is_ctx)
    hh = h.astype(bf16)
    h_ref[...] = hh
    hl = (h - hh.astype(f32)).astype(bf16)
    wg1 = wg1_ref[...]
    gr_ref[...] = (jnp.dot(hh, wg1, preferred_element_type=f32)
                   + jnp.dot(hl, wg1, preferred_element_type=f32)
                   + jnp.dot(hh, wg2_ref[...], preferred_element_type=f32))


def _norm1(x, moe, mod_prev, mod_cur, g, wg1, wg2):
    t = x.shape[0]
    tm = 256
    nrow = t // tm
    row = lambda i: (i, 0)
    full = lambda i: (0, 0)
    in_specs = [pl.BlockSpec((tm, D), row)]
    args = [x]
    if moe is not None:
        in_specs += [pl.BlockSpec((tm, D), row), pl.BlockSpec((tm, D), lambda i: (i + nrow, 0)),
                     pl.BlockSpec((8, 6 * D), full)]
        args += [moe, moe, mod_prev]
    in_specs += [pl.BlockSpec((8, 6 * D), full), pl.BlockSpec((1, D), full),
                 pl.BlockSpec((D, 128), full), pl.BlockSpec((D, 128), full)]
    args += [mod_cur, g.reshape(1, D), wg1, wg2]
    out_specs = [pl.BlockSpec((tm, D), row), pl.BlockSpec((tm, 128), row)]
    out_shape = [jax.ShapeDtypeStruct((t, D), bf16), jax.ShapeDtypeStruct((t, 128), f32)]
    if moe is not None:
        out_specs = [pl.BlockSpec((tm, D), row)] + out_specs
        out_shape = [jax.ShapeDtypeStruct((t, D), f32)] + out_shape
    outs = pl.pallas_call(
        functools.partial(_norm1_kernel, moe is not None, tm),
        grid=(nrow,), in_specs=in_specs, out_specs=out_specs, out_shape=out_shape,
        compiler_params=_cparams(("parallel",), 40),
        name="norm1_modulate",
    )(*args)
    if moe is not None:
        return outs
    return [x] + list(outs)


def _matmul_kernel(a_ref, b_ref, o_ref):
    o_ref[...] = jnp.dot(a_ref[...], b_ref[...], preferred_element_type=f32).astype(o_ref.dtype)


def _row_tile(t, choices):
    for c in choices:
        if t % c == 0:
            return c
    raise ValueError(f"no row tile for {t}")


def _matmul(a, b, out_dtype):
    m, k = a.shape
    n = b.shape[1]
    tm = _row_tile(m, (1280, 768, 512, 256))
    tn = _row_tile(n, (1408, 1024, 512))
    return pl.pallas_call(
        _matmul_kernel,
        grid=(n // tn, m // tm),
        in_specs=[pl.BlockSpec((tm, k), lambda j, i: (i, 0)),
                  pl.BlockSpec((k, tn), lambda j, i: (0, j))],
        out_specs=pl.BlockSpec((tm, tn), lambda j, i: (i, j)),
        out_shape=jax.ShapeDtypeStruct((m, n), out_dtype),
        compiler_params=_cparams(("parallel", "parallel"), 48),
        name="in_projection",
    )(a, b)


def _scan_prep_kernel(nblk, cur_ref, prev_ref, next_ref, cw_ref, graw_ref, gp_ref, q_ref, gc_ref, grow_ref, xs):
    i = pl.program_id(0)
    has_prev = i >= 2
    has_next = jnp.logical_and(i >= 1, i < nblk - 1)
    xs[0:16, :] = jnp.where(has_prev, prev_ref[...].astype(f32), 0.0)
    xs[16:16 + BLK, :] = cur_ref[...].astype(f32)
    xs[16 + BLK:32 + BLK, :] = jnp.where(has_next, next_ref[...].astype(f32), 0.0)
    acc = cw_ref[0:1, :] * xs[pl.ds(16 - CONV_W // 2, BLK), :]
    for j in range(1, CONV_W):
        acc = acc + cw_ref[j:j + 1, :] * xs[pl.ds(16 - CONV_W // 2 + j, BLK), :]
    a = acc * _sigmoid(acc)
    w = N_HEADS_SCAN * HD
    for h in range(2 * N_HEADS_SCAN):
        xh = a[:, h * HD:(h + 1) * HD]
        inv = lax.rsqrt(jnp.sum(xh * xh, axis=-1, keepdims=True) + EPS)
        scale = QK_SCALE if h < N_HEADS_SCAN else 1.0
        q_ref[:, h * HD:(h + 1) * HD] = (xh * (inv * scale)).astype(bf16)
    q_ref[:, 2 * w:3 * w] = a[:, 2 * w:3 * w].astype(bf16)

    z = graw_ref[...] + gp_ref[0:1, :]
    lane = lax.broadcasted_iota(jnp.int32, (BLK, 128), 1)
    sp = _softplus(z)
    vals = jnp.where(lane < 8, _sigmoid(z),
                     jnp.where(lane < 16, -jnp.exp(gp_ref[1:2, :]) * sp,
                               jnp.where(lane < 24, z, z - sp)))
    r = lax.broadcasted_iota(jnp.int32, (BLK, BLK), 0)
    c = lax.broadcasted_iota(jnp.int32, (BLK, BLK), 1)
    same = jnp.right_shift(r, 6) == jnp.right_shift(c, 6)
    tri_lo = jnp.where(jnp.logical_and(same, r >= c), 1.0, 0.0).astype(bf16)
    tri_up = jnp.where(jnp.logical_and(same, r <= c), 1.0, 0.0).astype(bf16)
    v1, v2, v3 = _split3(vals)
    dot = functools.partial(jnp.dot, preferred_element_type=f32)
    prefix = dot(tri_lo, v1) + dot(tri_lo, v2) + dot(tri_lo, v3)
    suffix = dot(tri_up, v1) + dot(tri_up, v2) + dot(tri_up, v3)
    is_cum = jnp.logical_and(jnp.bitwise_and(lane, 8) == 8, lane < 32)
    is_bwd = jnp.bitwise_and(lane, 4) == 4
    out = jnp.where(is_cum, jnp.where(is_bwd, suffix, prefix), vals)
    gc_ref[...] = out
    gt = out.T
    for cc in range(CPB):
        grow_ref[cc] = gt[0:32, cc * CHUNK:(cc + 1) * CHUNK]


def _scan_prep(p, graw, conv_w, gate_params):
    t = p.shape[0]
    nblk = t // BLK
    wq = 3 * N_HEADS_SCAN * HD
    n16 = t // 16
    return pl.pallas_call(
        functools.partial(_scan_prep_kernel, nblk),
        grid=(nblk,),
        in_specs=[pl.BlockSpec((BLK, wq), lambda i: (i, 0)),
                  pl.BlockSpec((16, wq), lambda i: (jnp.maximum(i * (BLK // 16) - 1, 0), 0)),
                  pl.BlockSpec((16, wq), lambda i: (jnp.minimum((i + 1) * (BLK // 16), n16 - 1), 0)),
                  pl.BlockSpec((8, wq), lambda i: (0, 0)),
                  pl.BlockSpec((BLK, 128), lambda i: (i, 0)),
                  pl.BlockSpec((8, 128), lambda i: (0, 0))],
        out_specs=[pl.BlockSpec((BLK, wq), lambda i: (i, 0)),
                   pl.BlockSpec((BLK, 128), lambda i: (i, 0)),
                   pl.BlockSpec((CPB, 32, CHUNK), lambda i: (i, 0, 0))],
        out_shape=[jax.ShapeDtypeStruct((t, wq), bf16),
                   jax.ShapeDtypeStruct((t, 128), f32),
                   jax.ShapeDtypeStruct((t // CHUNK, 32, CHUNK), f32)],
        scratch_shapes=[pltpu.VMEM((BLK + 32, wq), f32)],
        compiler_params=_cparams(("parallel",), 40),
        name="scan_prep",
    )(p, p, p, conv_w, graw, gate_params)


def _tri_masks():
    r = lax.broadcasted_iota(jnp.int32, (CHUNK, CHUNK), 0)
    c = lax.broadcasted_iota(jnp.int32, (CHUNK, CHUNK), 1)
    blk = jnp.right_shift(r, 4) == jnp.right_shift(c, 4)
    eye = jnp.where(r == c, 1.0, 0.0)
    return (r >= c, r <= c), (r > c, r < c), blk, eye


def _gdn_kernel(qf_ref, qb_ref, gcf_ref, gcb_ref, grf_ref, grb_ref, of_ref, ob_ref, s_scr):
    @pl.when(pl.program_id(0) == 0)
    def _():
        s_scr[...] = jnp.zeros_like(s_scr)

    incl, strict, blk, eye = _tri_masks()
    w = N_HEADS_SCAN * HD

    def chunk(cc, carry):
        for d in range(2):
            c = cc if d == 0 else CPB - 1 - cc
            q_ref, gc_ref, gr_ref, o_ref = (qf_ref, gcf_ref, grf_ref, of_ref) if d == 0 else (qb_ref, gcb_ref, grb_ref, ob_ref)
            rows = pl.ds(pl.multiple_of(c * CHUNK, CHUNK), CHUNK)
            gcol = gc_ref[rows, :]
            grow = gr_ref[c]
            for h in range(N_HEADS_SCAN):
                u_idx = d * N_HEADS_SCAN + h
                q = q_ref[rows, h * HD:(h + 1) * HD]
                k = q_ref[rows, w + h * HD:w + (h + 1) * HD]
                v = q_ref[rows, 2 * w + h * HD:2 * w + (h + 1) * HD]
                beta = gcol[:, u_idx:u_idx + 1]
                cum_c = gcol[:, 8 + u_idx:9 + u_idx]
                cum_r = grow[8 + u_idx:9 + u_idx, :]
                tot = cum_c[CHUNK - 1:CHUNK, :] if d == 0 else cum_c[0:1, :]
                decay = jnp.exp(jnp.where(incl[d], cum_c - cum_r, NEG))
                kk = _mm_nt(k, k)
                nm = jnp.where(strict[d], beta * kk * decay, 0.0)
                dm = jnp.where(blk, nm, 0.0)
                lm = nm - dm
                m2 = _mm(dm, dm)
                p2 = (eye - dm) + _mm(eye - dm, m2)
                m4 = _mm(m2, m2)
                p3 = p2 + _mm(p2, m4)
                m8 = _mm(m4, m4)
                dinv = p3 + _mm(p3, m8)
                qm = _mm(dinv, lm)
                qm2 = _mm(qm, qm)
                tm = eye - qm + qm2 - _mm(qm, qm2)
                kf = k.astype(f32)
                rhs = jnp.concatenate([(beta * jnp.exp(cum_c)) * kf, beta * v.astype(f32)], axis=1)
                sol = _mm(tm, _mm(dinv, rhs))
                w_m = sol[:, :HD]
                u0 = sol[:, HD:]
                k_dec = kf * jnp.exp(tot - cum_c)
                q_dec = q.astype(f32) * jnp.exp(cum_c)
                qk = _mm_nt(q, k) * decay
                s = s_scr[u_idx]
                u = u0 - _mm(w_m, s)
                o_ref[rows, h * HD:(h + 1) * HD] = _mm(q_dec, s) + _mm(qk, u)
                s_scr[u_idx] = jnp.exp(tot) * s + _mm_tn(k_dec, u)
        return carry

    lax.fori_loop(0, CPB, chunk, 0)


def _bwd_block(nblk):
    return lambda s: (jnp.where(s == 0, 0, nblk - s), 0)


def _gdn_scan(qkv, gcol, grow):
    t = qkv.shape[0]
    nblk = t // BLK
    wq = 3 * N_HEADS_SCAN * HD
    w = N_HEADS_SCAN * HD
    fwd = lambda s: (s, 0)
    bwd = _bwd_block(nblk)
    fwd3 = lambda s: (s, 0, 0)
    bwd3 = lambda s: (jnp.where(s == 0, 0, nblk - s), 0, 0)
    return pl.pallas_call(
        _gdn_kernel,
        grid=(nblk,),
        in_specs=[pl.BlockSpec((BLK, wq), fwd), pl.BlockSpec((BLK, wq), bwd),
                  pl.BlockSpec((BLK, 128), fwd), pl.BlockSpec((BLK, 128), bwd),
                  pl.BlockSpec((CPB, 32, CHUNK), fwd3), pl.BlockSpec((CPB, 32, CHUNK), bwd3)],
        out_specs=[pl.BlockSpec((BLK, w), fwd), pl.BlockSpec((BLK, w), bwd)],
        out_shape=[jax.ShapeDtypeStruct((t, w), f32), jax.ShapeDtypeStruct((t, w), f32)],
        scratch_shapes=[pltpu.VMEM((2 * N_HEADS_SCAN, HD, HD), f32)],
        compiler_params=_cparams(("arbitrary",), 40),
        name="gdn_scan",
    )(qkv, qkv, gcol, gcol, grow, grow)


def _mlstm_kernel(pf_q, pf_k, pf_v, pb_q, pb_k, pb_v, gcf_ref, gcb_ref, grf_ref, grb_ref,
                  of_ref, ob_ref, c_scr, n_scr, m_scr):
    @pl.when(pl.program_id(0) == 0)
    def _():
        c_scr[...] = jnp.zeros_like(c_scr)
        n_scr[...] = jnp.zeros_like(n_scr)
        m_scr[...] = jnp.full_like(m_scr, NEG)

    incl, _, _, _ = _tri_masks()

    def chunk(cc, carry):
        for d in range(2):
            c = cc if d == 0 else CPB - 1 - cc
            q_ref, k_ref, v_ref, gc_ref, gr_ref, o_ref = (
                (pf_q, pf_k, pf_v, gcf_ref, grf_ref, of_ref) if d == 0 else
                (pb_q, pb_k, pb_v, gcb_ref, grb_ref, ob_ref))
            rows = pl.ds(pl.multiple_of(c * CHUNK, CHUNK), CHUNK)
            gcol = gc_ref[rows, :]
            grow = gr_ref[c]
            for h in range(N_HEADS_SCAN):
                u_idx = d * N_HEADS_SCAN + h
                q = q_ref[rows, h * HD:(h + 1) * HD]
                k = k_ref[rows, h * HD:(h + 1) * HD]
                v = v_ref[rows, h * HD:(h + 1) * HD]
                i_c = gcol[:, 16 + u_idx:17 + u_idx]
                b_c = gcol[:, 24 + u_idx:25 + u_idx]
                i_r = grow[16 + u_idx:17 + u_idx, :]
                b_r = grow[24 + u_idx:25 + u_idx, :]
                b_last = b_c[CHUNK - 1:CHUNK, :] if d == 0 else b_c[0:1, :]
                w_log = b_last - b_c + i_c
                m_st = jnp.max(w_log, axis=0, keepdims=True)
                e_w = jnp.exp(w_log - m_st)
                kf = k.astype(f32)
                kv = _mm_tn(k, e_w * v.astype(f32)) * QK_SCALE
                nk = jnp.sum(kf * e_w, axis=0, keepdims=True) * QK_SCALE
                d_log = jnp.where(incl[d], b_c - b_r + i_r, NEG)
                m_loc = jnp.max(d_log, axis=-1, keepdims=True)
                p = jnp.exp(d_log - m_loc) * (_mm_nt(q, k) * QK_SCALE)
                num_loc = _mm(p, v)
                den_loc = jnp.sum(p, axis=-1, keepdims=True)
                c_mat = c_scr[u_idx]
                n_vec = n_scr[u_idx][0:1, :]
                m = m_scr[u_idx][0:1, 0:1]
                inter = b_c + m
                m_r = jnp.maximum(inter, m_loc)
                a_in = jnp.exp(inter - m_r)
                a_lo = jnp.exp(m_loc - m_r)
                num = a_in * _mm(q, c_mat) + a_lo * num_loc
                den = a_in * jnp.sum(q.astype(f32) * n_vec, axis=-1, keepdims=True) + a_lo * den_loc
                o_ref[rows, h * HD:(h + 1) * HD] = num / jnp.maximum(jnp.abs(den), jnp.exp(-m_r))
                m_new = jnp.maximum(b_last + m, m_st)
                s_old = jnp.exp(b_last + m - m_new)
                s_new = jnp.exp(m_st - m_new)
                c_scr[u_idx] = s_old * c_mat + s_new * kv
                n_scr[u_idx] = jnp.broadcast_to(s_old * n_vec + s_new * nk, (8, HD))
                m_scr[u_idx] = jnp.broadcast_to(m_new, (8, HD))
        return carry

    lax.fori_loop(0, CPB, chunk, 0)


def _mlstm_scan(p, gcol, grow):
    t = p.shape[0]
    nblk = t // BLK
    w = N_HEADS_SCAN * HD
    nu = 2 * N_HEADS_SCAN
    fwd = lambda s: (s, 0)
    bwd = _bwd_block(nblk)
    fwd3 = lambda s: (s, 0, 0)
    bwd3 = lambda s: (jnp.where(s == 0, 0, nblk - s), 0, 0)

    def col(base, bwd_dir):
        cb = base // w
        if bwd_dir:
            return pl.BlockSpec((BLK, w), lambda s: (jnp.where(s == 0, 0, nblk - s), cb))
        return pl.BlockSpec((BLK, w), lambda s: (s, cb))

    return pl.pallas_call(
        _mlstm_kernel,
        grid=(nblk,),
        in_specs=[col(C_MLQ, False), col(C_MLK, False), col(C_MLV, False),
                  col(C_MLQ, True), col(C_MLK, True), col(C_MLV, True),
                  pl.BlockSpec((BLK, 128), fwd), pl.BlockSpec((BLK, 128), bwd),
                  pl.BlockSpec((CPB, 32, CHUNK), fwd3), pl.BlockSpec((CPB, 32, CHUNK), bwd3)],
        out_specs=[pl.BlockSpec((BLK, w), fwd), pl.BlockSpec((BLK, w), bwd)],
        out_shape=[jax.ShapeDtypeStruct((t, w), f32), jax.ShapeDtypeStruct((t, w), f32)],
        scratch_shapes=[pltpu.VMEM((nu, HD, HD), f32), pltpu.VMEM((nu, 8, HD), f32),
                        pltpu.VMEM((nu, 8, HD), f32)],
        compiler_params=_cparams(("arbitrary",), 40),
        name="mlstm_scan",
    )(p, p, p, p, p, p, gcol, gcol, grow, grow)


def _attn_prep_kernel(q_ref, k_ref, qg_ref, kg_ref, qo_ref, ko_ref):
    i = pl.program_id(0)
    r = lax.broadcasted_iota(jnp.int32, (BLK, HD), 0)
    lane = lax.broadcasted_iota(jnp.int32, (BLK, HD), 1)
    tok = (i - 1) * BLK + r
    pos = jnp.where(lane < HD // 2, jnp.right_shift(tok, 6), jnp.bitwise_and(tok, GRID_W - 1)).astype(f32)
    pair = jnp.bitwise_and(lane, HD // 4 - 1).astype(f32)
    inv_freq = jnp.exp(pair * (-jnp.log(ROPE_THETA) / (HD // 4)))
    ang = pos * inv_freq
    is_ctx = i == 0
    cos = jnp.where(is_ctx, 1.0, jnp.cos(ang))
    sin = jnp.where(is_ctx, 0.0, jnp.sin(ang))
    first = jnp.bitwise_and(lane, HD // 4) == 0
    sin_signed = jnp.where(first, -sin, sin)

    def norm_rope(x, g, scale):
        y = x * lax.rsqrt(jnp.mean(x * x, axis=-1, keepdims=True) + EPS) * g
        partner = jnp.where(first, pltpu.roll(y, HD - HD // 4, 1), pltpu.roll(y, HD // 4, 1))
        return (y * cos + partner * sin_signed) * scale

    for h in range(H_AT // H_KV * H_KV):
        x = q_ref[:, h * HD:(h + 1) * HD].astype(f32)
        qo_ref[:, h * HD:(h + 1) * HD] = norm_rope(x, qg_ref[...], QK_SCALE).astype(bf16)
    for h in range(H_KV):
        x = k_ref[:, h * HD:(h + 1) * HD].astype(f32)
        ko_ref[:, h * HD:(h + 1) * HD] = norm_rope(x, kg_ref[...], 1.0).astype(bf16)


def _attn_prep(p, q_g, k_g):
    t = p.shape[0]
    wq, wk = H_AT * HD, H_KV * HD
    return pl.pallas_call(
        _attn_prep_kernel,
        grid=(t // BLK,),
        in_specs=[pl.BlockSpec((BLK, wq), lambda i: (i, C_ATQ // wq)),
                  pl.BlockSpec((BLK, wk), lambda i: (i, C_ATK // wk)),
                  pl.BlockSpec((1, HD), lambda i: (0, 0)),
                  pl.BlockSpec((1, HD), lambda i: (0, 0))],
        out_specs=[pl.BlockSpec((BLK, wq), lambda i: (i, 0)), pl.BlockSpec((BLK, wk), lambda i: (i, 0))],
        out_shape=[jax.ShapeDtypeStruct((t, wq), bf16), jax.ShapeDtypeStruct((t, wk), bf16)],
        compiler_params=_cparams(("parallel",), 40),
        name="attn_prep",
    )(p, p, q_g.reshape(1, HD), k_g.reshape(1, HD))


def _attn_kernel(tq, tk, n_ctx_tiles, n_kv_ctx, n_kv_all, q_ref, k_ref, v_ref, o_ref, m_scr, l_scr, acc_scr):
    qi = pl.program_id(1)
    grp = H_AT // H_KV
    m_scr[...] = jnp.full_like(m_scr, NEG)
    l_scr[...] = jnp.zeros_like(l_scr)
    acc_scr[...] = jnp.zeros_like(acc_scr)
    n_kv = jnp.where(qi < n_ctx_tiles, n_kv_ctx, n_kv_all)

    def body(j, carry):
        rows = pl.ds(pl.multiple_of(j * tk, tk), tk)
        kt = k_ref[rows, :]
        vt = v_ref[rows, :]
        for h in range(grp):
            s = _mm_nt(q_ref[:, h * HD:(h + 1) * HD], kt)
            m_prev = m_scr[h]
            m_cur = jnp.maximum(m_prev, jnp.max(s, axis=-1, keepdims=True))
            alpha = jnp.exp(m_prev - m_cur)
            p = jnp.exp(s - m_cur)
            l_scr[h] = alpha * l_scr[h] + jnp.sum(p, axis=-1, keepdims=True)
            acc_scr[h] = alpha * acc_scr[h] + _mm(p, vt)
            m_scr[h] = m_cur
        return carry

    lax.fori_loop(0, n_kv, body, 0)
    for h in range(grp):
        o_ref[:, h * HD:(h + 1) * HD] = (acc_scr[h] / l_scr[h]).astype(o_ref.dtype)


def _attention(qr, kr, p):
    t = qr.shape[0]
    tq, tk = 256, 256
    grp = H_AT // H_KV
    wg = grp * HD
    kern = functools.partial(_attn_kernel, tq, tk, BLK // tq, BLK // tk, t // tk)
    return pl.pallas_call(
        kern,
        grid=(H_KV, t // tq),
        in_specs=[pl.BlockSpec((tq, wg), lambda g, i: (i, g)),
                  pl.BlockSpec((t, HD), lambda g, i: (0, g)),
                  pl.BlockSpec((t, HD), lambda g, i: (0, C_ATV // HD + g))],
        out_specs=pl.BlockSpec((tq, wg), lambda g, i: (i, g)),
        out_shape=jax.ShapeDtypeStruct((t, H_AT * HD), bf16),
        scratch_shapes=[pltpu.VMEM((grp, tq, 1), f32), pltpu.VMEM((grp, tq, 1), f32),
                        pltpu.VMEM((grp, tq, HD), f32)],
        compiler_params=_cparams(("parallel", "arbitrary"), 48),
        name="flash_attention",
    )(qr, kr, p)


def _merge_kernel(tm, dnf_ref, dnb_ref, z_ref, at_ref, mlf_ref, mlb_ref, og_ref, x_ref, mod_ref,
                  dng_ref, mlg_ref, wo_ref, n2g_ref, rw1_ref, rw2_ref, xo_ref, h2_ref, lg_ref):
    i = pl.program_id(0)
    rows = i * tm + lax.broadcasted_iota(jnp.int32, (tm, 1), 0)
    is_ctx = rows < BLK
    w = N_HEADS_SCAN * HD

    def head_norm(x, g):
        return x * lax.rsqrt(jnp.mean(x * x, axis=-1, keepdims=True) + EPS) * g

    acc = jnp.dot(at_ref[...], wo_ref[w:w + H_AT * HD, :], preferred_element_type=f32)
    dn_parts, ml_parts = [], []
    for h in range(N_HEADS_SCAN):
        sl = slice(h * HD, (h + 1) * HD)
        z = z_ref[:, sl].astype(f32)
        dn_parts.append(head_norm(dnf_ref[:, sl] + dnb_ref[:, sl], dng_ref[...]) * (z * _sigmoid(z)))
        ml_parts.append(head_norm(mlf_ref[:, sl] + mlb_ref[:, sl], mlg_ref[...]) * _sigmoid(og_ref[:, sl].astype(f32)))
    dn = jnp.concatenate(dn_parts, axis=1).astype(bf16)
    ml = jnp.concatenate(ml_parts, axis=1).astype(bf16)
    acc = acc + jnp.dot(dn, wo_ref[0:w, :], preferred_element_type=f32)
    acc = acc + jnp.dot(ml, wo_ref[w + H_AT * HD:, :], preferred_element_type=f32)
    x = x_ref[...] + _mod_rows(mod_ref, 2, is_ctx) * acc
    xo_ref[...] = x
    y = x * lax.rsqrt(jnp.mean(x * x, axis=-1, keepdims=True) + EPS) * n2g_ref[...]
    h2 = y * (1.0 + _mod_rows(mod_ref, 4, is_ctx)) + _mod_rows(mod_ref, 3, is_ctx)
    h2_ref[...] = h2
    hh = h2.astype(bf16)
    hl = (h2 - hh.astype(f32)).astype(bf16)
    rw1 = rw1_ref[...]
    lg_ref[...] = (jnp.dot(hh, rw1, preferred_element_type=f32) + jnp.dot(hl, rw1, preferred_element_type=f32)
                   + jnp.dot(hh, rw2_ref[...], preferred_element_type=f32))


def _merge_outproj(dnf, dnb, p, at, mlf, mlb, x, mod, dn_g, ml_g, w_out, n2g, rw1, rw2):
    t = x.shape[0]
    tm = 256
    w = N_HEADS_SCAN * HD
    row = lambda i: (i, 0)
    full = lambda i: (0, 0)
    sw = pl.BlockSpec((tm, w), row)
    return pl.pallas_call(
        functools.partial(_merge_kernel, tm),
        grid=(t // tm,),
        in_specs=[sw, sw, pl.BlockSpec((tm, w), lambda i: (i, C_DNZ // w)),
                  pl.BlockSpec((tm, H_AT * HD), row), sw, sw,
                  pl.BlockSpec((tm, w), lambda i: (i, C_MLO // w)),
                  pl.BlockSpec((tm, D), row), pl.BlockSpec((8, 6 * D), full),
                  pl.BlockSpec((1, HD), full), pl.BlockSpec((1, HD), full),
                  pl.BlockSpec((D, D), full), pl.BlockSpec((1, D), full),
                  pl.BlockSpec((D, 128), full), pl.BlockSpec((D, 128), full)],
        out_specs=[pl.BlockSpec((tm, D), row), pl.BlockSpec((tm, D), row), pl.BlockSpec((tm, 128), row)],
        out_shape=[jax.ShapeDtypeStruct((t, D), f32), jax.ShapeDtypeStruct((t, D), f32),
                   jax.ShapeDtypeStruct((t, 128), f32)],
        compiler_params=_cparams(("parallel",), 48),
        name="merge_outproj",
    )(dnf, dnb, p, at, mlf, mlb, p, x, mod, dn_g.reshape(1, HD), ml_g.reshape(1, HD), w_out,
      n2g.reshape(1, D), rw1, rw2)


def _route_kernel(lg_ref, bias_ref, e_ref, g_ref):
    lt = lg_ref[...].T
    sc = [_sigmoid(lt[e:e + 1, :]) for e in range(N_EXPERTS)]
    bi = [sc[e] + bias_ref[e:e + 1, 0:1] for e in range(N_EXPERTS)]
    n_groups = N_EXPERTS // EXPERTS_PER_GROUP
    best, best_g = None, None
    for g in range(n_groups):
        a, b, c, d = bi[4 * g:4 * g + 4]
        gs = jnp.maximum(jnp.maximum(jnp.maximum(a + b, a + c), jnp.maximum(a + d, b + c)),
                         jnp.maximum(b + d, c + d))
        if g == 0:
            best, best_g = gs, jnp.zeros_like(gs, dtype=jnp.int32)
        else:
            better = gs > best
            best = jnp.where(better, gs, best)
            best_g = jnp.where(better, g, best_g)
    t1 = jnp.full_like(best, -jnp.inf)
    t2 = jnp.full_like(best, -jnp.inf)
    i1 = jnp.zeros_like(best_g)
    i2 = jnp.zeros_like(best_g)
    s1 = jnp.zeros_like(best)
    s2 = jnp.zeros_like(best)
    for e in range(N_EXPERTS):
        v = jnp.where(best_g == e // EXPERTS_PER_GROUP, bi[e], -jnp.inf)
        gt1 = v > t1
        gt2 = jnp.logical_and(jnp.logical_not(gt1), v > t2)
        t2 = jnp.where(gt1, t1, jnp.where(gt2, v, t2))
        i2 = jnp.where(gt1, i1, jnp.where(gt2, e, i2))
        s2 = jnp.where(gt1, s1, jnp.where(gt2, sc[e], s2))
        t1 = jnp.where(gt1, v, t1)
        i1 = jnp.where(gt1, e, i1)
        s1 = jnp.where(gt1, sc[e], s1)
    tot = s1 + s2
    zi = jnp.zeros_like(i1)
    zf = jnp.zeros_like(s1)
    e_ref[...] = jnp.concatenate([i1, i2, zi, zi, zi, zi, zi, zi], axis=0)
    g_ref[...] = jnp.concatenate([s1 / tot, s2 / tot, zf, zf, zf, zf, zf, zf], axis=0)


def _route(logits, router_bias):
    t = logits.shape[0]
    tm = 256
    bias = jnp.zeros((N_EXPERTS, 128), f32).at[:, 0].set(router_bias)
    return pl.pallas_call(
        _route_kernel,
        grid=(t // tm,),
        in_specs=[pl.BlockSpec((tm, 128), lambda i: (i, 0)), pl.BlockSpec((N_EXPERTS, 128), lambda i: (0, 0))],
        out_specs=[pl.BlockSpec((8, tm), lambda i: (0, i)), pl.BlockSpec((8, tm), lambda i: (0, i))],
        out_shape=[jax.ShapeDtypeStruct((8, t), jnp.int32), jax.ShapeDtypeStruct((8, t), f32)],
        compiler_params=_cparams(("parallel",)),
        name="route_top2",
    )(logits, bias)


def _dispatch(e_rows, g_rows, n_blocks):
    t = e_rows.shape[1]
    n = 2 * t
    flat_e = e_rows[0:2].T.reshape(n)
    flat_w = g_rows[0:2].T.reshape(n)
    order = jnp.argsort(flat_e, stable=True).astype(jnp.int32)
    counts = jnp.sum(flat_e[:, None] == jnp.arange(N_EXPERTS, dtype=jnp.int32)[None, :], axis=0).astype(jnp.int32)
    starts = jnp.cumsum(counts) - counts
    padded = (counts + MOE_BM - 1) // MOE_BM * MOE_BM
    p_ends = jnp.cumsum(padded)
    p_starts = p_ends - padded
    blk_e = jnp.minimum(jnp.searchsorted(p_ends, jnp.arange(n_blocks, dtype=jnp.int32) * MOE_BM, side="right"),
                        N_EXPERTS - 1).astype(jnp.int32)
    slot = jnp.arange(n_blocks * MOE_BM, dtype=jnp.int32)
    e_of = jnp.repeat(blk_e, MOE_BM)
    rank = slot - p_starts[e_of]
    valid = jnp.logical_and(rank >= 0, rank < counts[e_of])
    src = order[jnp.clip(starts[e_of] + rank, 0, n - 1)]
    tok = jnp.where(valid, src // 2, -1).astype(jnp.int32)
    dst = jnp.where(valid, (src % 2) * t + src // 2, -1).astype(jnp.int32)
    wt = jnp.where(valid, flat_w[src], 0.0)
    return (blk_e, tok.reshape(n_blocks, 1, MOE_BM), dst.reshape(n_blocks, 1, MOE_BM),
            wt.reshape(n_blocks, MOE_BM, 1))


def _moe_kernel(blk_e_ref, tok_ref, dst_ref, wt_ref, h_hbm, wg_ref, wu_ref, wd_ref, out_hbm, xb, yb, sems):
    del blk_e_ref

    def gather(r):
        tok = jnp.maximum(tok_ref[0, 0, r], 0)
        return pltpu.make_async_copy(h_hbm.at[pl.ds(tok, 1)], xb.at[pl.ds(r, 1)], sems.at[0])

    def scatter(r):
        return pltpu.make_async_copy(yb.at[pl.ds(r, 1)], out_hbm.at[pl.ds(jnp.maximum(dst_ref[0, 0, r], 0), 1)],
                                     sems.at[1])

    def start_gather(r, c):
        gather(r).start()
        return c

    def wait_gather(r, c):
        gather(r).wait()
        return c

    lax.fori_loop(0, MOE_BM, start_gather, 0)
    lax.fori_loop(0, MOE_BM, wait_gather, 0)
    x = xb[...].astype(bf16)
    g = jnp.dot(x, wg_ref[0], preferred_element_type=f32)
    u = jnp.dot(x, wu_ref[0], preferred_element_type=f32)
    a = (g * _sigmoid(g) * u).astype(bf16)
    yb[...] = jnp.dot(a, wd_ref[0], preferred_element_type=f32) * wt_ref[0]

    def start_scatter(r, c):
        @pl.when(dst_ref[0, 0, r] >= 0)
        def _():
            scatter(r).start()
        return c

    def wait_scatter(r, c):
        @pl.when(dst_ref[0, 0, r] >= 0)
        def _():
            scatter(r).wait()
        return c

    lax.fori_loop(0, MOE_BM, start_scatter, 0)
    lax.fori_loop(0, MOE_BM, wait_scatter, 0)


def _moe(h2, blk_e, tok, dst, wt, wg, wu, wd):
    t = h2.shape[0]
    nb = tok.shape[0]
    grid_spec = pltpu.PrefetchScalarGridSpec(
        num_scalar_prefetch=1,
        grid=(nb,),
        in_specs=[pl.BlockSpec((1, 1, MOE_BM), lambda b, e: (b, 0, 0), memory_space=pltpu.SMEM),
                  pl.BlockSpec((1, 1, MOE_BM), lambda b, e: (b, 0, 0), memory_space=pltpu.SMEM),
                  pl.BlockSpec((1, MOE_BM, 1), lambda b, e: (b, 0, 0)),
                  pl.BlockSpec(memory_space=pl.ANY),
                  pl.BlockSpec((1, D, D_EXPERT), lambda b, e: (e[b], 0, 0)),
                  pl.BlockSpec((1, D, D_EXPERT), lambda b, e: (e[b], 0, 0)),
                  pl.BlockSpec((1, D_EXPERT, D), lambda b, e: (e[b], 0, 0))],
        out_specs=pl.BlockSpec(memory_space=pl.ANY),
        scratch_shapes=[pltpu.VMEM((MOE_BM, D), f32), pltpu.VMEM((MOE_BM, D), f32),
                        pltpu.SemaphoreType.DMA((2,))],
    )
    return pl.pallas_call(
        _moe_kernel,
        grid_spec=grid_spec,
        out_shape=jax.ShapeDtypeStruct((2 * t, D), f32),
        compiler_params=_cparams(("arbitrary",), 48),
        name="moe_experts",
    )(blk_e, tok, dst, wt, h2, wg, wu, wd)


def _final_kernel(x_ref, y0_ref, y1_ref, mod_ref, g_ref, o_ref):
    x = x_ref[...] + mod_ref[0:1, 5 * D:6 * D] * (y0_ref[...] + y1_ref[...])
    o_ref[...] = x * lax.rsqrt(jnp.mean(x * x, axis=-1, keepdims=True) + EPS) * g_ref[...]


def _final(x, moe, mod, g):
    t = x.shape[0]
    tm = 256
    nrow = t // tm
    nctx = BLK // tm
    return pl.pallas_call(
        _final_kernel,
        grid=(nrow - nctx,),
        in_specs=[pl.BlockSpec((tm, D), lambda i: (i + nctx, 0)),
                  pl.BlockSpec((tm, D), lambda i: (i + nctx, 0)),
                  pl.BlockSpec((tm, D), lambda i: (i + nctx + nrow, 0)),
                  pl.BlockSpec((8, 6 * D), lambda i: (0, 0)),
                  pl.BlockSpec((1, D), lambda i: (0, 0))],
        out_specs=pl.BlockSpec((tm, D), lambda i: (i, 0)),
        out_shape=jax.ShapeDtypeStruct((t - BLK, D), f32),
        compiler_params=_cparams(("parallel",), 40),
        name="final_norm",
    )(x, moe, moe, mod, g.reshape(1, D))


def _prep_in_weights(w_in):
    splits = (1536, 512, 8, 8, 1024, 256, 256, 512, 512, 512, 512, 8, 8)
    offs = [0]
    for s in splits:
        offs.append(offs[-1] + s)
    part = lambda i: w_in[:, offs[i]:offs[i + 1]]
    main = jnp.concatenate([part(i) for i in (0, 1, 4, 5, 6, 7, 8, 9, 10)], axis=1).astype(bf16)
    gates = jnp.concatenate([part(i) for i in (2, 3, 11, 12)], axis=1)
    gates = jnp.pad(gates, ((0, 0), (0, 128 - gates.shape[1])))
    g1 = gates.astype(bf16)
    g2 = (gates - g1.astype(f32)).astype(bf16)
    return main, g1, g2


def _layer(l, last, x, moe_prev, mods, norm1_g, norm2_g, w_in, dn_conv, dn_a_log, dn_dt_bias, dn_norm_g,
           q_norm_g, k_norm_g, ml_i_bias, ml_f_bias, ml_norm_g, w_out, rw1, rw2, router_bias,
           w_gate, w_up, w_down):
    del last
    t = x.shape[0]
    w_main, wg1, wg2 = _prep_in_weights(w_in[l])
    x, h, graw = _norm1(x, moe_prev, mods[l - 1] if l else None, mods[l], norm1_g[l], wg1, wg2)
    p = _matmul(h, w_main, bf16)
    conv_w = jnp.pad(dn_conv[l], ((0, 8 - CONV_W), (0, 0)))
    gate_params = jnp.zeros((8, 128), f32)
    gate_params = gate_params.at[0, 8:16].set(dn_dt_bias[l].reshape(8))
    gate_params = gate_params.at[0, 16:24].set(ml_i_bias[l].reshape(8))
    gate_params = gate_params.at[0, 24:32].set(ml_f_bias[l].reshape(8))
    gate_params = gate_params.at[1, 8:16].set(dn_a_log[l].reshape(8))
    dnq, gcol, grow = _scan_prep(p, graw, conv_w, gate_params)
    dnf, dnb = _gdn_scan(dnq, gcol, grow)
    mlf, mlb = _mlstm_scan(p, gcol, grow)
    qr, kr = _attn_prep(p, q_norm_g[l], k_norm_g[l])
    at = _attention(qr, kr, p)
    x, h2, logits = _merge_outproj(dnf, dnb, p, at, mlf, mlb, x, mods[l], dn_norm_g[l], ml_norm_g[l],
                                   w_out[l].astype(bf16), norm2_g[l], rw1, rw2)
    e_rows, g_rows = _route(logits, router_bias)
    n_blocks = (2 * t + N_EXPERTS * (MOE_BM - 1) + MOE_BM - 1) // MOE_BM
    blk_e, tok, dst, wt = _dispatch(e_rows, g_rows, n_blocks)
    moe = _moe(h2, blk_e, tok, dst, wt, w_gate[l].astype(bf16), w_up[l].astype(bf16), w_down[l].astype(bf16))
    return x, moe


def kernel(x, c, ctx, c_ctx, w_mod, b_mod, norm1_g, norm2_g, w_in, dn_conv, dn_a_log, dn_dt_bias, dn_norm_g, q_norm_g, k_norm_g, ml_i_bias, ml_f_bias, ml_norm_g, w_out, router_w, router_bias, w_gate, w_up, w_down, final_norm_g):
    b, seq, d = x.shape
    assert b == 1 and d == D and ctx.shape[1] == BLK and seq % BLK == 0 and seq % GRID_W == 0
    depth = w_mod.shape[0]
    mods = _mods(c, c_ctx, w_mod, b_mod)
    xs = jnp.concatenate([ctx[0], x[0]], axis=0)
    rw = jnp.pad(router_w, ((0, 0), (0, 128 - N_EXPERTS)))
    rw1 = rw.astype(bf16)
    rw2 = (rw - rw1.astype(f32)).astype(bf16)
    moe = None
    for l in range(depth):
        xs, moe = _layer(l, l == depth - 1, xs, moe, mods, norm1_g, norm2_g, w_in, dn_conv, dn_a_log, dn_dt_bias,
                         dn_norm_g, q_norm_g, k_norm_g, ml_i_bias, ml_f_bias, ml_norm_g, w_out, rw1, rw2,
                         router_bias, w_gate, w_up, w_down)
    out = _final(xs, moe, mods[depth - 1], final_norm_g)
    return out.reshape(b, seq, d)
```

```python
import functools

import jax
import jax.numpy as jnp
from jax import lax
from jax.experimental import pallas as pl
from jax.experimental.pallas import tpu as pltpu

f32 = jnp.float32
bf16 = jnp.bfloat16

D = 2048
HD = 128
N_HEADS_SCAN = 4
H_AT = 8
H_KV = 2
CHUNK = 64
BLK = 256
CPB = BLK // CHUNK
GRID_W = 64
ROPE_THETA = 10000.0
QK_SCALE = HD ** -0.5
LOG2E = 1.4426950408889634
N_EXPERTS = 16
EXPERTS_PER_GROUP = 4
D_EXPERT = D // 2
MOE_BM = 256
EPS = 1e-6
NEG = -1e30
CONV_W = 5

C_DNQKV, C_DNZ, C_ATQ, C_ATK, C_ATV, C_MLQ, C_MLK, C_MLV, C_MLO, P_COLS = (
    0, 1536, 2048, 3072, 3328, 3584, 4096, 4608, 5120, 5632)

V7X_VMEM_LIMIT_MB = 56


def _cparams(sems, vmem_mb=None):
    return pltpu.CompilerParams(
        dimension_semantics=sems,
        vmem_limit_bytes=None if vmem_mb is None else vmem_mb << 20)


def _mm(a, b):
    return jnp.dot(a.astype(bf16), b.astype(bf16), preferred_element_type=f32)


def _mm_nt(a, b):
    return lax.dot_general(a.astype(bf16), b.astype(bf16), (((1,), (1,)), ((), ())),
                           preferred_element_type=f32)


def _mm_tn(a, b):
    return lax.dot_general(a.astype(bf16), b.astype(bf16), (((0,), (0,)), ((), ())),
                           preferred_element_type=f32)


def _split3(x):
    x1 = x.astype(bf16)
    r1 = x - x1.astype(f32)
    x2 = r1.astype(bf16)
    x3 = (r1 - x2.astype(f32)).astype(bf16)
    return x1, x2, x3


def _sigmoid(x):
    return 1.0 / (1.0 + jnp.exp(-x))


def _softplus(x):
    return jnp.maximum(x, 0.0) + jnp.log(1.0 + jnp.exp(-jnp.abs(x)))


def _mod_kernel(s_ref, w_ref, b_ref, o_ref):
    s = s_ref[...]
    s = s * _sigmoid(s)
    o_ref[0] = jnp.dot(s, w_ref[0], preferred_element_type=f32,
                       precision=lax.Precision.HIGHEST) + b_ref[0]


def _mods(c, c_ctx, w_mod, b_mod):
    depth, d, n6 = w_mod.shape
    s = jnp.zeros((8, d), f32).at[0].set(c[0]).at[1].set(c_ctx)
    tn = 1024
    return pl.pallas_call(
        _mod_kernel,
        grid=(depth, n6 // tn),
        in_specs=[pl.BlockSpec((8, d), lambda l, j: (0, 0)),
                  pl.BlockSpec((1, d, tn), lambda l, j: (l, 0, j)),
                  pl.BlockSpec((1, 1, tn), lambda l, j: (l, 0, j))],
        out_specs=pl.BlockSpec((1, 8, tn), lambda l, j: (l, 0, j)),
        out_shape=jax.ShapeDtypeStruct((depth, 8, n6), f32),
        compiler_params=_cparams(("parallel", "parallel"), 40),
        name="mod_vectors",
    )(s, w_mod, b_mod.reshape(depth, 1, n6))


def _mod_rows(mod_ref, k, is_ctx):
    lat = mod_ref[0:1, k * D:(k + 1) * D]
    ctx = mod_ref[1:2, k * D:(k + 1) * D]
    return jnp.where(is_ctx, ctx, lat)


def _norm1_kernel(has_moe, tm, *refs):
    if has_moe:
        x_ref, y0_ref, y1_ref, modp_ref, mod_ref, g_ref, wg1_ref, wg2_ref, xo_ref, h_ref, gr_ref = refs
    else:
        x_ref, mod_ref, g_ref, wg1_ref, wg2_ref, h_ref, gr_ref = refs
    i = pl.program_id(0)
    rows = i * tm + lax.broadcasted_iota(jnp.int32, (tm, 1), 0)
    is_ctx = rows < BLK
    x = x_ref[...]
    if has_moe:
        x = x + _mod_rows(modp_ref, 5, is_ctx) * (y0_ref[...] + y1_ref[...])
        xo_ref[...] = x
    ms = jnp.mean(x * x, axis=-1, keepdims=True)
    y = x * lax.rsqrt(ms + EPS) * g_ref[...]
    h = y * (1.0 + _mod_rows(mod_ref, 1, is_ctx)) + _mod_rows(mod_ref, 0, is_ctx)
    hh = h.astype(bf16)
    h_ref[...] = hh
    hl = (h - hh.astype(f32)).astype(bf16)
    wg1 = wg1_ref[...]
    gr_ref[...] = (jnp.dot(hh, wg1, preferred_element_type=f32)
                   + jnp.dot(hl, wg1, preferred_element_type=f32)
                   + jnp.dot(hh, wg2_ref[...], preferred_element_type=f32))


def _norm1(x, moe, mod_prev, mod_cur, g, wg1, wg2):
    t = x.shape[0]
    tm = 256
    nrow = t // tm
    row = lambda i: (i, 0)
    full = lambda i: (0, 0)
    in_specs = [pl.BlockSpec((tm, D), row)]
    args = [x]
    if moe is not None:
        in_specs += [pl.BlockSpec((tm, D), row), pl.BlockSpec((tm, D), lambda i: (i + nrow, 0)),
                     pl.BlockSpec((8, 6 * D), full)]
        args += [moe, moe, mod_prev]
    in_specs += [pl.BlockSpec((8, 6 * D), full), pl.BlockSpec((1, D), full),
                 pl.BlockSpec((D, 128), full), pl.BlockSpec((D, 128), full)]
    args += [mod_cur, g.reshape(1, D), wg1, wg2]
    out_specs = [pl.BlockSpec((tm, D), row), pl.BlockSpec((tm, 128), row)]
    out_shape = [jax.ShapeDtypeStruct((t, D), bf16), jax.ShapeDtypeStruct((t, 128), f32)]
    if moe is not None:
        out_specs = [pl.BlockSpec((tm, D), row)] + out_specs
        out_shape = [jax.ShapeDtypeStruct((t, D), f32)] + out_shape
    outs = pl.pallas_call(
        functools.partial(_norm1_kernel, moe is not None, tm),
        grid=(nrow,), in_specs=in_specs, out_specs=out_specs, out_shape=out_shape,
        compiler_params=_cparams(("parallel",), 40),
        name="norm1_modulate",
    )(*args)
    if moe is not None:
        return outs
    return [x] + list(outs)


def _matmul_kernel(a_ref, b_ref, o_ref):
    o_ref[...] = jnp.dot(a_ref[...], b_ref[...], preferred_element_type=f32).astype(o_ref.dtype)


def _row_tile(t, choices):
    for c in choices:
        if t % c == 0:
            return c
    raise ValueError(f"no row tile for {t}")


def _matmul(a, b, out_dtype):
    m, k = a.shape
    n = b.shape[1]
    tm = _row_tile(m, (1280, 768, 512, 256))
    tn = _row_tile(n, (1408, 1024, 512))
    return pl.pallas_call(
        _matmul_kernel,
        grid=(n // tn, m // tm),
        in_specs=[pl.BlockSpec((tm, k), lambda j, i: (i, 0)),
                  pl.BlockSpec((k, tn), lambda j, i: (0, j))],
        out_specs=pl.BlockSpec((tm, tn), lambda j, i: (i, j)),
        out_shape=jax.ShapeDtypeStruct((m, n), out_dtype),
        compiler_params=_cparams(("parallel", "parallel"), 48),
        name="in_projection",
    )(a, b)


def _scan_prep_kernel(nblk, cur_ref, prev_ref, next_ref, cw_ref, graw_ref, gp_ref, q_ref, gc_ref, grow_ref, xs):
    i = pl.program_id(0)
    has_prev = i >= 2
    has_next = jnp.logical_and(i >= 1, i < nblk - 1)
    xs[0:16, :] = jnp.where(has_prev, prev_ref[...].astype(f32), 0.0)
    xs[16:16 + BLK, :] = cur_ref[...].astype(f32)
    xs[16 + BLK:32 + BLK, :] = jnp.where(has_next, next_ref[...].astype(f32), 0.0)
    acc = cw_ref[0:1, :] * xs[pl.ds(16 - CONV_W // 2, BLK), :]
    for j in range(1, CONV_W):
        acc = acc + cw_ref[j:j + 1, :] * xs[pl.ds(16 - CONV_W // 2 + j, BLK), :]
    a = acc * _sigmoid(acc)
    w = N_HEADS_SCAN * HD
    for h in range(2 * N_HEADS_SCAN):
        xh = a[:, h * HD:(h + 1) * HD]
        inv = lax.rsqrt(jnp.sum(xh * xh, axis=-1, keepdims=True) + EPS)
        scale = QK_SCALE if h < N_HEADS_SCAN else 1.0
        q_ref[:, h * HD:(h + 1) * HD] = (xh * (inv * scale)).astype(bf16)
    q_ref[:, 2 * w:3 * w] = a[:, 2 * w:3 * w].astype(bf16)

    z = graw_ref[...] + gp_ref[0:1, :]
    lane = lax.broadcasted_iota(jnp.int32, (BLK, 128), 1)
    sp = _softplus(z)
    vals = jnp.where(lane < 8, _sigmoid(z),
                     jnp.where(lane < 16, -jnp.exp(gp_ref[1:2, :]) * sp,
                               jnp.where(lane < 24, z, z - sp)))
    r = lax.broadcasted_iota(jnp.int32, (BLK, BLK), 0)
    c = lax.broadcasted_iota(jnp.int32, (BLK, BLK), 1)
    same = jnp.right_shift(r, 6) == jnp.right_shift(c, 6)
    tri_lo = jnp.where(jnp.logical_and(same, r >= c), 1.0, 0.0).astype(bf16)
    tri_up = jnp.where(jnp.logical_and(same, r <= c), 1.0, 0.0).astype(bf16)
    v1, v2, v3 = _split3(vals)
    dot = functools.partial(jnp.dot, preferred_element_type=f32)
    prefix = dot(tri_lo, v1) + dot(tri_lo, v2) + dot(tri_lo, v3)
    suffix = dot(tri_up, v1) + dot(tri_up, v2) + dot(tri_up, v3)
    is_cum = jnp.logical_and(jnp.bitwise_and(lane, 8) == 8, lane < 32)
    is_bwd = jnp.bitwise_and(lane, 4) == 4
    out = jnp.where(is_cum, jnp.where(is_bwd, suffix, prefix), vals)
    gc_ref[...] = out
    gt = out.T
    for cc in range(CPB):
        grow_ref[cc] = gt[0:32, cc * CHUNK:(cc + 1) * CHUNK]


def _scan_prep(p, graw, conv_w, gate_params):
    t = p.shape[0]
    nblk = t // BLK
    wq = 3 * N_HEADS_SCAN * HD
    n16 = t // 16
    return pl.pallas_call(
        functools.partial(_scan_prep_kernel, nblk),
        grid=(nblk,),
        in_specs=[pl.BlockSpec((BLK, wq), lambda i: (i, 0)),
                  pl.BlockSpec((16, wq), lambda i: (jnp.maximum(i * (BLK // 16) - 1, 0), 0)),
                  pl.BlockSpec((16, wq), lambda i: (jnp.minimum((i + 1) * (BLK // 16), n16 - 1), 0)),
                  pl.BlockSpec((8, wq), lambda i: (0, 0)),
                  pl.BlockSpec((BLK, 128), lambda i: (i, 0)),
                  pl.BlockSpec((8, 128), lambda i: (0, 0))],
        out_specs=[pl.BlockSpec((BLK, wq), lambda i: (i, 0)),
                   pl.BlockSpec((BLK, 128), lambda i: (i, 0)),
                   pl.BlockSpec((CPB, 32, CHUNK), lambda i: (i, 0, 0))],
        out_shape=[jax.ShapeDtypeStruct((t, wq), bf16),
                   jax.ShapeDtypeStruct((t, 128), f32),
                   jax.ShapeDtypeStruct((t // CHUNK, 32, CHUNK), f32)],
        scratch_shapes=[pltpu.VMEM((BLK + 32, wq), f32)],
        compiler_params=_cparams(("parallel",), 40),
        name="scan_prep",
    )(p, p, p, conv_w, graw, gate_params)


def _tri_masks():
    r = lax.broadcasted_iota(jnp.int32, (CHUNK, CHUNK), 0)
    c = lax.broadcasted_iota(jnp.int32, (CHUNK, CHUNK), 1)
    blk = jnp.right_shift(r, 4) == jnp.right_shift(c, 4)
    eye = jnp.where(r == c, 1.0, 0.0)
    return (r >= c, r <= c), (r > c, r < c), blk, eye


def _gdn_kernel(qf_ref, qb_ref, gcf_ref, gcb_ref, grf_ref, grb_ref, of_ref, ob_ref, s_scr):
    @pl.when(pl.program_id(0) == 0)
    def _():
        s_scr[...] = jnp.zeros_like(s_scr)

    incl, strict, blk, eye = _tri_masks()
    w = N_HEADS_SCAN * HD

    units = [(d, h) for d in range(2) for h in range(N_HEADS_SCAN)]
    rows_cat = lambda a, b: jnp.concatenate([a, b], axis=0)
    cols_cat = lambda a, b: jnp.concatenate([a, b], axis=1)
    C = CHUNK

    def chunk(cc, carry):
        ld = []
        for d, h in units:
            c = cc if d == 0 else CPB - 1 - cc
            q_ref, gc_ref, gr_ref = (qf_ref, gcf_ref, grf_ref) if d == 0 else (qb_ref, gcb_ref, grb_ref)
            rows = pl.ds(pl.multiple_of(c * C, C), C)
            u_idx = d * N_HEADS_SCAN + h
            q = q_ref[rows, h * HD:(h + 1) * HD]
            k = q_ref[rows, w + h * HD:w + (h + 1) * HD]
            v = q_ref[rows, 2 * w + h * HD:2 * w + (h + 1) * HD]
            beta = gc_ref[rows, u_idx:u_idx + 1]
            cum_c = gc_ref[rows, 8 + u_idx:9 + u_idx]
            cum_r = gr_ref[c][8 + u_idx:9 + u_idx, :]
            tot = cum_c[C - 1:C, :] if d == 0 else cum_c[0:1, :]
            ld.append((rows, q, k, v, beta, cum_c, cum_r, tot))
        g1 = [_mm_nt(rows_cat(k, q), k) for (_, q, k, *_) in ld]
        st = []
        for (d, h), (rows, q, k, v, beta, cum_c, cum_r, tot), g in zip(units, ld, g1):
            decay = jnp.exp(jnp.where(incl[d], cum_c - cum_r, NEG))
            nm = jnp.where(strict[d], beta * g[:C] * decay, 0.0)
            dm = jnp.where(blk, nm, 0.0)
            kf = k.astype(f32)
            e_c = jnp.exp(cum_c)
            rhs = cols_cat(cols_cat((beta * e_c) * kf, beta * v.astype(f32)), nm - dm)
            st.append(dict(dm=dm, rhs=rhs, qk=g[C:] * decay, k_dec=kf * jnp.exp(tot - cum_c),
                           q_dec=q.astype(f32) * e_c, g_last=jnp.exp(tot), p1=eye - dm))
        m2 = [_mm(s["dm"], s["dm"]) for s in st]
        r = [_mm(rows_cat(s["p1"], m), m) for s, m in zip(st, m2)]
        p2 = [s["p1"] + x[:C] for s, x in zip(st, r)]
        m4 = [x[C:] for x in r]
        r = [_mm(rows_cat(p, m), m) for p, m in zip(p2, m4)]
        p3 = [p + x[:C] for p, x in zip(p2, r)]
        m8 = [x[C:] for x in r]
        dinv = [p + _mm(p, m) for p, m in zip(p3, m8)]
        r = [_mm(di, s["rhs"]) for di, s in zip(dinv, st)]
        t1 = [x[:, :2 * HD] for x in r]
        qm = [x[:, 2 * HD:] for x in r]
        r = [_mm(qq, cols_cat(t, qq)) for qq, t in zip(qm, t1)]
        a1 = [x[:, :2 * HD] for x in r]
        qm2 = [x[:, 2 * HD:] for x in r]
        b2 = [_mm(q2, t) for q2, t in zip(qm2, t1)]
        c3 = [_mm(qq, b) for qq, b in zip(qm, b2)]
        sol = [t - a + b - c for t, a, b, c in zip(t1, a1, b2, c3)]
        s_old = [s_scr[i] for i in range(len(units))]
        r = [_mm(rows_cat(x[:, :HD], s["q_dec"]), so) for x, s, so in zip(sol, st, s_old)]
        u = [x[:, HD:] - y[:C] for x, y in zip(sol, r)]
        o_intra = [_mm(s["qk"], uu) for s, uu in zip(st, u)]
        s_add = [_mm_tn(s["k_dec"], uu) for s, uu in zip(st, u)]
        for i, ((d, h), l) in enumerate(zip(units, ld)):
            o_ref = of_ref if d == 0 else ob_ref
            o_ref[l[0], h * HD:(h + 1) * HD] = r[i][C:] + o_intra[i]
            s_scr[i] = st[i]["g_last"] * s_old[i] + s_add[i]
        return carry

    lax.fori_loop(0, CPB, chunk, 0)


def _bwd_block(nblk):
    return lambda s: (jnp.where(s == 0, 0, nblk - s), 0)


def _gdn_scan(qkv, gcol, grow):
    t = qkv.shape[0]
    nblk = t // BLK
    wq = 3 * N_HEADS_SCAN * HD
    w = N_HEADS_SCAN * HD
    fwd = lambda s: (s, 0)
    bwd = _bwd_block(nblk)
    fwd3 = lambda s: (s, 0, 0)
    bwd3 = lambda s: (jnp.where(s == 0, 0, nblk - s), 0, 0)
    return pl.pallas_call(
        _gdn_kernel,
        grid=(nblk,),
        in_specs=[pl.BlockSpec((BLK, wq), fwd), pl.BlockSpec((BLK, wq), bwd),
                  pl.BlockSpec((BLK, 128), fwd), pl.BlockSpec((BLK, 128), bwd),
                  pl.BlockSpec((CPB, 32, CHUNK), fwd3), pl.BlockSpec((CPB, 32, CHUNK), bwd3)],
        out_specs=[pl.BlockSpec((BLK, w), fwd), pl.BlockSpec((BLK, w), bwd)],
        out_shape=[jax.ShapeDtypeStruct((t, w), f32), jax.ShapeDtypeStruct((t, w), f32)],
        scratch_shapes=[pltpu.VMEM((2 * N_HEADS_SCAN, HD, HD), f32)],
        compiler_params=_cparams(("arbitrary",), 40),
        name="gdn_scan",
    )(qkv, qkv, gcol, gcol, grow, grow)


def _mlstm_kernel(pf_q, pf_k, pf_v, pb_q, pb_k, pb_v, gcf_ref, gcb_ref, grf_ref, grb_ref,
                  of_ref, ob_ref, c_scr, n_scr, m_scr):
    @pl.when(pl.program_id(0) == 0)
    def _():
        c_scr[...] = jnp.zeros_like(c_scr)
        n_scr[...] = jnp.zeros_like(n_scr)
        m_scr[...] = jnp.full_like(m_scr, NEG)

    incl, _, _, _ = _tri_masks()

    units = [(d, h) for d in range(2) for h in range(N_HEADS_SCAN)]

    def chunk(cc, carry):
        ld = []
        for d, h in units:
            c = cc if d == 0 else CPB - 1 - cc
            q_ref, k_ref, v_ref, gc_ref, gr_ref = (
                (pf_q, pf_k, pf_v, gcf_ref, grf_ref) if d == 0 else (pb_q, pb_k, pb_v, gcb_ref, grb_ref))
            rows = pl.ds(pl.multiple_of(c * CHUNK, CHUNK), CHUNK)
            u_idx = d * N_HEADS_SCAN + h
            q = q_ref[rows, h * HD:(h + 1) * HD]
            k = k_ref[rows, h * HD:(h + 1) * HD]
            v = v_ref[rows, h * HD:(h + 1) * HD]
            i_c = gc_ref[rows, 16 + u_idx:17 + u_idx]
            b_c = gc_ref[rows, 24 + u_idx:25 + u_idx]
            grow = gr_ref[c]
            i_r = grow[16 + u_idx:17 + u_idx, :]
            b_r = grow[24 + u_idx:25 + u_idx, :]
            b_last = b_c[CHUNK - 1:CHUNK, :] if d == 0 else b_c[0:1, :]
            w_log = b_last - b_c + i_c
            m_st = jnp.max(w_log, axis=0, keepdims=True)
            e_w = jnp.exp(w_log - m_st)
            ld.append(dict(rows=rows, q=q, k=k, v=v, i_r=i_r, b_c=b_c, b_r=b_r, b_last=b_last, m_st=m_st, e_w=e_w))
        qk = [_mm_nt(l["q"], l["k"]) for l in ld]
        kv = [_mm_tn(l["k"], l["e_w"] * l["v"].astype(f32)) for l in ld]
        c_old = [c_scr[i] for i in range(len(units))]
        qc = [_mm(l["q"], cm) for l, cm in zip(ld, c_old)]
        ps, m_locs = [], []
        for (d, h), l, g in zip(units, ld, qk):
            d_log = jnp.where(incl[d], l["b_c"] - l["b_r"] + l["i_r"], NEG)
            m_loc = jnp.max(d_log, axis=-1, keepdims=True)
            ps.append(jnp.exp(d_log - m_loc) * (g * QK_SCALE))
            m_locs.append(m_loc)
        num_loc = [_mm(p, l["v"]) for p, l in zip(ps, ld)]
        for i, ((d, h), l) in enumerate(zip(units, ld)):
            o_ref = of_ref if d == 0 else ob_ref
            p, m_loc = ps[i], m_locs[i]
            den_loc = jnp.sum(p, axis=-1, keepdims=True)
            nk = jnp.sum(l["k"].astype(f32) * l["e_w"], axis=0, keepdims=True) * QK_SCALE
            n_vec = n_scr[i][0:1, :]
            m = m_scr[i][0:1, 0:1]
            inter = l["b_c"] + m
            m_r = jnp.maximum(inter, m_loc)
            a_in = jnp.exp(inter - m_r)
            a_lo = jnp.exp(m_loc - m_r)
            num = a_in * qc[i] + a_lo * num_loc[i]
            den = a_in * jnp.sum(l["q"].astype(f32) * n_vec, axis=-1, keepdims=True) + a_lo * den_loc
            o_ref[l["rows"], h * HD:(h + 1) * HD] = num / jnp.maximum(jnp.abs(den), jnp.exp(-m_r))
            m_new = jnp.maximum(l["b_last"] + m, l["m_st"])
            s_old = jnp.exp(l["b_last"] + m - m_new)
            s_new = jnp.exp(l["m_st"] - m_new)
            c_scr[i] = s_old * c_old[i] + (s_new * QK_SCALE) * kv[i]
            n_scr[i] = jnp.broadcast_to(s_old * n_vec + s_new * nk, (8, HD))
            m_scr[i] = jnp.broadcast_to(m_new, (8, HD))
        return carry

    lax.fori_loop(0, CPB, chunk, 0)


def _mlstm_scan(p, gcol, grow):
    t = p.shape[0]
    nblk = t // BLK
    w = N_HEADS_SCAN * HD
    nu = 2 * N_HEADS_SCAN
    fwd = lambda s: (s, 0)
    bwd = _bwd_block(nblk)
    fwd3 = lambda s: (s, 0, 0)
    bwd3 = lambda s: (jnp.where(s == 0, 0, nblk - s), 0, 0)

    def col(base, bwd_dir):
        cb = base // w
        if bwd_dir:
            return pl.BlockSpec((BLK, w), lambda s: (jnp.where(s == 0, 0, nblk - s), cb))
        return pl.BlockSpec((BLK, w), lambda s: (s, cb))

    return pl.pallas_call(
        _mlstm_kernel,
        grid=(nblk,),
        in_specs=[col(C_MLQ, False), col(C_MLK, False), col(C_MLV, False),
                  col(C_MLQ, True), col(C_MLK, True), col(C_MLV, True),
                  pl.BlockSpec((BLK, 128), fwd), pl.BlockSpec((BLK, 128), bwd),
                  pl.BlockSpec((CPB, 32, CHUNK), fwd3), pl.BlockSpec((CPB, 32, CHUNK), bwd3)],
        out_specs=[pl.BlockSpec((BLK, w), fwd), pl.BlockSpec((BLK, w), bwd)],
        out_shape=[jax.ShapeDtypeStruct((t, w), f32), jax.ShapeDtypeStruct((t, w), f32)],
        scratch_shapes=[pltpu.VMEM((nu, HD, HD), f32), pltpu.VMEM((nu, 8, HD), f32),
                        pltpu.VMEM((nu, 8, HD), f32)],
        compiler_params=_cparams(("arbitrary",), 40),
        name="mlstm_scan",
    )(p, p, p, p, p, p, gcol, gcol, grow, grow)


def _attn_prep_kernel(q_ref, k_ref, v_ref, qg_ref, kg_ref, qo_ref, ko_ref, vo_ref):
    i = pl.program_id(0)
    r = lax.broadcasted_iota(jnp.int32, (BLK, HD), 0)
    lane = lax.broadcasted_iota(jnp.int32, (BLK, HD), 1)
    tok = (i - 1) * BLK + r
    pos = jnp.where(lane < HD // 2, jnp.right_shift(tok, 6), jnp.bitwise_and(tok, GRID_W - 1)).astype(f32)
    pair = jnp.bitwise_and(lane, HD // 4 - 1).astype(f32)
    inv_freq = jnp.exp(pair * (-jnp.log(ROPE_THETA) / (HD // 4)))
    ang = pos * inv_freq
    is_ctx = i == 0
    cos = jnp.where(is_ctx, 1.0, jnp.cos(ang))
    sin = jnp.where(is_ctx, 0.0, jnp.sin(ang))
    first = jnp.bitwise_and(lane, HD // 4) == 0
    sin_signed = jnp.where(first, -sin, sin)

    def norm_rope(x, g, scale):
        y = x * lax.rsqrt(jnp.mean(x * x, axis=-1, keepdims=True) + EPS) * g
        partner = jnp.where(first, pltpu.roll(y, HD - HD // 4, 1), pltpu.roll(y, HD // 4, 1))
        return (y * cos + partner * sin_signed) * scale

    for h in range(H_AT):
        x = q_ref[:, h * HD:(h + 1) * HD].astype(f32)
        qo_ref[:, h * HD:(h + 1) * HD] = norm_rope(x, qg_ref[...], QK_SCALE * LOG2E).astype(bf16)
    ones_col = jnp.where(lax.broadcasted_iota(jnp.int32, (BLK, HD), 1) == 0, 1.0, 0.0).astype(bf16)
    for h in range(H_KV):
        x = k_ref[:, h * HD:(h + 1) * HD].astype(f32)
        ko_ref[:, h * HD:(h + 1) * HD] = norm_rope(x, kg_ref[...], 1.0).astype(bf16)
        vo_ref[:, 2 * h * HD:(2 * h + 1) * HD] = v_ref[:, h * HD:(h + 1) * HD]
        vo_ref[:, (2 * h + 1) * HD:(2 * h + 2) * HD] = ones_col


def _attn_prep(p, q_g, k_g):
    t = p.shape[0]
    wq, wk = H_AT * HD, H_KV * HD
    return pl.pallas_call(
        _attn_prep_kernel,
        grid=(t // BLK,),
        in_specs=[pl.BlockSpec((BLK, wq), lambda i: (i, C_ATQ // wq)),
                  pl.BlockSpec((BLK, wk), lambda i: (i, C_ATK // wk)),
                  pl.BlockSpec((BLK, wk), lambda i: (i, C_ATV // wk)),
                  pl.BlockSpec((1, HD), lambda i: (0, 0)),
                  pl.BlockSpec((1, HD), lambda i: (0, 0))],
        out_specs=[pl.BlockSpec((BLK, wq), lambda i: (i, 0)), pl.BlockSpec((BLK, wk), lambda i: (i, 0)),
                   pl.BlockSpec((BLK, 2 * wk), lambda i: (i, 0))],
        out_shape=[jax.ShapeDtypeStruct((t, wq), bf16), jax.ShapeDtypeStruct((t, wk), bf16),
                   jax.ShapeDtypeStruct((t, 2 * wk), bf16)],
        compiler_params=_cparams(("parallel",), 40),
        name="attn_prep",
    )(p, p, p, q_g.reshape(1, HD), k_g.reshape(1, HD))


def _attn_kernel(tq, tk, n_ctx_tiles, n_main, q_ref, k_ref, v_ref, o_ref, m_scr, acc_scr):
    qi = pl.program_id(1)
    grp = H_AT // H_KV
    m_scr[...] = jnp.full_like(m_scr, NEG)
    acc_scr[...] = jnp.zeros_like(acc_scr)

    def kv_step(rows, width):
        kt = k_ref[rows, :]
        va = v_ref[rows, :]
        ss = [_mm_nt(q_ref[:, h * HD:(h + 1) * HD], kt) for h in range(grp)]
        ps = []
        for h in range(grp):
            s = ss[h]
            mx = s[:, 0:HD]
            for c in range(1, width // HD):
                mx = jnp.maximum(mx, s[:, c * HD:(c + 1) * HD])
            m_prev = m_scr[h]
            m_new = jnp.maximum(m_prev, jnp.max(mx, axis=-1, keepdims=True))
            alpha = jnp.exp2(m_prev - m_new)
            p = jnp.concatenate([jnp.exp2(s[:, c * HD:(c + 1) * HD] - m_new).astype(bf16)
                                 for c in range(width // HD)], axis=1)
            m_scr[h] = m_new
            ps.append((alpha, p))
        for h in range(grp):
            alpha, p = ps[h]
            acc = acc_scr[h]
            pv = jnp.dot(p, va, preferred_element_type=f32)
            acc_scr[h] = jnp.concatenate([alpha * acc[:, :HD], alpha * acc[:, HD:]], axis=1) + pv

    kv_step(pl.ds(0, BLK), BLK)

    @pl.when(qi >= n_ctx_tiles)
    def _():
        def body(j, carry):
            kv_step(pl.ds(pl.multiple_of(BLK + j * tk, HD), tk), tk)
            return carry
        lax.fori_loop(0, n_main, body, 0)

    for h in range(grp):
        acc = acc_scr[h]
        o_ref[:, h * HD:(h + 1) * HD] = (acc[:, :HD] / acc[:, HD:HD + 1]).astype(o_ref.dtype)


def _attention(qr, kr, va):
    t = qr.shape[0]
    tq = 256
    tk = _row_tile(t - BLK, (1024, 512, 256))
    grp = H_AT // H_KV
    wg = grp * HD
    kern = functools.partial(_attn_kernel, tq, tk, BLK // tq, (t - BLK) // tk)
    return pl.pallas_call(
        kern,
        grid=(H_KV, t // tq),
        in_specs=[pl.BlockSpec((tq, wg), lambda g, i: (i, g)),
                  pl.BlockSpec((t, HD), lambda g, i: (0, g)),
                  pl.BlockSpec((t, 2 * HD), lambda g, i: (0, g))],
        out_specs=pl.BlockSpec((tq, wg), lambda g, i: (i, g)),
        out_shape=jax.ShapeDtypeStruct((t, H_AT * HD), bf16),
        scratch_shapes=[pltpu.VMEM((grp, tq, HD), f32), pltpu.VMEM((grp, tq, 2 * HD), f32)],
        compiler_params=_cparams(("parallel", "arbitrary"), 52),
        name="flash_attention",
    )(qr, kr, va)


def _merge_kernel(tm, dnf_ref, dnb_ref, z_ref, at_ref, mlf_ref, mlb_ref, og_ref, x_ref, mod_ref,
                  dng_ref, mlg_ref, wo_ref, n2g_ref, rw1_ref, rw2_ref, xo_ref, h2_ref, lg_ref):
    i = pl.program_id(0)
    rows = i * tm + lax.broadcasted_iota(jnp.int32, (tm, 1), 0)
    is_ctx = rows < BLK
    w = N_HEADS_SCAN * HD

    def head_norm(x, g):
        return x * lax.rsqrt(jnp.mean(x * x, axis=-1, keepdims=True) + EPS) * g

    acc = jnp.dot(at_ref[...], wo_ref[w:w + H_AT * HD, :], preferred_element_type=f32)
    dn_parts, ml_parts = [], []
    for h in range(N_HEADS_SCAN):
        sl = slice(h * HD, (h + 1) * HD)
        z = z_ref[:, sl].astype(f32)
        dn_parts.append(head_norm(dnf_ref[:, sl] + dnb_ref[:, sl], dng_ref[...]) * (z * _sigmoid(z)))
        ml_parts.append(head_norm(mlf_ref[:, sl] + mlb_ref[:, sl], mlg_ref[...]) * _sigmoid(og_ref[:, sl].astype(f32)))
    dn = jnp.concatenate(dn_parts, axis=1).astype(bf16)
    ml = jnp.concatenate(ml_parts, axis=1).astype(bf16)
    acc = acc + jnp.dot(dn, wo_ref[0:w, :], preferred_element_type=f32)
    acc = acc + jnp.dot(ml, wo_ref[w + H_AT * HD:, :], preferred_element_type=f32)
    x = x_ref[...] + _mod_rows(mod_ref, 2, is_ctx) * acc
    xo_ref[...] = x
    y = x * lax.rsqrt(jnp.mean(x * x, axis=-1, keepdims=True) + EPS) * n2g_ref[...]
    h2 = y * (1.0 + _mod_rows(mod_ref, 4, is_ctx)) + _mod_rows(mod_ref, 3, is_ctx)
    h2_ref[...] = h2
    hh = h2.astype(bf16)
    hl = (h2 - hh.astype(f32)).astype(bf16)
    rw1 = rw1_ref[...]
    lg_ref[...] = (jnp.dot(hh, rw1, preferred_element_type=f32) + jnp.dot(hl, rw1, preferred_element_type=f32)
                   + jnp.dot(hh, rw2_ref[...], preferred_element_type=f32))


def _merge_outproj(dnf, dnb, p, at, mlf, mlb, x, mod, dn_g, ml_g, w_out, n2g, rw1, rw2):
    t = x.shape[0]
    tm = 256
    w = N_HEADS_SCAN * HD
    row = lambda i: (i, 0)
    full = lambda i: (0, 0)
    sw = pl.BlockSpec((tm, w), row)
    return pl.pallas_call(
        functools.partial(_merge_kernel, tm),
        grid=(t // tm,),
        in_specs=[sw, sw, pl.BlockSpec((tm, w), lambda i: (i, C_DNZ // w)),
                  pl.BlockSpec((tm, H_AT * HD), row), sw, sw,
                  pl.BlockSpec((tm, w), lambda i: (i, C_MLO // w)),
                  pl.BlockSpec((tm, D), row), pl.BlockSpec((8, 6 * D), full),
                  pl.BlockSpec((1, HD), full), pl.BlockSpec((1, HD), full),
                  pl.BlockSpec((D, D), full), pl.BlockSpec((1, D), full),
                  pl.BlockSpec((D, 128), full), pl.BlockSpec((D, 128), full)],
        out_specs=[pl.BlockSpec((tm, D), row), pl.BlockSpec((tm, D), row), pl.BlockSpec((tm, 128), row)],
        out_shape=[jax.ShapeDtypeStruct((t, D), f32), jax.ShapeDtypeStruct((t, D), f32),
                   jax.ShapeDtypeStruct((t, 128), f32)],
        compiler_params=_cparams(("parallel",), 48),
        name="merge_outproj",
    )(dnf, dnb, p, at, mlf, mlb, p, x, mod, dn_g.reshape(1, HD), ml_g.reshape(1, HD), w_out,
      n2g.reshape(1, D), rw1, rw2)


def _route_kernel(lg_ref, bias_ref, e_ref, g_ref):
    lt = lg_ref[...].T
    sc = [_sigmoid(lt[e:e + 1, :]) for e in range(N_EXPERTS)]
    bi = [sc[e] + bias_ref[e:e + 1, 0:1] for e in range(N_EXPERTS)]
    n_groups = N_EXPERTS // EXPERTS_PER_GROUP
    best, best_g = None, None
    for g in range(n_groups):
        a, b, c, d = bi[4 * g:4 * g + 4]
        gs = jnp.maximum(jnp.maximum(jnp.maximum(a + b, a + c), jnp.maximum(a + d, b + c)),
                         jnp.maximum(b + d, c + d))
        if g == 0:
            best, best_g = gs, jnp.zeros_like(gs, dtype=jnp.int32)
        else:
            better = gs > best
            best = jnp.where(better, gs, best)
            best_g = jnp.where(better, g, best_g)
    t1 = jnp.full_like(best, -jnp.inf)
    t2 = jnp.full_like(best, -jnp.inf)
    i1 = jnp.zeros_like(best_g)
    i2 = jnp.zeros_like(best_g)
    s1 = jnp.zeros_like(best)
    s2 = jnp.zeros_like(best)
    for e in range(N_EXPERTS):
        v = jnp.where(best_g == e // EXPERTS_PER_GROUP, bi[e], -jnp.inf)
        gt1 = v > t1
        gt2 = jnp.logical_and(jnp.logical_not(gt1), v > t2)
        t2 = jnp.where(gt1, t1, jnp.where(gt2, v, t2))
        i2 = jnp.where(gt1, i1, jnp.where(gt2, e, i2))
        s2 = jnp.where(gt1, s1, jnp.where(gt2, sc[e], s2))
        t1 = jnp.where(gt1, v, t1)
        i1 = jnp.where(gt1, e, i1)
        s1 = jnp.where(gt1, sc[e], s1)
    tot = s1 + s2
    zi = jnp.zeros_like(i1)
    zf = jnp.zeros_like(s1)
    e_ref[...] = jnp.concatenate([i1, i2, zi, zi, zi, zi, zi, zi], axis=0)
    g_ref[...] = jnp.concatenate([s1 / tot, s2 / tot, zf, zf, zf, zf, zf, zf], axis=0)


def _route(logits, router_bias):
    t = logits.shape[0]
    tm = 256
    bias = jnp.zeros((N_EXPERTS, 128), f32).at[:, 0].set(router_bias)
    return pl.pallas_call(
        _route_kernel,
        grid=(t // tm,),
        in_specs=[pl.BlockSpec((tm, 128), lambda i: (i, 0)), pl.BlockSpec((N_EXPERTS, 128), lambda i: (0, 0))],
        out_specs=[pl.BlockSpec((8, tm), lambda i: (0, i)), pl.BlockSpec((8, tm), lambda i: (0, i))],
        out_shape=[jax.ShapeDtypeStruct((8, t), jnp.int32), jax.ShapeDtypeStruct((8, t), f32)],
        compiler_params=_cparams(("parallel",)),
        name="route_top2",
    )(logits, bias)


def _dispatch(e_rows, g_rows, n_blocks):
    t = e_rows.shape[1]
    n = 2 * t
    flat_e = e_rows[0:2].T.reshape(n)
    flat_w = g_rows[0:2].T.reshape(n)
    order = jnp.argsort(flat_e, stable=True).astype(jnp.int32)
    counts = jnp.sum(flat_e[:, None] == jnp.arange(N_EXPERTS, dtype=jnp.int32)[None, :], axis=0).astype(jnp.int32)
    starts = jnp.cumsum(counts) - counts
    padded = (counts + MOE_BM - 1) // MOE_BM * MOE_BM
    p_ends = jnp.cumsum(padded)
    p_starts = p_ends - padded
    blk_e = jnp.minimum(jnp.searchsorted(p_ends, jnp.arange(n_blocks, dtype=jnp.int32) * MOE_BM, side="right"),
                        N_EXPERTS - 1).astype(jnp.int32)
    slot = jnp.arange(n_blocks * MOE_BM, dtype=jnp.int32)
    e_of = jnp.repeat(blk_e, MOE_BM)
    rank = slot - p_starts[e_of]
    valid = jnp.logical_and(rank >= 0, rank < counts[e_of])
    src = order[jnp.clip(starts[e_of] + rank, 0, n - 1)]
    tok = jnp.where(valid, src // 2, 0).astype(jnp.int32).reshape(n_blocks, 1, MOE_BM)
    dst = jnp.where(valid, (src % 2) * t + src // 2, -1).astype(jnp.int32).reshape(n_blocks, 1, MOE_BM)
    wt = jnp.where(valid, flat_w[src], 0.0).reshape(n_blocks, MOE_BM, 1)
    n_steps = n_blocks + 2
    tok_s = jnp.concatenate([tok, jnp.zeros((2, 1, MOE_BM), jnp.int32)], axis=0)
    e_s = jnp.concatenate([blk_e[:1], blk_e, blk_e[-1:]], axis=0)
    wt_s = jnp.concatenate([jnp.zeros((1, MOE_BM, 1), f32), wt, jnp.zeros((1, MOE_BM, 1), f32)], axis=0)
    dst_s = jnp.concatenate([jnp.full((2, 1, MOE_BM), -1, jnp.int32), dst], axis=0)
    is_pad = (dst_s < 0).reshape(-1)
    pad_rank = (jnp.cumsum(is_pad.astype(jnp.int32)) - 1).reshape(n_steps, 1, MOE_BM)
    dst_s = jnp.where(dst_s >= 0, dst_s, 2 * t + pad_rank)
    return e_s, tok_s, dst_s, wt_s


def _moe_kernel(n_steps, e_ref, tok_ref, dst_ref, wt_ref, h_hbm, wg_ref, wu_ref, wd_ref, out_hbm,
                xb0, xb1, yb0, yb1, sems):
    del e_ref
    s = pl.program_id(0)
    xbs, ybs = (xb0, xb1), (yb0, yb1)

    def wait_step_dmas():
        pltpu.make_async_copy(h_hbm.at[pl.ds(0, MOE_BM)], xb0, sems.at[0]).wait()
        pltpu.make_async_copy(yb0, out_hbm.at[pl.ds(0, MOE_BM)], sems.at[1]).wait()

    @pl.when(s == 0)
    def _():
        xb1[...] = jnp.zeros_like(xb1)
        yb1[...] = jnp.zeros_like(yb1)

    @pl.when(s > 0)
    def _():
        wait_step_dmas()

    def step(par):
        x_in, x_cur = xbs[par], xbs[1 - par]
        y_cur, y_out = ybs[par], ybs[1 - par]
        for r in range(MOE_BM):
            pltpu.make_async_copy(h_hbm.at[pl.ds(tok_ref[0, 0, r], 1)], x_in.at[pl.ds(r, 1)], sems.at[0]).start()
        x = x_cur[...].astype(bf16)
        g = jnp.dot(x, wg_ref[0], preferred_element_type=f32)
        u = jnp.dot(x, wu_ref[0], preferred_element_type=f32)
        a = (g * _sigmoid(g) * u).astype(bf16)
        y_cur[...] = jnp.dot(a, wd_ref[0], preferred_element_type=f32) * wt_ref[0]
        for r in range(MOE_BM):
            pltpu.make_async_copy(y_out.at[pl.ds(r, 1)], out_hbm.at[pl.ds(dst_ref[0, 0, r], 1)], sems.at[1]).start()

    @pl.when(lax.rem(s, 2) == 0)
    def _():
        step(0)

    @pl.when(lax.rem(s, 2) == 1)
    def _():
        step(1)

    @pl.when(s == n_steps - 1)
    def _():
        wait_step_dmas()


def _moe(h2, e_s, tok_s, dst_s, wt_s, wg, wu, wd):
    n_steps = tok_s.shape[0]
    grid_spec = pltpu.PrefetchScalarGridSpec(
        num_scalar_prefetch=1,
        grid=(n_steps,),
        in_specs=[pl.BlockSpec((1, 1, MOE_BM), lambda s, e: (s, 0, 0), memory_space=pltpu.SMEM),
                  pl.BlockSpec((1, 1, MOE_BM), lambda s, e: (s, 0, 0), memory_space=pltpu.SMEM),
                  pl.BlockSpec((1, MOE_BM, 1), lambda s, e: (s, 0, 0)),
                  pl.BlockSpec(memory_space=pl.ANY),
                  pl.BlockSpec((1, D, D_EXPERT), lambda s, e: (e[s], 0, 0)),
                  pl.BlockSpec((1, D, D_EXPERT), lambda s, e: (e[s], 0, 0)),
                  pl.BlockSpec((1, D_EXPERT, D), lambda s, e: (e[s], 0, 0))],
        out_specs=pl.BlockSpec(memory_space=pl.ANY),
        scratch_shapes=[pltpu.VMEM((MOE_BM, D), f32), pltpu.VMEM((MOE_BM, D), f32),
                        pltpu.VMEM((MOE_BM, D), f32), pltpu.VMEM((MOE_BM, D), f32),
                        pltpu.SemaphoreType.DMA((2,))],
    )
    return pl.pallas_call(
        functools.partial(_moe_kernel, n_steps),
        grid_spec=grid_spec,
        out_shape=jax.ShapeDtypeStruct((n_steps * MOE_BM, D), f32),
        compiler_params=_cparams(("arbitrary",), 52),
        name="moe_experts",
    )(e_s, tok_s, dst_s, wt_s, h2, wg, wu, wd)


def _final_kernel(x_ref, y0_ref, y1_ref, mod_ref, g_ref, o_ref):
    x = x_ref[...] + mod_ref[0:1, 5 * D:6 * D] * (y0_ref[...] + y1_ref[...])
    o_ref[...] = x * lax.rsqrt(jnp.mean(x * x, axis=-1, keepdims=True) + EPS) * g_ref[...]


def _final(x, moe, mod, g):
    t = x.shape[0]
    tm = 256
    nrow = t // tm
    nctx = BLK // tm
    return pl.pallas_call(
        _final_kernel,
        grid=(nrow - nctx,),
        in_specs=[pl.BlockSpec((tm, D), lambda i: (i + nctx, 0)),
                  pl.BlockSpec((tm, D), lambda i: (i + nctx, 0)),
                  pl.BlockSpec((tm, D), lambda i: (i + nctx + nrow, 0)),
                  pl.BlockSpec((8, 6 * D), lambda i: (0, 0)),
                  pl.BlockSpec((1, D), lambda i: (0, 0))],
        out_specs=pl.BlockSpec((tm, D), lambda i: (i, 0)),
        out_shape=jax.ShapeDtypeStruct((t - BLK, D), f32),
        compiler_params=_cparams(("parallel",), 40),
        name="final_norm",
    )(x, moe, moe, mod, g.reshape(1, D))


def _prep_in_weights(w_in):
    splits = (1536, 512, 8, 8, 1024, 256, 256, 512, 512, 512, 512, 8, 8)
    offs = [0]
    for s in splits:
        offs.append(offs[-1] + s)
    part = lambda i: w_in[:, offs[i]:offs[i + 1]]
    main = jnp.concatenate([part(i) for i in (0, 1, 4, 5, 6, 7, 8, 9, 10)], axis=1).astype(bf16)
    gates = jnp.concatenate([part(i) for i in (2, 3, 11, 12)], axis=1)
    gates = jnp.pad(gates, ((0, 0), (0, 128 - gates.shape[1])))
    g1 = gates.astype(bf16)
    g2 = (gates - g1.astype(f32)).astype(bf16)
    return main, g1, g2


def _layer(l, last, x, moe_prev, mods, norm1_g, norm2_g, w_in, dn_conv, dn_a_log, dn_dt_bias, dn_norm_g,
           q_norm_g, k_norm_g, ml_i_bias, ml_f_bias, ml_norm_g, w_out, rw1, rw2, router_bias,
           w_gate, w_up, w_down):
    del last
    t = x.shape[0]
    w_main, wg1, wg2 = _prep_in_weights(w_in[l])
    x, h, graw = _norm1(x, moe_prev, mods[l - 1] if l else None, mods[l], norm1_g[l], wg1, wg2)
    p = _matmul(h, w_main, bf16)
    conv_w = jnp.pad(dn_conv[l], ((0, 8 - CONV_W), (0, 0)))
    gate_params = jnp.zeros((8, 128), f32)
    gate_params = gate_params.at[0, 8:16].set(dn_dt_bias[l].reshape(8))
    gate_params = gate_params.at[0, 16:24].set(ml_i_bias[l].reshape(8))
    gate_params = gate_params.at[0, 24:32].set(ml_f_bias[l].reshape(8))
    gate_params = gate_params.at[1, 8:16].set(dn_a_log[l].reshape(8))
    dnq, gcol, grow = _scan_prep(p, graw, conv_w, gate_params)
    dnf, dnb = _gdn_scan(dnq, gcol, grow)
    mlf, mlb = _mlstm_scan(p, gcol, grow)
    qr, kr, va = _attn_prep(p, q_norm_g[l], k_norm_g[l])
    at = _attention(qr, kr, va)
    x, h2, logits = _merge_outproj(dnf, dnb, p, at, mlf, mlb, x, mods[l], dn_norm_g[l], ml_norm_g[l],
                                   w_out[l].astype(bf16), norm2_g[l], rw1, rw2)
    e_rows, g_rows = _route(logits, router_bias)
    n_blocks = (2 * t + N_EXPERTS * (MOE_BM - 1) + MOE_BM - 1) // MOE_BM
    blk_e, tok, dst, wt = _dispatch(e_rows, g_rows, n_blocks)
    moe = _moe(h2, blk_e, tok, dst, wt, w_gate[l].astype(bf16), w_up[l].astype(bf16), w_down[l].astype(bf16))
    return x, moe


def kernel(x, c, ctx, c_ctx, w_mod, b_mod, norm1_g, norm2_g, w_in, dn_conv, dn_a_log, dn_dt_bias, dn_norm_g, q_norm_g, k_norm_g, ml_i_bias, ml_f_bias, ml_norm_g, w_out, router_w, router_bias, w_gate, w_up, w_down, final_norm_g):
    b, seq, d = x.shape
    assert b == 1 and d == D and ctx.shape[1] == BLK and seq % BLK == 0 and seq % GRID_W == 0
    depth = w_mod.shape[0]
    mods = _mods(c, c_ctx, w_mod, b_mod)
    xs = jnp.concatenate([ctx[0], x[0]], axis=0)
    rw = jnp.pad(router_w, ((0, 0), (0, 128 - N_EXPERTS)))
    rw1 = rw.astype(bf16)
    rw2 = (rw - rw1.astype(f32)).astype(bf16)
    moe = None
    for l in range(depth):
        xs, moe = _layer(l, l == depth - 1, xs, moe, mods, norm1_g, norm2_g, w_in, dn_conv, dn_a_log, dn_dt_bias,
                         dn_norm_g, q_norm_g, k_norm_g, ml_i_bias, ml_f_bias, ml_norm_g, w_out, rw1, rw2,
                         router_bias, w_gate, w_up, w_down)
    out = _final(xs, moe, mods[depth - 1], final_norm_g)
    return out.reshape(b, seq, d)
```

```python
import functools

import jax
import jax.numpy as jnp
from jax import lax
from jax.experimental import pallas as pl
from jax.experimental.pallas import tpu as pltpu

f32 = jnp.float32
bf16 = jnp.bfloat16

D = 2048
HD = 128
N_HEADS_SCAN = 4
H_AT = 8
H_KV = 2
CHUNK = 64
BLK = 256
CPB = BLK // CHUNK
GRID_W = 64
ROPE_THETA = 10000.0
QK_SCALE = HD ** -0.5
LOG2E = 1.4426950408889634
N_EXPERTS = 16
EXPERTS_PER_GROUP = 4
D_EXPERT = D // 2
MOE_BM = 256
EPS = 1e-6
NEG = -1e30
CONV_W = 5

C_DNQKV, C_DNZ, C_ATQ, C_ATK, C_ATV, C_MLQ, C_MLK, C_MLV, C_MLO, P_COLS = (
    0, 1536, 2048, 3072, 3328, 3584, 4096, 4608, 5120, 5632)

V7X_VMEM_LIMIT_MB = 56


def _cparams(sems, vmem_mb=None):
    return pltpu.CompilerParams(
        dimension_semantics=sems,
        vmem_limit_bytes=None if vmem_mb is None else vmem_mb << 20)


def _mm(a, b):
    return jnp.dot(a.astype(bf16), b.astype(bf16), preferred_element_type=f32)


def _mm_nt(a, b):
    return lax.dot_general(a.astype(bf16), b.astype(bf16), (((1,), (1,)), ((), ())),
                           preferred_element_type=f32)


def _mm_tn(a, b):
    return lax.dot_general(a.astype(bf16), b.astype(bf16), (((0,), (0,)), ((), ())),
                           preferred_element_type=f32)


def _split3(x):
    x1 = x.astype(bf16)
    r1 = x - x1.astype(f32)
    x2 = r1.astype(bf16)
    x3 = (r1 - x2.astype(f32)).astype(bf16)
    return x1, x2, x3


def _sigmoid(x):
    return 1.0 / (1.0 + jnp.exp(-x))


def _softplus(x):
    return jnp.maximum(x, 0.0) + jnp.log(1.0 + jnp.exp(-jnp.abs(x)))


def _mod_kernel(s_ref, w_ref, b_ref, o_ref):
    s = s_ref[...]
    s = s * _sigmoid(s)
    o_ref[0] = jnp.dot(s, w_ref[0], preferred_element_type=f32,
                       precision=lax.Precision.HIGHEST) + b_ref[0]


def _mods(c, c_ctx, w_mod, b_mod):
    depth, d, n6 = w_mod.shape
    s = jnp.zeros((8, d), f32).at[0].set(c[0]).at[1].set(c_ctx)
    tn = 1024
    return pl.pallas_call(
        _mod_kernel,
        grid=(depth, n6 // tn),
        in_specs=[pl.BlockSpec((8, d), lambda l, j: (0, 0)),
                  pl.BlockSpec((1, d, tn), lambda l, j: (l, 0, j)),
                  pl.BlockSpec((1, 1, tn), lambda l, j: (l, 0, j))],
        out_specs=pl.BlockSpec((1, 8, tn), lambda l, j: (l, 0, j)),
        out_shape=jax.ShapeDtypeStruct((depth, 8, n6), f32),
        compiler_params=_cparams(("parallel", "parallel"), 40),
        name="mod_vectors",
    )(s, w_mod, b_mod.reshape(depth, 1, n6))


def _mod_rows(mod_ref, k, is_ctx):
    lat = mod_ref[0:1, k * D:(k + 1) * D]
    ctx = mod_ref[1:2, k * D:(k + 1) * D]
    return jnp.where(is_ctx, ctx, lat)


def _norm1_kernel(has_moe, tm, *refs):
    if has_moe:
        x_ref, y0_ref, y1_ref, modp_ref, mod_ref, g_ref, wg1_ref, wg2_ref, xo_ref, h_ref, gr_ref = refs
    else:
        x_ref, mod_ref, g_ref, wg1_ref, wg2_ref, h_ref, gr_ref = refs
    i = pl.program_id(0)
    rows = i * tm + lax.broadcasted_iota(jnp.int32, (tm, 1), 0)
    is_ctx = rows < BLK
    x = x_ref[...]
    if has_moe:
        x = x + _mod_rows(modp_ref, 5, is_ctx) * (y0_ref[...] + y1_ref[...])
        xo_ref[...] = x
    ms = jnp.mean(x * x, axis=-1, keepdims=True)
    y = x * lax.rsqrt(ms + EPS) * g_ref[...]
    h = y * (1.0 + _mod_rows(mod_ref, 1, is_ctx)) + _mod_rows(mod_ref, 0, is_ctx)
    hh = h.astype(bf16)
    h_ref[...] = hh
    hl = (h - hh.astype(f32)).astype(bf16)
    wg1 = wg1_ref[...]
    gr_ref[...] = (jnp.dot(hh, wg1, preferred_element_type=f32)
                   + jnp.dot(hl, wg1, preferred_element_type=f32)
                   + jnp.dot(hh, wg2_ref[...], preferred_element_type=f32))


def _norm1(x, moe, mod_prev, mod_cur, g, wg1, wg2):
    t = x.shape[0]
    tm = 256
    nrow = t // tm
    row = lambda i: (i, 0)
    full = lambda i: (0, 0)
    in_specs = [pl.BlockSpec((tm, D), row)]
    args = [x]
    if moe is not None:
        in_specs += [pl.BlockSpec((tm, D), row), pl.BlockSpec((tm, D), lambda i: (i + nrow, 0)),
                     pl.BlockSpec((8, 6 * D), full)]
        args += [moe, moe, mod_prev]
    in_specs += [pl.BlockSpec((8, 6 * D), full), pl.BlockSpec((1, D), full),
                 pl.BlockSpec((D, 128), full), pl.BlockSpec((D, 128), full)]
    args += [mod_cur, g.reshape(1, D), wg1, wg2]
    out_specs = [pl.BlockSpec((tm, D), row), pl.BlockSpec((tm, 128), row)]
    out_shape = [jax.ShapeDtypeStruct((t, D), bf16), jax.ShapeDtypeStruct((t, 128), f32)]
    if moe is not None:
        out_specs = [pl.BlockSpec((tm, D), row)] + out_specs
        out_shape = [jax.ShapeDtypeStruct((t, D), f32)] + out_shape
    outs = pl.pallas_call(
        functools.partial(_norm1_kernel, moe is not None, tm),
        grid=(nrow,), in_specs=in_specs, out_specs=out_specs, out_shape=out_shape,
        compiler_params=_cparams(("parallel",), 40),
        name="norm1_modulate",
    )(*args)
    if moe is not None:
        return outs
    return [x] + list(outs)


def _matmul_kernel(a_ref, b_ref, o_ref):
    o_ref[...] = jnp.dot(a_ref[...], b_ref[...], preferred_element_type=f32).astype(o_ref.dtype)


def _row_tile(t, choices):
    for c in choices:
        if t % c == 0:
            return c
    raise ValueError(f"no row tile for {t}")


def _matmul(a, b, out_dtype):
    m, k = a.shape
    n = b.shape[1]
    tm = _row_tile(m, (1280, 768, 512, 256))
    tn = _row_tile(n, (1408, 1024, 512))
    return pl.pallas_call(
        _matmul_kernel,
        grid=(n // tn, m // tm),
        in_specs=[pl.BlockSpec((tm, k), lambda j, i: (i, 0)),
                  pl.BlockSpec((k, tn), lambda j, i: (0, j))],
        out_specs=pl.BlockSpec((tm, tn), lambda j, i: (i, j)),
        out_shape=jax.ShapeDtypeStruct((m, n), out_dtype),
        compiler_params=_cparams(("parallel", "parallel"), 48),
        name="in_projection",
    )(a, b)


def _scan_prep_kernel(nblk, cur_ref, prev_ref, next_ref, cw_ref, graw_ref, gp_ref, q_ref, gc_ref, grow_ref, xs):
    i = pl.program_id(0)
    has_prev = i >= 2
    has_next = jnp.logical_and(i >= 1, i < nblk - 1)
    xs[0:16, :] = jnp.where(has_prev, prev_ref[...].astype(f32), 0.0)
    xs[16:16 + BLK, :] = cur_ref[...].astype(f32)
    xs[16 + BLK:32 + BLK, :] = jnp.where(has_next, next_ref[...].astype(f32), 0.0)
    acc = cw_ref[0:1, :] * xs[pl.ds(16 - CONV_W // 2, BLK), :]
    for j in range(1, CONV_W):
        acc = acc + cw_ref[j:j + 1, :] * xs[pl.ds(16 - CONV_W // 2 + j, BLK), :]
    a = acc * _sigmoid(acc)
    w = N_HEADS_SCAN * HD
    for h in range(2 * N_HEADS_SCAN):
        xh = a[:, h * HD:(h + 1) * HD]
        inv = lax.rsqrt(jnp.sum(xh * xh, axis=-1, keepdims=True) + EPS)
        scale = QK_SCALE if h < N_HEADS_SCAN else 1.0
        q_ref[:, h * HD:(h + 1) * HD] = (xh * (inv * scale)).astype(bf16)
    q_ref[:, 2 * w:3 * w] = a[:, 2 * w:3 * w].astype(bf16)

    z = graw_ref[...] + gp_ref[0:1, :]
    lane = lax.broadcasted_iota(jnp.int32, (BLK, 128), 1)
    sp = _softplus(z)
    vals = jnp.where(lane < 8, _sigmoid(z),
                     jnp.where(lane < 16, -jnp.exp(gp_ref[1:2, :]) * sp,
                               jnp.where(lane < 24, z, z - sp)))
    r = lax.broadcasted_iota(jnp.int32, (BLK, BLK), 0)
    c = lax.broadcasted_iota(jnp.int32, (BLK, BLK), 1)
    same = jnp.right_shift(r, 6) == jnp.right_shift(c, 6)
    tri_lo = jnp.where(jnp.logical_and(same, r >= c), 1.0, 0.0).astype(bf16)
    tri_up = jnp.where(jnp.logical_and(same, r <= c), 1.0, 0.0).astype(bf16)
    v1, v2, v3 = _split3(vals)
    dot = functools.partial(jnp.dot, preferred_element_type=f32)
    prefix = dot(tri_lo, v1) + dot(tri_lo, v2) + dot(tri_lo, v3)
    suffix = dot(tri_up, v1) + dot(tri_up, v2) + dot(tri_up, v3)
    is_cum = jnp.logical_and(jnp.bitwise_and(lane, 8) == 8, lane < 32)
    is_bwd = jnp.bitwise_and(lane, 4) == 4
    out = jnp.where(is_cum, jnp.where(is_bwd, suffix, prefix), vals)
    gc_ref[...] = out
    gt = out.T
    for cc in range(CPB):
        grow_ref[cc] = gt[0:32, cc * CHUNK:(cc + 1) * CHUNK]


def _scan_prep(p, graw, conv_w, gate_params):
    t = p.shape[0]
    nblk = t // BLK
    wq = 3 * N_HEADS_SCAN * HD
    n16 = t // 16
    return pl.pallas_call(
        functools.partial(_scan_prep_kernel, nblk),
        grid=(nblk,),
        in_specs=[pl.BlockSpec((BLK, wq), lambda i: (i, 0)),
                  pl.BlockSpec((16, wq), lambda i: (jnp.maximum(i * (BLK // 16) - 1, 0), 0)),
                  pl.BlockSpec((16, wq), lambda i: (jnp.minimum((i + 1) * (BLK // 16), n16 - 1), 0)),
                  pl.BlockSpec((8, wq), lambda i: (0, 0)),
                  pl.BlockSpec((BLK, 128), lambda i: (i, 0)),
                  pl.BlockSpec((8, 128), lambda i: (0, 0))],
        out_specs=[pl.BlockSpec((BLK, wq), lambda i: (i, 0)),
                   pl.BlockSpec((BLK, 128), lambda i: (i, 0)),
                   pl.BlockSpec((CPB, 32, CHUNK), lambda i: (i, 0, 0))],
        out_shape=[jax.ShapeDtypeStruct((t, wq), bf16),
                   jax.ShapeDtypeStruct((t, 128), f32),
                   jax.ShapeDtypeStruct((t // CHUNK, 32, CHUNK), f32)],
        scratch_shapes=[pltpu.VMEM((BLK + 32, wq), f32)],
        compiler_params=_cparams(("parallel",), 40),
        name="scan_prep",
    )(p, p, p, conv_w, graw, gate_params)


def _tri_masks():
    r = lax.broadcasted_iota(jnp.int32, (CHUNK, CHUNK), 0)
    c = lax.broadcasted_iota(jnp.int32, (CHUNK, CHUNK), 1)
    blk = jnp.right_shift(r, 4) == jnp.right_shift(c, 4)
    eye = jnp.where(r == c, 1.0, 0.0)
    return (r >= c, r <= c), (r > c, r < c), blk, eye


def _gdn_kernel(qf_ref, qb_ref, gcf_ref, gcb_ref, grf_ref, grb_ref, of_ref, ob_ref, s_scr):
    @pl.when(pl.program_id(0) == 0)
    def _():
        s_scr[...] = jnp.zeros_like(s_scr)

    incl, strict, blk, eye = _tri_masks()
    w = N_HEADS_SCAN * HD

    units = [(d, h) for d in range(2) for h in range(N_HEADS_SCAN)]
    rows_cat = lambda a, b: jnp.concatenate([a, b], axis=0)
    cols_cat = lambda a, b: jnp.concatenate([a, b], axis=1)
    C = CHUNK

    def chunk(cc, carry):
        ld = []
        for d, h in units:
            c = cc if d == 0 else CPB - 1 - cc
            q_ref, gc_ref, gr_ref = (qf_ref, gcf_ref, grf_ref) if d == 0 else (qb_ref, gcb_ref, grb_ref)
            rows = pl.ds(pl.multiple_of(c * C, C), C)
            u_idx = d * N_HEADS_SCAN + h
            q = q_ref[rows, h * HD:(h + 1) * HD]
            k = q_ref[rows, w + h * HD:w + (h + 1) * HD]
            v = q_ref[rows, 2 * w + h * HD:2 * w + (h + 1) * HD]
            beta = gc_ref[rows, u_idx:u_idx + 1]
            cum_c = gc_ref[rows, 8 + u_idx:9 + u_idx]
            cum_r = gr_ref[c][8 + u_idx:9 + u_idx, :]
            tot = cum_c[C - 1:C, :] if d == 0 else cum_c[0:1, :]
            ld.append((rows, q, k, v, beta, cum_c, cum_r, tot))
        g1 = [_mm_nt(rows_cat(k, q), k) for (_, q, k, *_) in ld]
        st = []
        for (d, h), (rows, q, k, v, beta, cum_c, cum_r, tot), g in zip(units, ld, g1):
            decay = jnp.exp(jnp.where(incl[d], cum_c - cum_r, NEG))
            nm = jnp.where(strict[d], beta * g[:C] * decay, 0.0)
            dm = jnp.where(blk, nm, 0.0)
            kf = k.astype(f32)
            e_c = jnp.exp(cum_c)
            rhs = cols_cat(cols_cat((beta * e_c) * kf, beta * v.astype(f32)), nm - dm)
            st.append(dict(dm=dm, rhs=rhs, qk=g[C:] * decay, k_dec=kf * jnp.exp(tot - cum_c),
                           q_dec=q.astype(f32) * e_c, g_last=jnp.exp(tot), p1=eye - dm))
        m2 = [_mm(s["dm"], s["dm"]) for s in st]
        r = [_mm(rows_cat(s["p1"], m), m) for s, m in zip(st, m2)]
        p2 = [s["p1"] + x[:C] for s, x in zip(st, r)]
        m4 = [x[C:] for x in r]
        r = [_mm(rows_cat(p, m), m) for p, m in zip(p2, m4)]
        p3 = [p + x[:C] for p, x in zip(p2, r)]
        m8 = [x[C:] for x in r]
        dinv = [p + _mm(p, m) for p, m in zip(p3, m8)]
        r = [_mm(di, s["rhs"]) for di, s in zip(dinv, st)]
        t1 = [x[:, :2 * HD] for x in r]
        qm = [x[:, 2 * HD:] for x in r]
        r = [_mm(qq, cols_cat(t, qq)) for qq, t in zip(qm, t1)]
        a1 = [x[:, :2 * HD] for x in r]
        qm2 = [x[:, 2 * HD:] for x in r]
        b2 = [_mm(q2, t) for q2, t in zip(qm2, t1)]
        c3 = [_mm(qq, b) for qq, b in zip(qm, b2)]
        sol = [t - a + b - c for t, a, b, c in zip(t1, a1, b2, c3)]
        s_old = [s_scr[i] for i in range(len(units))]
        r = [_mm(rows_cat(x[:, :HD], s["q_dec"]), so) for x, s, so in zip(sol, st, s_old)]
        u = [x[:, HD:] - y[:C] for x, y in zip(sol, r)]
        o_intra = [_mm(s["qk"], uu) for s, uu in zip(st, u)]
        s_add = [_mm_tn(s["k_dec"], uu) for s, uu in zip(st, u)]
        for i, ((d, h), l) in enumerate(zip(units, ld)):
            o_ref = of_ref if d == 0 else ob_ref
            o_ref[l[0], h * HD:(h + 1) * HD] = r[i][C:] + o_intra[i]
            s_scr[i] = st[i]["g_last"] * s_old[i] + s_add[i]
        return carry

    lax.fori_loop(0, CPB, chunk, 0)


def _bwd_block(nblk):
    return lambda s: (jnp.where(s == 0, 0, nblk - s), 0)


def _gdn_scan(qkv, gcol, grow):
    t = qkv.shape[0]
    nblk = t // BLK
    wq = 3 * N_HEADS_SCAN * HD
    w = N_HEADS_SCAN * HD
    fwd = lambda s: (s, 0)
    bwd = _bwd_block(nblk)
    fwd3 = lambda s: (s, 0, 0)
    bwd3 = lambda s: (jnp.where(s == 0, 0, nblk - s), 0, 0)
    return pl.pallas_call(
        _gdn_kernel,
        grid=(nblk,),
        in_specs=[pl.BlockSpec((BLK, wq), fwd), pl.BlockSpec((BLK, wq), bwd),
                  pl.BlockSpec((BLK, 128), fwd), pl.BlockSpec((BLK, 128), bwd),
                  pl.BlockSpec((CPB, 32, CHUNK), fwd3), pl.BlockSpec((CPB, 32, CHUNK), bwd3)],
        out_specs=[pl.BlockSpec((BLK, w), fwd), pl.BlockSpec((BLK, w), bwd)],
        out_shape=[jax.ShapeDtypeStruct((t, w), f32), jax.ShapeDtypeStruct((t, w), f32)],
        scratch_shapes=[pltpu.VMEM((2 * N_HEADS_SCAN, HD, HD), f32)],
        compiler_params=_cparams(("arbitrary",), 40),
        name="gdn_scan",
    )(qkv, qkv, gcol, gcol, grow, grow)


def _mlstm_kernel(pf_q, pf_k, pf_v, pb_q, pb_k, pb_v, gcf_ref, gcb_ref, grf_ref, grb_ref,
                  of_ref, ob_ref, c_scr, m_scr):
    @pl.when(pl.program_id(0) == 0)
    def _():
        c_scr[...] = jnp.zeros_like(c_scr)
        m_scr[...] = jnp.full_like(m_scr, NEG)

    incl, _, _, _ = _tri_masks()
    ones_col = jnp.where(lax.broadcasted_iota(jnp.int32, (CHUNK, HD), 1) == 0, 1.0, 0.0).astype(bf16)

    units = [(d, h) for d in range(2) for h in range(N_HEADS_SCAN)]

    def chunk(cc, carry):
        ld = []
        for d, h in units:
            c = cc if d == 0 else CPB - 1 - cc
            q_ref, k_ref, v_ref, gc_ref, gr_ref = (
                (pf_q, pf_k, pf_v, gcf_ref, grf_ref) if d == 0 else (pb_q, pb_k, pb_v, gcb_ref, grb_ref))
            rows = pl.ds(pl.multiple_of(c * CHUNK, CHUNK), CHUNK)
            u_idx = d * N_HEADS_SCAN + h
            q = q_ref[rows, h * HD:(h + 1) * HD]
            k = k_ref[rows, h * HD:(h + 1) * HD]
            v = v_ref[rows, h * HD:(h + 1) * HD]
            i_c = gc_ref[rows, 16 + u_idx:17 + u_idx]
            b_c = gc_ref[rows, 24 + u_idx:25 + u_idx]
            grow = gr_ref[c]
            i_r = grow[16 + u_idx:17 + u_idx, :]
            b_r = grow[24 + u_idx:25 + u_idx, :]
            b_last = b_c[CHUNK - 1:CHUNK, :] if d == 0 else b_c[0:1, :]
            w_log = b_last - b_c + i_c
            m_st = jnp.max(w_log, axis=0, keepdims=True)
            e_w = jnp.exp(w_log - m_st)
            ld.append(dict(rows=rows, q=q, k=k, v=v, i_r=i_r, b_c=b_c, b_r=b_r, b_last=b_last, m_st=m_st, e_w=e_w))
        nu = len(units)
        v_aug = [jnp.concatenate([l["v"], ones_col], axis=1) for l in ld]
        qk = [_mm_nt(l["q"], l["k"]) for l in ld]
        kv = [_mm_tn(l["k"], l["e_w"] * va.astype(f32)) for l, va in zip(ld, v_aug)]
        c_old = [c_scr[i] for i in range(nu)]
        qc = [_mm(l["q"], cm) for l, cm in zip(ld, c_old)]
        ps, m_locs = [], []
        for (d, h), l, g in zip(units, ld, qk):
            d_log = jnp.where(incl[d], l["b_c"] - l["b_r"] + l["i_r"], NEG)
            m_loc = jnp.max(d_log, axis=-1, keepdims=True)
            ps.append(jnp.exp(d_log - m_loc) * (g * QK_SCALE))
            m_locs.append(m_loc)
        loc = [_mm(p, va) for p, va in zip(ps, v_aug)]
        m_old = [m_scr[i][0:1, 0:1] for i in range(nu)]
        inter = [l["b_c"] + m for l, m in zip(ld, m_old)]
        m_r = [jnp.maximum(a, b) for a, b in zip(inter, m_locs)]
        a_in = [jnp.exp(a - b) for a, b in zip(inter, m_r)]
        a_lo = [jnp.exp(a - b) for a, b in zip(m_locs, m_r)]
        floor = [jnp.exp(-b) for b in m_r]
        m_new = [jnp.maximum(l["b_last"] + m, l["m_st"]) for l, m in zip(ld, m_old)]
        s_old = [jnp.exp(l["b_last"] + m - mn) for l, m, mn in zip(ld, m_old, m_new)]
        s_new = [jnp.exp(l["m_st"] - mn) * QK_SCALE for l, mn in zip(ld, m_new)]
        for i, ((d, h), l) in enumerate(zip(units, ld)):
            o_ref = of_ref if d == 0 else ob_ref
            num = a_in[i] * qc[i][:, :HD] + a_lo[i] * loc[i][:, :HD]
            den = a_in[i] * qc[i][:, HD:HD + 1] + a_lo[i] * loc[i][:, HD:HD + 1]
            o_ref[l["rows"], h * HD:(h + 1) * HD] = num / jnp.maximum(jnp.abs(den), floor[i])
        for i in range(nu):
            c_scr[i] = s_old[i] * c_old[i] + s_new[i] * kv[i]
            m_scr[i] = jnp.broadcast_to(m_new[i], (8, HD))
        return carry

    lax.fori_loop(0, CPB, chunk, 0)


def _mlstm_scan(p, gcol, grow):
    t = p.shape[0]
    nblk = t // BLK
    w = N_HEADS_SCAN * HD
    nu = 2 * N_HEADS_SCAN
    fwd = lambda s: (s, 0)
    bwd = _bwd_block(nblk)
    fwd3 = lambda s: (s, 0, 0)
    bwd3 = lambda s: (jnp.where(s == 0, 0, nblk - s), 0, 0)

    def col(base, bwd_dir):
        cb = base // w
        if bwd_dir:
            return pl.BlockSpec((BLK, w), lambda s: (jnp.where(s == 0, 0, nblk - s), cb))
        return pl.BlockSpec((BLK, w), lambda s: (s, cb))

    return pl.pallas_call(
        _mlstm_kernel,
        grid=(nblk,),
        in_specs=[col(C_MLQ, False), col(C_MLK, False), col(C_MLV, False),
                  col(C_MLQ, True), col(C_MLK, True), col(C_MLV, True),
                  pl.BlockSpec((BLK, 128), fwd), pl.BlockSpec((BLK, 128), bwd),
                  pl.BlockSpec((CPB, 32, CHUNK), fwd3), pl.BlockSpec((CPB, 32, CHUNK), bwd3)],
        out_specs=[pl.BlockSpec((BLK, w), fwd), pl.BlockSpec((BLK, w), bwd)],
        out_shape=[jax.ShapeDtypeStruct((t, w), f32), jax.ShapeDtypeStruct((t, w), f32)],
        scratch_shapes=[pltpu.VMEM((nu, HD, 2 * HD), f32), pltpu.VMEM((nu, 8, HD), f32)],
        compiler_params=_cparams(("arbitrary",), 40),
        name="mlstm_scan",
    )(p, p, p, p, p, p, gcol, gcol, grow, grow)


def _attn_prep_kernel(q_ref, k_ref, v_ref, qg_ref, kg_ref, qo_ref, ko_ref, vo_ref):
    i = pl.program_id(0)
    r = lax.broadcasted_iota(jnp.int32, (BLK, HD), 0)
    lane = lax.broadcasted_iota(jnp.int32, (BLK, HD), 1)
    tok = (i - 1) * BLK + r
    pos = jnp.where(lane < HD // 2, jnp.right_shift(tok, 6), jnp.bitwise_and(tok, GRID_W - 1)).astype(f32)
    pair = jnp.bitwise_and(lane, HD // 4 - 1).astype(f32)
    inv_freq = jnp.exp(pair * (-jnp.log(ROPE_THETA) / (HD // 4)))
    ang = pos * inv_freq
    is_ctx = i == 0
    cos = jnp.where(is_ctx, 1.0, jnp.cos(ang))
    sin = jnp.where(is_ctx, 0.0, jnp.sin(ang))
    first = jnp.bitwise_and(lane, HD // 4) == 0
    sin_signed = jnp.where(first, -sin, sin)

    def norm_rope(x, g, scale):
        y = x * lax.rsqrt(jnp.mean(x * x, axis=-1, keepdims=True) + EPS) * g
        partner = jnp.where(first, pltpu.roll(y, HD - HD // 4, 1), pltpu.roll(y, HD // 4, 1))
        return (y * cos + partner * sin_signed) * scale

    for h in range(H_AT):
        x = q_ref[:, h * HD:(h + 1) * HD].astype(f32)
        qo_ref[:, h * HD:(h + 1) * HD] = norm_rope(x, qg_ref[...], QK_SCALE * LOG2E).astype(bf16)
    ones_col = jnp.where(lax.broadcasted_iota(jnp.int32, (BLK, HD), 1) == 0, 1.0, 0.0).astype(bf16)
    for h in range(H_KV):
        x = k_ref[:, h * HD:(h + 1) * HD].astype(f32)
        ko_ref[:, h * HD:(h + 1) * HD] = norm_rope(x, kg_ref[...], 1.0).astype(bf16)
        vo_ref[:, 2 * h * HD:(2 * h + 1) * HD] = v_ref[:, h * HD:(h + 1) * HD]
        vo_ref[:, (2 * h + 1) * HD:(2 * h + 2) * HD] = ones_col


def _attn_prep(p, q_g, k_g):
    t = p.shape[0]
    wq, wk = H_AT * HD, H_KV * HD
    return pl.pallas_call(
        _attn_prep_kernel,
        grid=(t // BLK,),
        in_specs=[pl.BlockSpec((BLK, wq), lambda i: (i, C_ATQ // wq)),
                  pl.BlockSpec((BLK, wk), lambda i: (i, C_ATK // wk)),
                  pl.BlockSpec((BLK, wk), lambda i: (i, C_ATV // wk)),
                  pl.BlockSpec((1, HD), lambda i: (0, 0)),
                  pl.BlockSpec((1, HD), lambda i: (0, 0))],
        out_specs=[pl.BlockSpec((BLK, wq), lambda i: (i, 0)), pl.BlockSpec((BLK, wk), lambda i: (i, 0)),
                   pl.BlockSpec((BLK, 2 * wk), lambda i: (i, 0))],
        out_shape=[jax.ShapeDtypeStruct((t, wq), bf16), jax.ShapeDtypeStruct((t, wk), bf16),
                   jax.ShapeDtypeStruct((t, 2 * wk), bf16)],
        compiler_params=_cparams(("parallel",), 40),
        name="attn_prep",
    )(p, p, p, q_g.reshape(1, HD), k_g.reshape(1, HD))


def _attn_kernel(tq, tk, n_ctx_tiles, n_main, q_ref, k_ref, v_ref, o_ref, m_scr, acc_scr, sa_scr, sb_scr):
    qi = pl.program_id(1)
    grp = H_AT // H_KV
    m_scr[...] = jnp.full_like(m_scr, NEG)
    acc_scr[...] = jnp.zeros_like(acc_scr)

    def scores(rows):
        kt = k_ref[rows, :]
        return [_mm_nt(q_ref[:, h * HD:(h + 1) * HD], kt) for h in range(grp)]

    def softmax_pv(get_s, rows, width):
        va = v_ref[rows, :]
        ps = []
        for h in range(grp):
            mx = get_s(h, 0)
            for c in range(1, width // HD):
                mx = jnp.maximum(mx, get_s(h, c))
            m_prev = m_scr[h]
            m_new = jnp.maximum(m_prev, jnp.max(mx, axis=-1, keepdims=True))
            alpha = jnp.exp2(m_prev - m_new)
            p = jnp.concatenate([jnp.exp2(get_s(h, c) - m_new).astype(bf16) for c in range(width // HD)], axis=1)
            m_scr[h] = m_new
            ps.append((alpha, p))
        for h in range(grp):
            alpha, p = ps[h]
            acc = acc_scr[h]
            pv = jnp.dot(p, va, preferred_element_type=f32)
            acc_scr[h] = jnp.concatenate([alpha * acc[:, :HD], alpha * acc[:, HD:]], axis=1) + pv

    @pl.when(qi < n_ctx_tiles)
    def _():
        ctx_rows = pl.ds(0, BLK)
        ss = scores(ctx_rows)
        softmax_pv(lambda h, c: ss[h][:, c * HD:(c + 1) * HD], ctx_rows, BLK)

    def main_rows(j):
        return pl.ds(pl.multiple_of(j * tk, HD), tk)

    def store_scores(s_ref, j):
        for h, s in enumerate(scores(main_rows(j))):
            s_ref[h] = s

    def pipelined_step(cur_ref, nxt_ref, j):
        store_scores(nxt_ref, jnp.minimum(j + 1, n_main - 1))
        softmax_pv(lambda h, c: cur_ref[h, :, c * HD:(c + 1) * HD], main_rows(j), tk)

    @pl.when(qi >= n_ctx_tiles)
    def _():
        store_scores(sa_scr, 0)

        def body(i, carry):
            pipelined_step(sa_scr, sb_scr, 2 * i)
            pipelined_step(sb_scr, sa_scr, 2 * i + 1)
            return carry
        lax.fori_loop(0, n_main // 2, body, 0)
        if n_main % 2:
            pipelined_step(sa_scr, sb_scr, n_main - 1)

    for h in range(grp):
        acc = acc_scr[h]
        o_ref[:, h * HD:(h + 1) * HD] = (acc[:, :HD] / acc[:, HD:HD + 1]).astype(o_ref.dtype)


def _attention(qr, kr, va):
    t = qr.shape[0]
    tq = 256
    tk = _row_tile(t, (1280, 768, 256))
    grp = H_AT // H_KV
    wg = grp * HD
    kern = functools.partial(_attn_kernel, tq, tk, BLK // tq, t // tk)
    return pl.pallas_call(
        kern,
        grid=(H_KV, t // tq),
        in_specs=[pl.BlockSpec((tq, wg), lambda g, i: (i, g)),
                  pl.BlockSpec((t, HD), lambda g, i: (0, g)),
                  pl.BlockSpec((t, 2 * HD), lambda g, i: (0, g))],
        out_specs=pl.BlockSpec((tq, wg), lambda g, i: (i, g)),
        out_shape=jax.ShapeDtypeStruct((t, H_AT * HD), bf16),
        scratch_shapes=[pltpu.VMEM((grp, tq, HD), f32), pltpu.VMEM((grp, tq, 2 * HD), f32),
                        pltpu.VMEM((grp, tq, tk), f32), pltpu.VMEM((grp, tq, tk), f32)],
        compiler_params=_cparams(("parallel", "arbitrary"), 52),
        name="flash_attention",
    )(qr, kr, va)


def _merge_kernel(tm, dnf_ref, dnb_ref, z_ref, at_ref, mlf_ref, mlb_ref, og_ref, x_ref, mod_ref,
                  dng_ref, mlg_ref, wo_ref, n2g_ref, rw1_ref, rw2_ref, xo_ref, h2_ref, lg_ref):
    i = pl.program_id(0)
    rows = i * tm + lax.broadcasted_iota(jnp.int32, (tm, 1), 0)
    is_ctx = rows < BLK
    w = N_HEADS_SCAN * HD

    def head_norm(x, g):
        return x * lax.rsqrt(jnp.mean(x * x, axis=-1, keepdims=True) + EPS) * g

    acc = jnp.dot(at_ref[...], wo_ref[w:w + H_AT * HD, :], preferred_element_type=f32)
    dn_parts, ml_parts = [], []
    for h in range(N_HEADS_SCAN):
        sl = slice(h * HD, (h + 1) * HD)
        z = z_ref[:, sl].astype(f32)
        dn_parts.append(head_norm(dnf_ref[:, sl] + dnb_ref[:, sl], dng_ref[...]) * (z * _sigmoid(z)))
        ml_parts.append(head_norm(mlf_ref[:, sl] + mlb_ref[:, sl], mlg_ref[...]) * _sigmoid(og_ref[:, sl].astype(f32)))
    dn = jnp.concatenate(dn_parts, axis=1).astype(bf16)
    ml = jnp.concatenate(ml_parts, axis=1).astype(bf16)
    acc = acc + jnp.dot(dn, wo_ref[0:w, :], preferred_element_type=f32)
    acc = acc + jnp.dot(ml, wo_ref[w + H_AT * HD:, :], preferred_element_type=f32)
    x = x_ref[...] + _mod_rows(mod_ref, 2, is_ctx) * acc
    xo_ref[...] = x
    y = x * lax.rsqrt(jnp.mean(x * x, axis=-1, keepdims=True) + EPS) * n2g_ref[...]
    h2 = y * (1.0 + _mod_rows(mod_ref, 4, is_ctx)) + _mod_rows(mod_ref, 3, is_ctx)
    h2_ref[...] = h2
    hh = h2.astype(bf16)
    hl = (h2 - hh.astype(f32)).astype(bf16)
    rw1 = rw1_ref[...]
    lg_ref[...] = (jnp.dot(hh, rw1, preferred_element_type=f32) + jnp.dot(hl, rw1, preferred_element_type=f32)
                   + jnp.dot(hh, rw2_ref[...], preferred_element_type=f32))


def _merge_outproj(dnf, dnb, p, at, mlf, mlb, x, mod, dn_g, ml_g, w_out, n2g, rw1, rw2):
    t = x.shape[0]
    tm = 256
    w = N_HEADS_SCAN * HD
    row = lambda i: (i, 0)
    full = lambda i: (0, 0)
    sw = pl.BlockSpec((tm, w), row)
    return pl.pallas_call(
        functools.partial(_merge_kernel, tm),
        grid=(t // tm,),
        in_specs=[sw, sw, pl.BlockSpec((tm, w), lambda i: (i, C_DNZ // w)),
                  pl.BlockSpec((tm, H_AT * HD), row), sw, sw,
                  pl.BlockSpec((tm, w), lambda i: (i, C_MLO // w)),
                  pl.BlockSpec((tm, D), row), pl.BlockSpec((8, 6 * D), full),
                  pl.BlockSpec((1, HD), full), pl.BlockSpec((1, HD), full),
                  pl.BlockSpec((D, D), full), pl.BlockSpec((1, D), full),
                  pl.BlockSpec((D, 128), full), pl.BlockSpec((D, 128), full)],
        out_specs=[pl.BlockSpec((tm, D), row), pl.BlockSpec((tm, D), row), pl.BlockSpec((tm, 128), row)],
        out_shape=[jax.ShapeDtypeStruct((t, D), f32), jax.ShapeDtypeStruct((t, D), f32),
                   jax.ShapeDtypeStruct((t, 128), f32)],
        compiler_params=_cparams(("parallel",), 48),
        name="merge_outproj",
    )(dnf, dnb, p, at, mlf, mlb, p, x, mod, dn_g.reshape(1, HD), ml_g.reshape(1, HD), w_out,
      n2g.reshape(1, D), rw1, rw2)


def _route_kernel(lg_ref, bias_ref, e_ref, g_ref):
    lt = lg_ref[...].T
    sc = [_sigmoid(lt[e:e + 1, :]) for e in range(N_EXPERTS)]
    bi = [sc[e] + bias_ref[e:e + 1, 0:1] for e in range(N_EXPERTS)]
    n_groups = N_EXPERTS // EXPERTS_PER_GROUP
    best, best_g = None, None
    for g in range(n_groups):
        a, b, c, d = bi[4 * g:4 * g + 4]
        gs = jnp.maximum(jnp.maximum(jnp.maximum(a + b, a + c), jnp.maximum(a + d, b + c)),
                         jnp.maximum(b + d, c + d))
        if g == 0:
            best, best_g = gs, jnp.zeros_like(gs, dtype=jnp.int32)
        else:
            better = gs > best
            best = jnp.where(better, gs, best)
            best_g = jnp.where(better, g, best_g)
    t1 = jnp.full_like(best, -jnp.inf)
    t2 = jnp.full_like(best, -jnp.inf)
    i1 = jnp.zeros_like(best_g)
    i2 = jnp.zeros_like(best_g)
    s1 = jnp.zeros_like(best)
    s2 = jnp.zeros_like(best)
    for e in range(N_EXPERTS):
        v = jnp.where(best_g == e // EXPERTS_PER_GROUP, bi[e], -jnp.inf)
        gt1 = v > t1
        gt2 = jnp.logical_and(jnp.logical_not(gt1), v > t2)
        t2 = jnp.where(gt1, t1, jnp.where(gt2, v, t2))
        i2 = jnp.where(gt1, i1, jnp.where(gt2, e, i2))
        s2 = jnp.where(gt1, s1, jnp.where(gt2, sc[e], s2))
        t1 = jnp.where(gt1, v, t1)
        i1 = jnp.where(gt1, e, i1)
        s1 = jnp.where(gt1, sc[e], s1)
    tot = s1 + s2
    zi = jnp.zeros_like(i1)
    zf = jnp.zeros_like(s1)
    e_ref[...] = jnp.concatenate([i1, i2, zi, zi, zi, zi, zi, zi], axis=0)
    g_ref[...] = jnp.concatenate([s1 / tot, s2 / tot, zf, zf, zf, zf, zf, zf], axis=0)


def _route(logits, router_bias):
    t = logits.shape[0]
    tm = 256
    bias = jnp.zeros((N_EXPERTS, 128), f32).at[:, 0].set(router_bias)
    return pl.pallas_call(
        _route_kernel,
        grid=(t // tm,),
        in_specs=[pl.BlockSpec((tm, 128), lambda i: (i, 0)), pl.BlockSpec((N_EXPERTS, 128), lambda i: (0, 0))],
        out_specs=[pl.BlockSpec((8, tm), lambda i: (0, i)), pl.BlockSpec((8, tm), lambda i: (0, i))],
        out_shape=[jax.ShapeDtypeStruct((8, t), jnp.int32), jax.ShapeDtypeStruct((8, t), f32)],
        compiler_params=_cparams(("parallel",)),
        name="route_top2",
    )(logits, bias)


def _dispatch(e_rows, g_rows, n_blocks):
    t = e_rows.shape[1]
    n = 2 * t
    flat_e = e_rows[0:2].reshape(n)
    flat_w = g_rows[0:2].reshape(n)
    order = jnp.argsort(flat_e, stable=True).astype(jnp.int32)
    experts = jnp.arange(N_EXPERTS, dtype=jnp.int32)
    counts = jnp.sum(flat_e[:, None] == experts[None, :], axis=0).astype(jnp.int32)
    starts = jnp.cumsum(counts) - counts
    padded = (counts + MOE_BM - 1) // MOE_BM * MOE_BM
    p_ends = jnp.cumsum(padded)
    p_starts = p_ends - padded
    blk_start = jnp.arange(n_blocks, dtype=jnp.int32) * MOE_BM
    blk_e = jnp.minimum(jnp.sum(p_ends[None, :] <= blk_start[:, None], axis=1), N_EXPERTS - 1).astype(jnp.int32)
    sel = (blk_e[:, None] == experts[None, :]).astype(jnp.int32)
    pick = lambda v: jnp.sum(sel * v[None, :], axis=1)[:, None]
    rank = blk_start[:, None] + jnp.arange(MOE_BM, dtype=jnp.int32)[None, :] - pick(p_starts)
    valid = jnp.logical_and(rank >= 0, rank < pick(counts))
    src = order[jnp.clip(pick(starts) + rank, 0, n - 1)]
    tok = jnp.where(valid, jnp.where(src >= t, src - t, src), 0).astype(jnp.int32).reshape(n_blocks, 1, MOE_BM)
    dst = jnp.where(valid, src, -1).astype(jnp.int32).reshape(n_blocks, 1, MOE_BM)
    wt = jnp.where(valid, flat_w[src], 0.0).reshape(n_blocks, MOE_BM, 1)
    n_steps = n_blocks + 2
    tok_s = jnp.concatenate([tok, jnp.zeros((2, 1, MOE_BM), jnp.int32)], axis=0)
    e_s = jnp.concatenate([blk_e[:1], blk_e, blk_e[-1:]], axis=0)
    wt_s = jnp.concatenate([jnp.zeros((1, MOE_BM, 1), f32), wt, jnp.zeros((1, MOE_BM, 1), f32)], axis=0)
    dst_s = jnp.concatenate([jnp.full((2, 1, MOE_BM), -1, jnp.int32), dst], axis=0)
    is_pad = (dst_s < 0).reshape(-1)
    pad_rank = (jnp.cumsum(is_pad.astype(jnp.int32)) - 1).reshape(n_steps, 1, MOE_BM)
    dst_s = jnp.where(dst_s >= 0, dst_s, 2 * t + pad_rank)
    return e_s, tok_s, dst_s, wt_s


def _moe_kernel(n_steps, e_ref, tok_ref, dst_ref, wt_ref, h_hbm, wg_ref, wu_ref, wd_ref, out_hbm,
                xb0, xb1, yb0, yb1, sems):
    del e_ref
    s = pl.program_id(0)
    xbs, ybs = (xb0, xb1), (yb0, yb1)

    def wait_step_dmas():
        pltpu.make_async_copy(h_hbm.at[pl.ds(0, MOE_BM)], xb0, sems.at[0]).wait()
        pltpu.make_async_copy(yb0, out_hbm.at[pl.ds(0, MOE_BM)], sems.at[1]).wait()

    @pl.when(s == 0)
    def _():
        xb1[...] = jnp.zeros_like(xb1)
        yb1[...] = jnp.zeros_like(yb1)

    @pl.when(s > 0)
    def _():
        wait_step_dmas()

    def step(par):
        x_in, x_cur = xbs[par], xbs[1 - par]
        y_cur, y_out = ybs[par], ybs[1 - par]
        for r in range(MOE_BM):
            pltpu.make_async_copy(h_hbm.at[pl.ds(tok_ref[0, 0, r], 1)], x_in.at[pl.ds(r, 1)],
                                  sems.at[0]).start(priority=r % 2)
        x = x_cur[...].astype(bf16)
        g = jnp.dot(x, wg_ref[0], preferred_element_type=f32)
        u = jnp.dot(x, wu_ref[0], preferred_element_type=f32)
        a = (g * _sigmoid(g) * u).astype(bf16)
        y_cur[...] = jnp.dot(a, wd_ref[0], preferred_element_type=f32) * wt_ref[0]
        for r in range(MOE_BM):
            pltpu.make_async_copy(y_out.at[pl.ds(r, 1)], out_hbm.at[pl.ds(dst_ref[0, 0, r], 1)],
                                  sems.at[1]).start(priority=r % 2)

    @pl.when(lax.rem(s, 2) == 0)
    def _():
        step(0)

    @pl.when(lax.rem(s, 2) == 1)
    def _():
        step(1)

    @pl.when(s == n_steps - 1)
    def _():
        wait_step_dmas()


def _moe(h2, e_s, tok_s, dst_s, wt_s, wg, wu, wd):
    n_steps = tok_s.shape[0]
    grid_spec = pltpu.PrefetchScalarGridSpec(
        num_scalar_prefetch=1,
        grid=(n_steps,),
        in_specs=[pl.BlockSpec((1, 1, MOE_BM), lambda s, e: (s, 0, 0), memory_space=pltpu.SMEM),
                  pl.BlockSpec((1, 1, MOE_BM), lambda s, e: (s, 0, 0), memory_space=pltpu.SMEM),
                  pl.BlockSpec((1, MOE_BM, 1), lambda s, e: (s, 0, 0)),
                  pl.BlockSpec(memory_space=pl.ANY),
                  pl.BlockSpec((1, D, D_EXPERT), lambda s, e: (e[s], 0, 0)),
                  pl.BlockSpec((1, D, D_EXPERT), lambda s, e: (e[s], 0, 0)),
                  pl.BlockSpec((1, D_EXPERT, D), lambda s, e: (e[s], 0, 0))],
        out_specs=pl.BlockSpec(memory_space=pl.ANY),
        scratch_shapes=[pltpu.VMEM((MOE_BM, D), f32), pltpu.VMEM((MOE_BM, D), f32),
                        pltpu.VMEM((MOE_BM, D), f32), pltpu.VMEM((MOE_BM, D), f32),
                        pltpu.SemaphoreType.DMA((2,))],
    )
    return pl.pallas_call(
        functools.partial(_moe_kernel, n_steps),
        grid_spec=grid_spec,
        out_shape=jax.ShapeDtypeStruct((n_steps * MOE_BM, D), f32),
        compiler_params=_cparams(("arbitrary",), 52),
        name="moe_experts",
    )(e_s, tok_s, dst_s, wt_s, h2, wg, wu, wd)


def _final_kernel(x_ref, y0_ref, y1_ref, mod_ref, g_ref, o_ref):
    x = x_ref[...] + mod_ref[0:1, 5 * D:6 * D] * (y0_ref[...] + y1_ref[...])
    o_ref[...] = x * lax.rsqrt(jnp.mean(x * x, axis=-1, keepdims=True) + EPS) * g_ref[...]


def _final(x, moe, mod, g):
    t = x.shape[0]
    tm = 256
    nrow = t // tm
    nctx = BLK // tm
    return pl.pallas_call(
        _final_kernel,
        grid=(nrow - nctx,),
        in_specs=[pl.BlockSpec((tm, D), lambda i: (i + nctx, 0)),
                  pl.BlockSpec((tm, D), lambda i: (i + nctx, 0)),
                  pl.BlockSpec((tm, D), lambda i: (i + nctx + nrow, 0)),
                  pl.BlockSpec((8, 6 * D), lambda i: (0, 0)),
                  pl.BlockSpec((1, D), lambda i: (0, 0))],
        out_specs=pl.BlockSpec((tm, D), lambda i: (i, 0)),
        out_shape=jax.ShapeDtypeStruct((t - BLK, D), f32),
        compiler_params=_cparams(("parallel",), 40),
        name="final_norm",
    )(x, moe, moe, mod, g.reshape(1, D))


def _cast_kernel(x_ref, o_ref):
    o_ref[...] = x_ref[...].astype(o_ref.dtype)


def _to_bf16(w, layer):
    shape = w.shape[1:]
    w2 = w.reshape(-1, shape[-1])
    cols = shape[-1]
    rows = w2.shape[0] // w.shape[0]
    tr = _row_tile(rows, (1024, 512, 256))
    off = layer * (rows // tr)
    out = pl.pallas_call(
        _cast_kernel,
        grid=(rows // tr,),
        in_specs=[pl.BlockSpec((tr, cols), lambda i: (i + off, 0))],
        out_specs=pl.BlockSpec((tr, cols), lambda i: (i, 0)),
        out_shape=jax.ShapeDtypeStruct((rows, cols), bf16),
        compiler_params=_cparams(("parallel",), 48),
        name="cast_bf16",
    )(w2)
    return out.reshape(shape)


def _prep_in_weights(w_in):
    splits = (1536, 512, 8, 8, 1024, 256, 256, 512, 512, 512, 512, 8, 8)
    offs = [0]
    for s in splits:
        offs.append(offs[-1] + s)
    part = lambda i: w_in[:, offs[i]:offs[i + 1]]
    main = jnp.concatenate([part(i) for i in (0, 1, 4, 5, 6, 7, 8, 9, 10)], axis=1).astype(bf16)
    gates = jnp.concatenate([part(i) for i in (2, 3, 11, 12)], axis=1)
    gates = jnp.pad(gates, ((0, 0), (0, 128 - gates.shape[1])))
    g1 = gates.astype(bf16)
    g2 = (gates - g1.astype(f32)).astype(bf16)
    return main, g1, g2


def _layer(l, last, x, moe_prev, mods, norm1_g, norm2_g, w_in, dn_conv, dn_a_log, dn_dt_bias, dn_norm_g,
           q_norm_g, k_norm_g, ml_i_bias, ml_f_bias, ml_norm_g, w_out, rw1, rw2, router_bias,
           w_gate, w_up, w_down):
    del last
    t = x.shape[0]
    w_main, wg1, wg2 = _prep_in_weights(w_in[l])
    x, h, graw = _norm1(x, moe_prev, mods[l - 1] if l else None, mods[l], norm1_g[l], wg1, wg2)
    p = _matmul(h, w_main, bf16)
    conv_w = jnp.pad(dn_conv[l], ((0, 8 - CONV_W), (0, 0)))
    gate_params = jnp.zeros((8, 128), f32)
    gate_params = gate_params.at[0, 8:16].set(dn_dt_bias[l].reshape(8))
    gate_params = gate_params.at[0, 16:24].set(ml_i_bias[l].reshape(8))
    gate_params = gate_params.at[0, 24:32].set(ml_f_bias[l].reshape(8))
    gate_params = gate_params.at[1, 8:16].set(dn_a_log[l].reshape(8))
    dnq, gcol, grow = _scan_prep(p, graw, conv_w, gate_params)
    dnf, dnb = _gdn_scan(dnq, gcol, grow)
    mlf, mlb = _mlstm_scan(p, gcol, grow)
    qr, kr, va = _attn_prep(p, q_norm_g[l], k_norm_g[l])
    at = _attention(qr, kr, va)
    x, h2, logits = _merge_outproj(dnf, dnb, p, at, mlf, mlb, x, mods[l], dn_norm_g[l], ml_norm_g[l],
                                   w_out[l].astype(bf16), norm2_g[l], rw1, rw2)
    e_rows, g_rows = _route(logits, router_bias)
    n_blocks = (2 * t + N_EXPERTS * (MOE_BM - 1) + MOE_BM - 1) // MOE_BM
    blk_e, tok, dst, wt = _dispatch(e_rows, g_rows, n_blocks)
    moe = _moe(h2, blk_e, tok, dst, wt, _to_bf16(w_gate, l), _to_bf16(w_up, l), _to_bf16(w_down, l))
    return x, moe


def kernel(x, c, ctx, c_ctx, w_mod, b_mod, norm1_g, norm2_g, w_in, dn_conv, dn_a_log, dn_dt_bias, dn_norm_g, q_norm_g, k_norm_g, ml_i_bias, ml_f_bias, ml_norm_g, w_out, router_w, router_bias, w_gate, w_up, w_down, final_norm_g):
    b, seq, d = x.shape
    assert b == 1 and d == D and ctx.shape[1] == BLK and seq % BLK == 0 and seq % GRID_W == 0
    depth = w_mod.shape[0]
    mods = _mods(c, c_ctx, w_mod, b_mod)
    xs = jnp.concatenate([ctx[0], x[0]], axis=0)
    rw = jnp.pad(router_w, ((0, 0), (0, 128 - N_EXPERTS)))
    rw1 = rw.astype(bf16)
    rw2 = (rw - rw1.astype(f32)).astype(bf16)
    moe = None
    for l in range(depth):
        xs, moe = _layer(l, l == depth - 1, xs, moe, mods, norm1_g, norm2_g, w_in, dn_conv, dn_a_log, dn_dt_bias,
                         dn_norm_g, q_norm_g, k_norm_g, ml_i_bias, ml_f_bias, ml_norm_g, w_out, rw1, rw2,
                         router_bias, w_gate, w_up, w_down)
    out = _final(xs, moe, mods[depth - 1], final_norm_g)
    return out.reshape(b, seq, d)
```

```python
import functools

import jax
import jax.numpy as jnp
from jax import lax
from jax.experimental import pallas as pl
from jax.experimental.pallas import tpu as pltpu

f32 = jnp.float32
bf16 = jnp.bfloat16

D = 2048
HD = 128
N_HEADS_SCAN = 4
H_AT = 8
H_KV = 2
CHUNK = 64
BLK = 256
CPB = BLK // CHUNK
GRID_W = 64
ROPE_THETA = 10000.0
QK_SCALE = HD ** -0.5
LOG2E = 1.4426950408889634
N_EXPERTS = 16
EXPERTS_PER_GROUP = 4
D_EXPERT = D // 2
MOE_BM = 256
EPS = 1e-6
NEG = -1e30
CONV_W = 5

C_DNQKV, C_DNZ, C_ATQ, C_ATK, C_ATV, C_MLQ, C_MLK, C_MLV, C_MLO, P_COLS = (
    0, 1536, 2048, 3072, 3328, 3584, 4096, 4608, 5120, 5632)

V7X_VMEM_LIMIT_MB = 56


def _cparams(sems, vmem_mb=None):
    return pltpu.CompilerParams(
        dimension_semantics=sems,
        vmem_limit_bytes=None if vmem_mb is None else vmem_mb << 20)


def _mm(a, b):
    return jnp.dot(a.astype(bf16), b.astype(bf16), preferred_element_type=f32)


def _mm_nt(a, b):
    return lax.dot_general(a.astype(bf16), b.astype(bf16), (((1,), (1,)), ((), ())),
                           preferred_element_type=f32)


def _mm_tn(a, b):
    return lax.dot_general(a.astype(bf16), b.astype(bf16), (((0,), (0,)), ((), ())),
                           preferred_element_type=f32)


def _split3(x):
    x1 = x.astype(bf16)
    r1 = x - x1.astype(f32)
    x2 = r1.astype(bf16)
    x3 = (r1 - x2.astype(f32)).astype(bf16)
    return x1, x2, x3


def _sigmoid(x):
    return 1.0 / (1.0 + jnp.exp(-x))


def _softplus(x):
    return jnp.maximum(x, 0.0) + jnp.log(1.0 + jnp.exp(-jnp.abs(x)))


def _mod_kernel(s_ref, w_ref, b_ref, o_ref):
    s = s_ref[...]
    s = s * _sigmoid(s)
    o_ref[0] = jnp.dot(s, w_ref[0], preferred_element_type=f32,
                       precision=lax.Precision.HIGHEST) + b_ref[0]


def _mods(c, c_ctx, w_mod, b_mod):
    depth, d, n6 = w_mod.shape
    s = jnp.zeros((8, d), f32).at[0].set(c[0]).at[1].set(c_ctx)
    tn = 1024
    return pl.pallas_call(
        _mod_kernel,
        grid=(depth, n6 // tn),
        in_specs=[pl.BlockSpec((8, d), lambda l, j: (0, 0)),
                  pl.BlockSpec((1, d, tn), lambda l, j: (l, 0, j)),
                  pl.BlockSpec((1, 1, tn), lambda l, j: (l, 0, j))],
        out_specs=pl.BlockSpec((1, 8, tn), lambda l, j: (l, 0, j)),
        out_shape=jax.ShapeDtypeStruct((depth, 8, n6), f32),
        compiler_params=_cparams(("parallel", "parallel"), 40),
        name="mod_vectors",
    )(s, w_mod, b_mod.reshape(depth, 1, n6))


def _mod_rows(mod_ref, k, is_ctx):
    lat = mod_ref[0:1, k * D:(k + 1) * D]
    ctx = mod_ref[1:2, k * D:(k + 1) * D]
    return jnp.where(is_ctx, ctx, lat)


def _norm1_kernel(has_moe, tm, *refs):
    if has_moe:
        x_ref, y0_ref, y1_ref, modp_ref, mod_ref, g_ref, wg1_ref, wg2_ref, xo_ref, h_ref, gr_ref = refs
    else:
        x_ref, mod_ref, g_ref, wg1_ref, wg2_ref, h_ref, gr_ref = refs
    i = pl.program_id(0)
    rows = i * tm + lax.broadcasted_iota(jnp.int32, (tm, 1), 0)
    is_ctx = rows < BLK
    x = x_ref[...]
    if has_moe:
        x = x + _mod_rows(modp_ref, 5, is_ctx) * (y0_ref[...] + y1_ref[...])
        xo_ref[...] = x
    ms = jnp.mean(x * x, axis=-1, keepdims=True)
    y = x * lax.rsqrt(ms + EPS) * g_ref[...]
    h = y * (1.0 + _mod_rows(mod_ref, 1, is_ctx)) + _mod_rows(mod_ref, 0, is_ctx)
    hh = h.astype(bf16)
    h_ref[...] = hh
    hl = (h - hh.astype(f32)).astype(bf16)
    wg1 = wg1_ref[...]
    gr_ref[...] = (jnp.dot(hh, wg1, preferred_element_type=f32)
                   + jnp.dot(hl, wg1, preferred_element_type=f32)
                   + jnp.dot(hh, wg2_ref[...], preferred_element_type=f32))


def _norm1(x, moe, mod_prev, mod_cur, g, wg1, wg2):
    t = x.shape[0]
    tm = 256
    nrow = t // tm
    row = lambda i: (i, 0)
    full = lambda i: (0, 0)
    in_specs = [pl.BlockSpec((tm, D), row)]
    args = [x]
    if moe is not None:
        in_specs += [pl.BlockSpec((tm, D), row), pl.BlockSpec((tm, D), lambda i: (i + nrow, 0)),
                     pl.BlockSpec((8, 6 * D), full)]
        args += [moe, moe, mod_prev]
    in_specs += [pl.BlockSpec((8, 6 * D), full), pl.BlockSpec((1, D), full),
                 pl.BlockSpec((D, 128), full), pl.BlockSpec((D, 128), full)]
    args += [mod_cur, g.reshape(1, D), wg1, wg2]
    out_specs = [pl.BlockSpec((tm, D), row), pl.BlockSpec((tm, 128), row)]
    out_shape = [jax.ShapeDtypeStruct((t, D), bf16), jax.ShapeDtypeStruct((t, 128), f32)]
    if moe is not None:
        out_specs = [pl.BlockSpec((tm, D), row)] + out_specs
        out_shape = [jax.ShapeDtypeStruct((t, D), f32)] + out_shape
    outs = pl.pallas_call(
        functools.partial(_norm1_kernel, moe is not None, tm),
        grid=(nrow,), in_specs=in_specs, out_specs=out_specs, out_shape=out_shape,
        compiler_params=_cparams(("parallel",), 40),
        name="norm1_modulate",
    )(*args)
    if moe is not None:
        return outs
    return [x] + list(outs)


def _matmul_kernel(a_ref, b_ref, o_ref):
    o_ref[...] = jnp.dot(a_ref[...], b_ref[...], preferred_element_type=f32).astype(o_ref.dtype)


def _row_tile(t, choices):
    for c in choices:
        if t % c == 0:
            return c
    raise ValueError(f"no row tile for {t}")


def _matmul(a, b, out_dtype):
    m, k = a.shape
    n = b.shape[1]
    tm = _row_tile(m, (1280, 768, 512, 256))
    tn = _row_tile(n, (1408, 1024, 512))
    return pl.pallas_call(
        _matmul_kernel,
        grid=(n // tn, m // tm),
        in_specs=[pl.BlockSpec((tm, k), lambda j, i: (i, 0)),
                  pl.BlockSpec((k, tn), lambda j, i: (0, j))],
        out_specs=pl.BlockSpec((tm, tn), lambda j, i: (i, j)),
        out_shape=jax.ShapeDtypeStruct((m, n), out_dtype),
        compiler_params=_cparams(("parallel", "parallel"), 48),
        name="in_projection",
    )(a, b)


def _scan_prep_kernel(nblk, cur_ref, prev_ref, next_ref, cw_ref, graw_ref, gp_ref, q_ref, gc_ref, grow_ref, xs):
    i = pl.program_id(0)
    has_prev = i >= 2
    has_next = jnp.logical_and(i >= 1, i < nblk - 1)
    xs[0:16, :] = jnp.where(has_prev, prev_ref[...].astype(f32), 0.0)
    xs[16:16 + BLK, :] = cur_ref[...].astype(f32)
    xs[16 + BLK:32 + BLK, :] = jnp.where(has_next, next_ref[...].astype(f32), 0.0)
    acc = cw_ref[0:1, :] * xs[pl.ds(16 - CONV_W // 2, BLK), :]
    for j in range(1, CONV_W):
        acc = acc + cw_ref[j:j + 1, :] * xs[pl.ds(16 - CONV_W // 2 + j, BLK), :]
    a = acc * _sigmoid(acc)
    w = N_HEADS_SCAN * HD
    for h in range(2 * N_HEADS_SCAN):
        xh = a[:, h * HD:(h + 1) * HD]
        inv = lax.rsqrt(jnp.sum(xh * xh, axis=-1, keepdims=True) + EPS)
        scale = QK_SCALE if h < N_HEADS_SCAN else 1.0
        q_ref[:, h * HD:(h + 1) * HD] = (xh * (inv * scale)).astype(bf16)
    q_ref[:, 2 * w:3 * w] = a[:, 2 * w:3 * w].astype(bf16)

    z = graw_ref[...] + gp_ref[0:1, :]
    lane = lax.broadcasted_iota(jnp.int32, (BLK, 128), 1)
    sp = _softplus(z)
    vals = jnp.where(lane < 8, _sigmoid(z),
                     jnp.where(lane < 16, -jnp.exp(gp_ref[1:2, :]) * sp,
                               jnp.where(lane < 24, z, z - sp)))
    r = lax.broadcasted_iota(jnp.int32, (BLK, BLK), 0)
    c = lax.broadcasted_iota(jnp.int32, (BLK, BLK), 1)
    same = jnp.right_shift(r, 6) == jnp.right_shift(c, 6)
    tri_lo = jnp.where(jnp.logical_and(same, r >= c), 1.0, 0.0).astype(bf16)
    tri_up = jnp.where(jnp.logical_and(same, r <= c), 1.0, 0.0).astype(bf16)
    v1, v2, v3 = _split3(vals)
    dot = functools.partial(jnp.dot, preferred_element_type=f32)
    prefix = dot(tri_lo, v1) + dot(tri_lo, v2) + dot(tri_lo, v3)
    suffix = dot(tri_up, v1) + dot(tri_up, v2) + dot(tri_up, v3)
    is_cum = jnp.logical_and(jnp.bitwise_and(lane, 8) == 8, lane < 32)
    is_bwd = jnp.bitwise_and(lane, 4) == 4
    out = jnp.where(is_cum, jnp.where(is_bwd, suffix, prefix), vals)
    gc_ref[...] = out
    gt = out.T
    for cc in range(CPB):
        grow_ref[cc] = gt[0:32, cc * CHUNK:(cc + 1) * CHUNK]


def _scan_prep(p, graw, conv_w, gate_params):
    t = p.shape[0]
    nblk = t // BLK
    wq = 3 * N_HEADS_SCAN * HD
    n16 = t // 16
    return pl.pallas_call(
        functools.partial(_scan_prep_kernel, nblk),
        grid=(nblk,),
        in_specs=[pl.BlockSpec((BLK, wq), lambda i: (i, 0)),
                  pl.BlockSpec((16, wq), lambda i: (jnp.maximum(i * (BLK // 16) - 1, 0), 0)),
                  pl.BlockSpec((16, wq), lambda i: (jnp.minimum((i + 1) * (BLK // 16), n16 - 1), 0)),
                  pl.BlockSpec((8, wq), lambda i: (0, 0)),
                  pl.BlockSpec((BLK, 128), lambda i: (i, 0)),
                  pl.BlockSpec((8, 128), lambda i: (0, 0))],
        out_specs=[pl.BlockSpec((BLK, wq), lambda i: (i, 0)),
                   pl.BlockSpec((BLK, 128), lambda i: (i, 0)),
                   pl.BlockSpec((CPB, 32, CHUNK), lambda i: (i, 0, 0))],
        out_shape=[jax.ShapeDtypeStruct((t, wq), bf16),
                   jax.ShapeDtypeStruct((t, 128), f32),
                   jax.ShapeDtypeStruct((t // CHUNK, 32, CHUNK), f32)],
        scratch_shapes=[pltpu.VMEM((BLK + 32, wq), f32)],
        compiler_params=_cparams(("parallel",), 40),
        name="scan_prep",
    )(p, p, p, conv_w, graw, gate_params)


def _tri_masks():
    r = lax.broadcasted_iota(jnp.int32, (CHUNK, CHUNK), 0)
    c = lax.broadcasted_iota(jnp.int32, (CHUNK, CHUNK), 1)
    blk = jnp.right_shift(r, 4) == jnp.right_shift(c, 4)
    eye = jnp.where(r == c, 1.0, 0.0)
    return (r >= c, r <= c), (r > c, r < c), blk, eye


def _gdn_kernel(qf_ref, qb_ref, gcf_ref, gcb_ref, grf_ref, grb_ref, of_ref, ob_ref, s_scr):
    @pl.when(pl.program_id(0) == 0)
    def _():
        s_scr[...] = jnp.zeros_like(s_scr)

    incl, strict, blk, eye = _tri_masks()
    w = N_HEADS_SCAN * HD

    units = [(d, h) for d in range(2) for h in range(N_HEADS_SCAN)]
    rows_cat = lambda a, b: jnp.concatenate([a, b], axis=0)
    cols_cat = lambda a, b: jnp.concatenate([a, b], axis=1)
    C = CHUNK

    def chunk(cc, carry):
        ld = []
        for d, h in units:
            c = cc if d == 0 else CPB - 1 - cc
            q_ref, gc_ref, gr_ref = (qf_ref, gcf_ref, grf_ref) if d == 0 else (qb_ref, gcb_ref, grb_ref)
            rows = pl.ds(pl.multiple_of(c * C, C), C)
            u_idx = d * N_HEADS_SCAN + h
            q = q_ref[rows, h * HD:(h + 1) * HD]
            k = q_ref[rows, w + h * HD:w + (h + 1) * HD]
            v = q_ref[rows, 2 * w + h * HD:2 * w + (h + 1) * HD]
            beta = gc_ref[rows, u_idx:u_idx + 1]
            cum_c = gc_ref[rows, 8 + u_idx:9 + u_idx]
            cum_r = gr_ref[c][8 + u_idx:9 + u_idx, :]
            tot = cum_c[C - 1:C, :] if d == 0 else cum_c[0:1, :]
            ld.append((rows, q, k, v, beta, cum_c, cum_r, tot))
        g1 = [_mm_nt(rows_cat(k, q), k) for (_, q, k, *_) in ld]
        st = []
        for (d, h), (rows, q, k, v, beta, cum_c, cum_r, tot), g in zip(units, ld, g1):
            decay = jnp.exp(jnp.where(incl[d], cum_c - cum_r, NEG))
            nm = jnp.where(strict[d], beta * g[:C] * decay, 0.0)
            dm = jnp.where(blk, nm, 0.0)
            kf = k.astype(f32)
            e_c = jnp.exp(cum_c)
            rhs = cols_cat(cols_cat((beta * e_c) * kf, beta * v.astype(f32)), nm - dm)
            st.append(dict(dm=dm, rhs=rhs, qk=g[C:] * decay, k_dec=kf * jnp.exp(tot - cum_c),
                           q_dec=q.astype(f32) * e_c, g_last=jnp.exp(tot), p1=eye - dm))
        m2 = [_mm(s["dm"], s["dm"]) for s in st]
        r = [_mm(rows_cat(s["p1"], m), m) for s, m in zip(st, m2)]
        p2 = [s["p1"] + x[:C] for s, x in zip(st, r)]
        m4 = [x[C:] for x in r]
        r = [_mm(rows_cat(p, m), m) for p, m in zip(p2, m4)]
        p3 = [p + x[:C] for p, x in zip(p2, r)]
        m8 = [x[C:] for x in r]
        dinv = [p + _mm(p, m) for p, m in zip(p3, m8)]
        r = [_mm(di, s["rhs"]) for di, s in zip(dinv, st)]
        t1 = [x[:, :2 * HD] for x in r]
        qm = [x[:, 2 * HD:] for x in r]
        r = [_mm(qq, cols_cat(t, qq)) for qq, t in zip(qm, t1)]
        a1 = [x[:, :2 * HD] for x in r]
        qm2 = [x[:, 2 * HD:] for x in r]
        b2 = [_mm(q2, t) for q2, t in zip(qm2, t1)]
        c3 = [_mm(qq, b) for qq, b in zip(qm, b2)]
        sol = [t - a + b - c for t, a, b, c in zip(t1, a1, b2, c3)]
        s_old = [s_scr[i] for i in range(len(units))]
        r = [_mm(rows_cat(x[:, :HD], s["q_dec"]), so) for x, s, so in zip(sol, st, s_old)]
        u = [x[:, HD:] - y[:C] for x, y in zip(sol, r)]
        o_intra = [_mm(s["qk"], uu) for s, uu in zip(st, u)]
        s_add = [_mm_tn(s["k_dec"], uu) for s, uu in zip(st, u)]
        for i, ((d, h), l) in enumerate(zip(units, ld)):
            o_ref = of_ref if d == 0 else ob_ref
            o_ref[l[0], h * HD:(h + 1) * HD] = r[i][C:] + o_intra[i]
            s_scr[i] = st[i]["g_last"] * s_old[i] + s_add[i]
        return carry

    lax.fori_loop(0, CPB, chunk, 0)


def _bwd_block(nblk):
    return lambda s: (jnp.where(s == 0, 0, nblk - s), 0)


def _gdn_scan(qkv, gcol, grow):
    t = qkv.shape[0]
    nblk = t // BLK
    wq = 3 * N_HEADS_SCAN * HD
    w = N_HEADS_SCAN * HD
    fwd = lambda s: (s, 0)
    bwd = _bwd_block(nblk)
    fwd3 = lambda s: (s, 0, 0)
    bwd3 = lambda s: (jnp.where(s == 0, 0, nblk - s), 0, 0)
    return pl.pallas_call(
        _gdn_kernel,
        grid=(nblk,),
        in_specs=[pl.BlockSpec((BLK, wq), fwd), pl.BlockSpec((BLK, wq), bwd),
                  pl.BlockSpec((BLK, 128), fwd), pl.BlockSpec((BLK, 128), bwd),
                  pl.BlockSpec((CPB, 32, CHUNK), fwd3), pl.BlockSpec((CPB, 32, CHUNK), bwd3)],
        out_specs=[pl.BlockSpec((BLK, w), fwd), pl.BlockSpec((BLK, w), bwd)],
        out_shape=[jax.ShapeDtypeStruct((t, w), f32), jax.ShapeDtypeStruct((t, w), f32)],
        scratch_shapes=[pltpu.VMEM((2 * N_HEADS_SCAN, HD, HD), f32)],
        compiler_params=_cparams(("arbitrary",), 40),
        name="gdn_scan",
    )(qkv, qkv, gcol, gcol, grow, grow)


def _mlstm_kernel(pf_q, pf_k, pf_v, pb_q, pb_k, pb_v, gcf_ref, gcb_ref, grf_ref, grb_ref,
                  of_ref, ob_ref, c_scr, m_scr):
    @pl.when(pl.program_id(0) == 0)
    def _():
        c_scr[...] = jnp.zeros_like(c_scr)
        m_scr[...] = jnp.full_like(m_scr, NEG)

    incl, _, _, _ = _tri_masks()
    ones_col = jnp.where(lax.broadcasted_iota(jnp.int32, (CHUNK, HD), 1) == 0, 1.0, 0.0).astype(bf16)

    units = [(d, h) for d in range(2) for h in range(N_HEADS_SCAN)]

    def chunk(cc, carry):
        ld = []
        for d, h in units:
            c = cc if d == 0 else CPB - 1 - cc
            q_ref, k_ref, v_ref, gc_ref, gr_ref = (
                (pf_q, pf_k, pf_v, gcf_ref, grf_ref) if d == 0 else (pb_q, pb_k, pb_v, gcb_ref, grb_ref))
            rows = pl.ds(pl.multiple_of(c * CHUNK, CHUNK), CHUNK)
            u_idx = d * N_HEADS_SCAN + h
            q = q_ref[rows, h * HD:(h + 1) * HD]
            k = k_ref[rows, h * HD:(h + 1) * HD]
            v = v_ref[rows, h * HD:(h + 1) * HD]
            i_c = gc_ref[rows, 16 + u_idx:17 + u_idx]
            b_c = gc_ref[rows, 24 + u_idx:25 + u_idx]
            grow = gr_ref[c]
            i_r = grow[16 + u_idx:17 + u_idx, :]
            b_r = grow[24 + u_idx:25 + u_idx, :]
            b_last = b_c[CHUNK - 1:CHUNK, :] if d == 0 else b_c[0:1, :]
            w_log = b_last - b_c + i_c
            m_st = jnp.max(w_log, axis=0, keepdims=True)
            e_w = jnp.exp(w_log - m_st)
            ld.append(dict(rows=rows, q=q, k=k, v=v, i_r=i_r, b_c=b_c, b_r=b_r, b_last=b_last, m_st=m_st, e_w=e_w))
        nu = len(units)
        v_aug = [jnp.concatenate([l["v"], ones_col], axis=1) for l in ld]
        qk = [_mm_nt(l["q"], l["k"]) for l in ld]
        kv = [_mm_tn(l["k"], l["e_w"] * va.astype(f32)) for l, va in zip(ld, v_aug)]
        c_old = [c_scr[i] for i in range(nu)]
        qc = [_mm(l["q"], cm) for l, cm in zip(ld, c_old)]
        ps, m_locs = [], []
        for (d, h), l, g in zip(units, ld, qk):
            d_log = jnp.where(incl[d], l["b_c"] - l["b_r"] + l["i_r"], NEG)
            m_loc = jnp.max(d_log, axis=-1, keepdims=True)
            ps.append(jnp.exp(d_log - m_loc) * (g * QK_SCALE))
            m_locs.append(m_loc)
        loc = [_mm(p, va) for p, va in zip(ps, v_aug)]
        m_old = [m_scr[i][0:1, 0:1] for i in range(nu)]
        inter = [l["b_c"] + m for l, m in zip(ld, m_old)]
        m_r = [jnp.maximum(a, b) for a, b in zip(inter, m_locs)]
        a_in = [jnp.exp(a - b) for a, b in zip(inter, m_r)]
        a_lo = [jnp.exp(a - b) for a, b in zip(m_locs, m_r)]
        floor = [jnp.exp(-b) for b in m_r]
        m_new = [jnp.maximum(l["b_last"] + m, l["m_st"]) for l, m in zip(ld, m_old)]
        s_old = [jnp.exp(l["b_last"] + m - mn) for l, m, mn in zip(ld, m_old, m_new)]
        s_new = [jnp.exp(l["m_st"] - mn) * QK_SCALE for l, mn in zip(ld, m_new)]
        for i, ((d, h), l) in enumerate(zip(units, ld)):
            o_ref = of_ref if d == 0 else ob_ref
            num = a_in[i] * qc[i][:, :HD] + a_lo[i] * loc[i][:, :HD]
            den = a_in[i] * qc[i][:, HD:HD + 1] + a_lo[i] * loc[i][:, HD:HD + 1]
            o_ref[l["rows"], h * HD:(h + 1) * HD] = num / jnp.maximum(jnp.abs(den), floor[i])
        for i in range(nu):
            c_scr[i] = s_old[i] * c_old[i] + s_new[i] * kv[i]
            m_scr[i] = jnp.broadcast_to(m_new[i], (8, HD))
        return carry

    lax.fori_loop(0, CPB, chunk, 0)


def _mlstm_scan(p, gcol, grow):
    t = p.shape[0]
    nblk = t // BLK
    w = N_HEADS_SCAN * HD
    nu = 2 * N_HEADS_SCAN
    fwd = lambda s: (s, 0)
    bwd = _bwd_block(nblk)
    fwd3 = lambda s: (s, 0, 0)
    bwd3 = lambda s: (jnp.where(s == 0, 0, nblk - s), 0, 0)

    def col(base, bwd_dir):
        cb = base // w
        if bwd_dir:
            return pl.BlockSpec((BLK, w), lambda s: (jnp.where(s == 0, 0, nblk - s), cb))
        return pl.BlockSpec((BLK, w), lambda s: (s, cb))

    return pl.pallas_call(
        _mlstm_kernel,
        grid=(nblk,),
        in_specs=[col(C_MLQ, False), col(C_MLK, False), col(C_MLV, False),
                  col(C_MLQ, True), col(C_MLK, True), col(C_MLV, True),
                  pl.BlockSpec((BLK, 128), fwd), pl.BlockSpec((BLK, 128), bwd),
                  pl.BlockSpec((CPB, 32, CHUNK), fwd3), pl.BlockSpec((CPB, 32, CHUNK), bwd3)],
        out_specs=[pl.BlockSpec((BLK, w), fwd), pl.BlockSpec((BLK, w), bwd)],
        out_shape=[jax.ShapeDtypeStruct((t, w), f32), jax.ShapeDtypeStruct((t, w), f32)],
        scratch_shapes=[pltpu.VMEM((nu, HD, 2 * HD), f32), pltpu.VMEM((nu, 8, HD), f32)],
        compiler_params=_cparams(("arbitrary",), 40),
        name="mlstm_scan",
    )(p, p, p, p, p, p, gcol, gcol, grow, grow)


def _rope_kernel(cos_ref, sin_ref):
    i = pl.program_id(0)
    r = lax.broadcasted_iota(jnp.int32, (BLK, HD), 0)
    lane = lax.broadcasted_iota(jnp.int32, (BLK, HD), 1)
    tok = (i - 1) * BLK + r
    pos = jnp.where(lane < HD // 2, jnp.right_shift(tok, 6), jnp.bitwise_and(tok, GRID_W - 1)).astype(f32)
    pair = jnp.bitwise_and(lane, HD // 4 - 1).astype(f32)
    inv_freq = jnp.exp(pair * (-jnp.log(ROPE_THETA) / (HD // 4)))
    ang = pos * inv_freq
    is_ctx = i == 0
    sin = jnp.where(is_ctx, 0.0, jnp.sin(ang))
    first = jnp.bitwise_and(lane, HD // 4) == 0
    cos_ref[...] = jnp.where(is_ctx, 1.0, jnp.cos(ang))
    sin_ref[...] = jnp.where(first, -sin, sin)


def _rope_tables(t):
    spec = pl.BlockSpec((BLK, HD), lambda i: (i, 0))
    return pl.pallas_call(
        _rope_kernel, grid=(t // BLK,), in_specs=[], out_specs=[spec, spec],
        out_shape=[jax.ShapeDtypeStruct((t, HD), f32)] * 2,
        compiler_params=_cparams(("parallel",)), name="rope_tables",
    )()


def _attn_prep_kernel(q_ref, k_ref, v_ref, qg_ref, kg_ref, cos_ref, sin_ref, qo_ref, ko_ref, vo_ref):
    lane = lax.broadcasted_iota(jnp.int32, (BLK, HD), 1)
    first = jnp.bitwise_and(lane, HD // 4) == 0
    cos = cos_ref[...]
    sin_signed = sin_ref[...]

    def norm_rope(x, g, scale):
        y = x * lax.rsqrt(jnp.mean(x * x, axis=-1, keepdims=True) + EPS) * g
        partner = jnp.where(first, pltpu.roll(y, HD - HD // 4, 1), pltpu.roll(y, HD // 4, 1))
        return (y * cos + partner * sin_signed) * scale

    for h in range(H_AT):
        x = q_ref[:, h * HD:(h + 1) * HD].astype(f32)
        qo_ref[:, h * HD:(h + 1) * HD] = norm_rope(x, qg_ref[...], QK_SCALE * LOG2E).astype(bf16)
    ones_col = jnp.where(lax.broadcasted_iota(jnp.int32, (BLK, HD), 1) == 0, 1.0, 0.0).astype(bf16)
    for h in range(H_KV):
        x = k_ref[:, h * HD:(h + 1) * HD].astype(f32)
        ko_ref[:, h * HD:(h + 1) * HD] = norm_rope(x, kg_ref[...], 1.0).astype(bf16)
        vo_ref[:, 2 * h * HD:(2 * h + 1) * HD] = v_ref[:, h * HD:(h + 1) * HD]
        vo_ref[:, (2 * h + 1) * HD:(2 * h + 2) * HD] = ones_col


def _attn_prep(p, q_g, k_g, rope):
    t = p.shape[0]
    wq, wk = H_AT * HD, H_KV * HD
    return pl.pallas_call(
        _attn_prep_kernel,
        grid=(t // BLK,),
        in_specs=[pl.BlockSpec((BLK, wq), lambda i: (i, C_ATQ // wq)),
                  pl.BlockSpec((BLK, wk), lambda i: (i, C_ATK // wk)),
                  pl.BlockSpec((BLK, wk), lambda i: (i, C_ATV // wk)),
                  pl.BlockSpec((1, HD), lambda i: (0, 0)),
                  pl.BlockSpec((1, HD), lambda i: (0, 0)),
                  pl.BlockSpec((BLK, HD), lambda i: (i, 0)),
                  pl.BlockSpec((BLK, HD), lambda i: (i, 0))],
        out_specs=[pl.BlockSpec((BLK, wq), lambda i: (i, 0)), pl.BlockSpec((BLK, wk), lambda i: (i, 0)),
                   pl.BlockSpec((BLK, 2 * wk), lambda i: (i, 0))],
        out_shape=[jax.ShapeDtypeStruct((t, wq), bf16), jax.ShapeDtypeStruct((t, wk), bf16),
                   jax.ShapeDtypeStruct((t, 2 * wk), bf16)],
        compiler_params=_cparams(("parallel",), 40),
        name="attn_prep",
    )(p, p, p, q_g.reshape(1, HD), k_g.reshape(1, HD), *rope)


def _attn_kernel(tq, tk, n_ctx_tiles, n_main, q_ref, k_ref, v_ref, o_ref, m_scr, acc_scr, sa_scr, sb_scr):
    qi = pl.program_id(1)
    grp = H_AT // H_KV
    m_scr[...] = jnp.full_like(m_scr, NEG)
    acc_scr[...] = jnp.zeros_like(acc_scr)

    def scores(rows):
        kt = k_ref[rows, :]
        return [_mm_nt(q_ref[:, h * HD:(h + 1) * HD], kt) for h in range(grp)]

    def softmax_pv(get_s, rows, width):
        va = v_ref[rows, :]
        ps = []
        for h in range(grp):
            mx = get_s(h, 0)
            for c in range(1, width // HD):
                mx = jnp.maximum(mx, get_s(h, c))
            m_prev = m_scr[h]
            m_new = jnp.maximum(m_prev, jnp.max(mx, axis=-1, keepdims=True))
            alpha = jnp.exp2(m_prev - m_new)
            p = jnp.concatenate([jnp.exp2(get_s(h, c) - m_new).astype(bf16) for c in range(width // HD)], axis=1)
            m_scr[h] = m_new
            ps.append((alpha, p))
        for h in range(grp):
            alpha, p = ps[h]
            acc = acc_scr[h]
            pv = jnp.dot(p, va, preferred_element_type=f32)
            acc_scr[h] = jnp.concatenate([alpha * acc[:, :HD], alpha * acc[:, HD:]], axis=1) + pv

    @pl.when(qi < n_ctx_tiles)
    def _():
        ctx_rows = pl.ds(0, BLK)
        ss = scores(ctx_rows)
        softmax_pv(lambda h, c: ss[h][:, c * HD:(c + 1) * HD], ctx_rows, BLK)

    def main_rows(j):
        return pl.ds(pl.multiple_of(j * tk, HD), tk)

    def store_scores(s_ref, j):
        for h, s in enumerate(scores(main_rows(j))):
            s_ref[h] = s

    def pipelined_step(cur_ref, nxt_ref, j):
        store_scores(nxt_ref, jnp.minimum(j + 1, n_main - 1))
        softmax_pv(lambda h, c: cur_ref[h, :, c * HD:(c + 1) * HD], main_rows(j), tk)

    @pl.when(qi >= n_ctx_tiles)
    def _():
        store_scores(sa_scr, 0)

        def body(i, carry):
            pipelined_step(sa_scr, sb_scr, 2 * i)
            pipelined_step(sb_scr, sa_scr, 2 * i + 1)
            return carry
        lax.fori_loop(0, n_main // 2, body, 0)
        if n_main % 2:
            pipelined_step(sa_scr, sb_scr, n_main - 1)

    for h in range(grp):
        acc = acc_scr[h]
        o_ref[:, h * HD:(h + 1) * HD] = (acc[:, :HD] / acc[:, HD:HD + 1]).astype(o_ref.dtype)


def _attention(qr, kr, va):
    t = qr.shape[0]
    tq = 256
    tk = _row_tile(t, (1280, 768, 256))
    grp = H_AT // H_KV
    wg = grp * HD
    kern = functools.partial(_attn_kernel, tq, tk, BLK // tq, t // tk)
    return pl.pallas_call(
        kern,
        grid=(H_KV, t // tq),
        in_specs=[pl.BlockSpec((tq, wg), lambda g, i: (i, g)),
                  pl.BlockSpec((t, HD), lambda g, i: (0, g)),
                  pl.BlockSpec((t, 2 * HD), lambda g, i: (0, g))],
        out_specs=pl.BlockSpec((tq, wg), lambda g, i: (i, g)),
        out_shape=jax.ShapeDtypeStruct((t, H_AT * HD), bf16),
        scratch_shapes=[pltpu.VMEM((grp, tq, HD), f32), pltpu.VMEM((grp, tq, 2 * HD), f32),
                        pltpu.VMEM((grp, tq, tk), f32), pltpu.VMEM((grp, tq, tk), f32)],
        compiler_params=_cparams(("parallel", "arbitrary"), 52),
        name="flash_attention",
    )(qr, kr, va)


def _merge_kernel(tm, dnf_ref, dnb_ref, z_ref, at_ref, mlf_ref, mlb_ref, og_ref, x_ref, mod_ref,
                  dng_ref, mlg_ref, wo_ref, n2g_ref, rw1_ref, rw2_ref, xo_ref, h2_ref, lg_ref):
    i = pl.program_id(0)
    rows = i * tm + lax.broadcasted_iota(jnp.int32, (tm, 1), 0)
    is_ctx = rows < BLK
    w = N_HEADS_SCAN * HD

    def head_norm(x, g):
        return x * lax.rsqrt(jnp.mean(x * x, axis=-1, keepdims=True) + EPS) * g

    acc = jnp.dot(at_ref[...], wo_ref[w:w + H_AT * HD, :], preferred_element_type=f32)
    dn_parts, ml_parts = [], []
    for h in range(N_HEADS_SCAN):
        sl = slice(h * HD, (h + 1) * HD)
        z = z_ref[:, sl].astype(f32)
        dn_parts.append(head_norm(dnf_ref[:, sl] + dnb_ref[:, sl], dng_ref[...]) * (z * _sigmoid(z)))
        ml_parts.append(head_norm(mlf_ref[:, sl] + mlb_ref[:, sl], mlg_ref[...]) * _sigmoid(og_ref[:, sl].astype(f32)))
    dn = jnp.concatenate(dn_parts, axis=1).astype(bf16)
    ml = jnp.concatenate(ml_parts, axis=1).astype(bf16)
    acc = acc + jnp.dot(dn, wo_ref[0:w, :], preferred_element_type=f32)
    acc = acc + jnp.dot(ml, wo_ref[w + H_AT * HD:, :], preferred_element_type=f32)
    x = x_ref[...] + _mod_rows(mod_ref, 2, is_ctx) * acc
    xo_ref[...] = x
    y = x * lax.rsqrt(jnp.mean(x * x, axis=-1, keepdims=True) + EPS) * n2g_ref[...]
    h2 = y * (1.0 + _mod_rows(mod_ref, 4, is_ctx)) + _mod_rows(mod_ref, 3, is_ctx)
    hh = h2.astype(bf16)
    hb = lax.bitcast_convert_type(hh.astype(f32), jnp.uint32)
    h2_ref[...] = jnp.bitwise_or(jnp.bitwise_and(hb[:, D // 2:], jnp.uint32(0xFFFF0000)),
                                 jnp.right_shift(hb[:, :D // 2], jnp.uint32(16)))
    hl = (h2 - hh.astype(f32)).astype(bf16)
    rw1 = rw1_ref[...]
    lg_ref[...] = (jnp.dot(hh, rw1, preferred_element_type=f32) + jnp.dot(hl, rw1, preferred_element_type=f32)
                   + jnp.dot(hh, rw2_ref[...], preferred_element_type=f32))


def _merge_outproj(dnf, dnb, p, at, mlf, mlb, x, mod, dn_g, ml_g, w_out, n2g, rw1, rw2):
    t = x.shape[0]
    tm = 256
    w = N_HEADS_SCAN * HD
    row = lambda i: (i, 0)
    full = lambda i: (0, 0)
    sw = pl.BlockSpec((tm, w), row)
    return pl.pallas_call(
        functools.partial(_merge_kernel, tm),
        grid=(t // tm,),
        in_specs=[sw, sw, pl.BlockSpec((tm, w), lambda i: (i, C_DNZ // w)),
                  pl.BlockSpec((tm, H_AT * HD), row), sw, sw,
                  pl.BlockSpec((tm, w), lambda i: (i, C_MLO // w)),
                  pl.BlockSpec((tm, D), row), pl.BlockSpec((8, 6 * D), full),
                  pl.BlockSpec((1, HD), full), pl.BlockSpec((1, HD), full),
                  pl.BlockSpec((D, D), full), pl.BlockSpec((1, D), full),
                  pl.BlockSpec((D, 128), full), pl.BlockSpec((D, 128), full)],
        out_specs=[pl.BlockSpec((tm, D), row), pl.BlockSpec((tm, D // 2), row), pl.BlockSpec((tm, 128), row)],
        out_shape=[jax.ShapeDtypeStruct((t, D), f32), jax.ShapeDtypeStruct((t, D // 2), jnp.uint32),
                   jax.ShapeDtypeStruct((t, 128), f32)],
        compiler_params=_cparams(("parallel",), 48),
        name="merge_outproj",
    )(dnf, dnb, p, at, mlf, mlb, p, x, mod, dn_g.reshape(1, HD), ml_g.reshape(1, HD), w_out,
      n2g.reshape(1, D), rw1, rw2)


def _route_kernel(lg_ref, bias_ref, e_ref, g_ref):
    lt = lg_ref[...].T
    sc = [_sigmoid(lt[e:e + 1, :]) for e in range(N_EXPERTS)]
    bi = [sc[e] + bias_ref[e:e + 1, 0:1] for e in range(N_EXPERTS)]
    n_groups = N_EXPERTS // EXPERTS_PER_GROUP
    best, best_g = None, None
    for g in range(n_groups):
        a, b, c, d = bi[4 * g:4 * g + 4]
        gs = jnp.maximum(jnp.maximum(jnp.maximum(a + b, a + c), jnp.maximum(a + d, b + c)),
                         jnp.maximum(b + d, c + d))
        if g == 0:
            best, best_g = gs, jnp.zeros_like(gs, dtype=jnp.int32)
        else:
            better = gs > best
            best = jnp.where(better, gs, best)
            best_g = jnp.where(better, g, best_g)
    t1 = jnp.full_like(best, -jnp.inf)
    t2 = jnp.full_like(best, -jnp.inf)
    i1 = jnp.zeros_like(best_g)
    i2 = jnp.zeros_like(best_g)
    s1 = jnp.zeros_like(best)
    s2 = jnp.zeros_like(best)
    for e in range(N_EXPERTS):
        v = jnp.where(best_g == e // EXPERTS_PER_GROUP, bi[e], -jnp.inf)
        gt1 = v > t1
        gt2 = jnp.logical_and(jnp.logical_not(gt1), v > t2)
        t2 = jnp.where(gt1, t1, jnp.where(gt2, v, t2))
        i2 = jnp.where(gt1, i1, jnp.where(gt2, e, i2))
        s2 = jnp.where(gt1, s1, jnp.where(gt2, sc[e], s2))
        t1 = jnp.where(gt1, v, t1)
        i1 = jnp.where(gt1, e, i1)
        s1 = jnp.where(gt1, sc[e], s1)
    tot = s1 + s2
    zi = jnp.zeros_like(i1)
    zf = jnp.zeros_like(s1)
    e_ref[...] = jnp.concatenate([i1, i2, zi, zi, zi, zi, zi, zi], axis=0)
    g_ref[...] = jnp.concatenate([s1 / tot, s2 / tot, zf, zf, zf, zf, zf, zf], axis=0)


def _route(logits, router_bias):
    t = logits.shape[0]
    tm = 256
    bias = jnp.zeros((N_EXPERTS, 128), f32).at[:, 0].set(router_bias)
    return pl.pallas_call(
        _route_kernel,
        grid=(t // tm,),
        in_specs=[pl.BlockSpec((tm, 128), lambda i: (i, 0)), pl.BlockSpec((N_EXPERTS, 128), lambda i: (0, 0))],
        out_specs=[pl.BlockSpec((8, tm), lambda i: (0, i)), pl.BlockSpec((8, tm), lambda i: (0, i))],
        out_shape=[jax.ShapeDtypeStruct((8, t), jnp.int32), jax.ShapeDtypeStruct((8, t), f32)],
        compiler_params=_cparams(("parallel",)),
        name="route_top2",
    )(logits, bias)


def _dispatch(e_rows, g_rows, n_blocks):
    t = e_rows.shape[1]
    n = 2 * t
    flat_e = e_rows[0:2].reshape(n)
    flat_w = g_rows[0:2].reshape(n)
    order = jnp.argsort(flat_e, stable=True).astype(jnp.int32)
    experts = jnp.arange(N_EXPERTS, dtype=jnp.int32)
    counts = jnp.sum(flat_e[:, None] == experts[None, :], axis=0).astype(jnp.int32)
    starts = jnp.cumsum(counts) - counts
    padded = (counts + MOE_BM - 1) // MOE_BM * MOE_BM
    p_ends = jnp.cumsum(padded)
    p_starts = p_ends - padded
    blk_start = jnp.arange(n_blocks, dtype=jnp.int32) * MOE_BM
    blk_e = jnp.minimum(jnp.sum(p_ends[None, :] <= blk_start[:, None], axis=1), N_EXPERTS - 1).astype(jnp.int32)
    sel = (blk_e[:, None] == experts[None, :]).astype(jnp.int32)
    pick = lambda v: jnp.sum(sel * v[None, :], axis=1)[:, None]
    rank = blk_start[:, None] + jnp.arange(MOE_BM, dtype=jnp.int32)[None, :] - pick(p_starts)
    valid = jnp.logical_and(rank >= 0, rank < pick(counts))
    src = order[jnp.clip(pick(starts) + rank, 0, n - 1)]
    tok = jnp.where(valid, jnp.where(src >= t, src - t, src), 0).astype(jnp.int32).reshape(n_blocks, 1, MOE_BM)
    dst = jnp.where(valid, src, -1).astype(jnp.int32).reshape(n_blocks, 1, MOE_BM)
    wt = jnp.where(valid, flat_w[src], 0.0).reshape(n_blocks, MOE_BM, 1)
    n_steps = n_blocks + 2
    tok_s = jnp.concatenate([tok, jnp.zeros((2, 1, MOE_BM), jnp.int32)], axis=0)
    e_s = jnp.concatenate([blk_e[:1], blk_e, blk_e[-1:]], axis=0)
    wt_s = jnp.concatenate([jnp.zeros((1, MOE_BM, 1), f32), wt, jnp.zeros((1, MOE_BM, 1), f32)], axis=0)
    dst_s = jnp.concatenate([jnp.full((2, 1, MOE_BM), -1, jnp.int32), dst], axis=0)
    is_pad = (dst_s < 0).reshape(-1)
    pad_rank = (jnp.cumsum(is_pad.astype(jnp.int32)) - 1).reshape(n_steps, 1, MOE_BM)
    dst_s = jnp.where(dst_s >= 0, dst_s, 2 * t + pad_rank)
    return e_s, tok_s, dst_s, wt_s


def _moe_kernel(n_steps, e_ref, tok_ref, dst_ref, wt_ref, h_hbm, wg_ref, wu_ref, wd_ref, out_hbm,
                xb0, xb1, yb0, yb1, sems):
    del e_ref
    s = pl.program_id(0)
    xbs, ybs = (xb0, xb1), (yb0, yb1)

    def wait_step_dmas():
        pltpu.make_async_copy(h_hbm.at[pl.ds(0, MOE_BM)], xb0, sems.at[0]).wait()
        pltpu.make_async_copy(yb0, out_hbm.at[pl.ds(0, MOE_BM)], sems.at[1]).wait()

    @pl.when(s == 0)
    def _():
        xb1[...] = jnp.zeros_like(xb1)
        yb1[...] = jnp.zeros_like(yb1)

    @pl.when(s > 0)
    def _():
        wait_step_dmas()

    def step(par):
        x_in, x_cur = xbs[par], xbs[1 - par]
        y_cur, y_out = ybs[par], ybs[1 - par]
        for r in range(MOE_BM):
            pltpu.make_async_copy(h_hbm.at[pl.ds(tok_ref[0, 0, r], 1)], x_in.at[pl.ds(r, 1)], sems.at[0]).start()
        w = x_cur[...]
        x_lo = lax.bitcast_convert_type(jnp.left_shift(w, jnp.uint32(16)), f32).astype(bf16)
        x_hi = lax.bitcast_convert_type(jnp.bitwise_and(w, jnp.uint32(0xFFFF0000)), f32).astype(bf16)
        dh = D // 2
        g = (jnp.dot(x_lo, wg_ref[0, :dh, :], preferred_element_type=f32)
             + jnp.dot(x_hi, wg_ref[0, dh:, :], preferred_element_type=f32))
        u = (jnp.dot(x_lo, wu_ref[0, :dh, :], preferred_element_type=f32)
             + jnp.dot(x_hi, wu_ref[0, dh:, :], preferred_element_type=f32))
        a = (g * _sigmoid(g) * u).astype(bf16)
        y_cur[...] = jnp.dot(a, wd_ref[0], preferred_element_type=f32) * wt_ref[0]
        for r in range(MOE_BM):
            pltpu.make_async_copy(y_out.at[pl.ds(r, 1)], out_hbm.at[pl.ds(dst_ref[0, 0, r], 1)], sems.at[1]).start()

    @pl.when(lax.rem(s, 2) == 0)
    def _():
        step(0)

    @pl.when(lax.rem(s, 2) == 1)
    def _():
        step(1)

    @pl.when(s == n_steps - 1)
    def _():
        wait_step_dmas()


def _moe(h2, e_s, tok_s, dst_s, wt_s, wg, wu, wd):
    n_steps = tok_s.shape[0]
    grid_spec = pltpu.PrefetchScalarGridSpec(
        num_scalar_prefetch=1,
        grid=(n_steps,),
        in_specs=[pl.BlockSpec((1, 1, MOE_BM), lambda s, e: (s, 0, 0), memory_space=pltpu.SMEM),
                  pl.BlockSpec((1, 1, MOE_BM), lambda s, e: (s, 0, 0), memory_space=pltpu.SMEM),
                  pl.BlockSpec((1, MOE_BM, 1), lambda s, e: (s, 0, 0)),
                  pl.BlockSpec(memory_space=pl.ANY),
                  pl.BlockSpec((1, D, D_EXPERT), lambda s, e: (e[s], 0, 0)),
                  pl.BlockSpec((1, D, D_EXPERT), lambda s, e: (e[s], 0, 0)),
                  pl.BlockSpec((1, D_EXPERT, D), lambda s, e: (e[s], 0, 0))],
        out_specs=pl.BlockSpec(memory_space=pl.ANY),
        scratch_shapes=[pltpu.VMEM((MOE_BM, D // 2), jnp.uint32)] * 2 + [pltpu.VMEM((MOE_BM, D), f32)] * 2
        + [pltpu.SemaphoreType.DMA((2,))],
    )
    return pl.pallas_call(
        functools.partial(_moe_kernel, n_steps),
        grid_spec=grid_spec,
        out_shape=jax.ShapeDtypeStruct((n_steps * MOE_BM, D), f32),
        compiler_params=_cparams(("arbitrary",), 52),
        name="moe_experts",
    )(e_s, tok_s, dst_s, wt_s, h2, wg, wu, wd)


def _final_kernel(x_ref, y0_ref, y1_ref, mod_ref, g_ref, o_ref):
    x = x_ref[...] + mod_ref[0:1, 5 * D:6 * D] * (y0_ref[...] + y1_ref[...])
    o_ref[...] = x * lax.rsqrt(jnp.mean(x * x, axis=-1, keepdims=True) + EPS) * g_ref[...]


def _final(x, moe, mod, g):
    t = x.shape[0]
    tm = 256
    nrow = t // tm
    nctx = BLK // tm
    return pl.pallas_call(
        _final_kernel,
        grid=(nrow - nctx,),
        in_specs=[pl.BlockSpec((tm, D), lambda i: (i + nctx, 0)),
                  pl.BlockSpec((tm, D), lambda i: (i + nctx, 0)),
                  pl.BlockSpec((tm, D), lambda i: (i + nctx + nrow, 0)),
                  pl.BlockSpec((8, 6 * D), lambda i: (0, 0)),
                  pl.BlockSpec((1, D), lambda i: (0, 0))],
        out_specs=pl.BlockSpec((tm, D), lambda i: (i, 0)),
        out_shape=jax.ShapeDtypeStruct((t - BLK, D), f32),
        compiler_params=_cparams(("parallel",), 40),
        name="final_norm",
    )(x, moe, moe, mod, g.reshape(1, D))


def _cast_kernel(x_ref, o_ref):
    o_ref[...] = x_ref[...].astype(o_ref.dtype)


def _to_bf16(w, layer):
    shape = w.shape[1:]
    w2 = w.reshape(-1, shape[-1])
    cols = shape[-1]
    rows = w2.shape[0] // w.shape[0]
    tr = _row_tile(rows, (1024, 512, 256))
    off = layer * (rows // tr)
    out = pl.pallas_call(
        _cast_kernel,
        grid=(rows // tr,),
        in_specs=[pl.BlockSpec((tr, cols), lambda i: (i + off, 0))],
        out_specs=pl.BlockSpec((tr, cols), lambda i: (i, 0)),
        out_shape=jax.ShapeDtypeStruct((rows, cols), bf16),
        compiler_params=_cparams(("parallel",), 48),
        name="cast_bf16",
    )(w2)
    return out.reshape(shape)


def _prep_in_weights(w_in):
    splits = (1536, 512, 8, 8, 1024, 256, 256, 512, 512, 512, 512, 8, 8)
    offs = [0]
    for s in splits:
        offs.append(offs[-1] + s)
    part = lambda i: w_in[:, offs[i]:offs[i + 1]]
    main = jnp.concatenate([part(i) for i in (0, 1, 4, 5, 6, 7, 8, 9, 10)], axis=1).astype(bf16)
    gates = jnp.concatenate([part(i) for i in (2, 3, 11, 12)], axis=1)
    gates = jnp.pad(gates, ((0, 0), (0, 128 - gates.shape[1])))
    g1 = gates.astype(bf16)
    g2 = (gates - g1.astype(f32)).astype(bf16)
    return main, g1, g2


def _layer(l, rope, x, moe_prev, mods, norm1_g, norm2_g, w_in, dn_conv, dn_a_log, dn_dt_bias, dn_norm_g,
           q_norm_g, k_norm_g, ml_i_bias, ml_f_bias, ml_norm_g, w_out, rw1, rw2, router_bias,
           w_gate, w_up, w_down):
    t = x.shape[0]
    w_main, wg1, wg2 = _prep_in_weights(w_in[l])
    x, h, graw = _norm1(x, moe_prev, mods[l - 1] if l else None, mods[l], norm1_g[l], wg1, wg2)
    p = _matmul(h, w_main, bf16)
    conv_w = jnp.pad(dn_conv[l], ((0, 8 - CONV_W), (0, 0)))
    gate_params = jnp.zeros((8, 128), f32)
    gate_params = gate_params.at[0, 8:16].set(dn_dt_bias[l].reshape(8))
    gate_params = gate_params.at[0, 16:24].set(ml_i_bias[l].reshape(8))
    gate_params = gate_params.at[0, 24:32].set(ml_f_bias[l].reshape(8))
    gate_params = gate_params.at[1, 8:16].set(dn_a_log[l].reshape(8))
    dnq, gcol, grow = _scan_prep(p, graw, conv_w, gate_params)
    dnf, dnb = _gdn_scan(dnq, gcol, grow)
    mlf, mlb = _mlstm_scan(p, gcol, grow)
    qr, kr, va = _attn_prep(p, q_norm_g[l], k_norm_g[l], rope)
    at = _attention(qr, kr, va)
    x, h2, logits = _merge_outproj(dnf, dnb, p, at, mlf, mlb, x, mods[l], dn_norm_g[l], ml_norm_g[l],
                                   w_out[l].astype(bf16), norm2_g[l], rw1, rw2)
    e_rows, g_rows = _route(logits, router_bias)
    n_blocks = (2 * t + N_EXPERTS * (MOE_BM - 1) + MOE_BM - 1) // MOE_BM
    blk_e, tok, dst, wt = _dispatch(e_rows, g_rows, n_blocks)
    moe = _moe(h2, blk_e, tok, dst, wt, _to_bf16(w_gate, l), _to_bf16(w_up, l), _to_bf16(w_down, l))
    return x, moe


def kernel(x, c, ctx, c_ctx, w_mod, b_mod, norm1_g, norm2_g, w_in, dn_conv, dn_a_log, dn_dt_bias, dn_norm_g, q_norm_g, k_norm_g, ml_i_bias, ml_f_bias, ml_norm_g, w_out, router_w, router_bias, w_gate, w_up, w_down, final_norm_g):
    b, seq, d = x.shape
    assert b == 1 and d == D and ctx.shape[1] == BLK and seq % BLK == 0 and seq % GRID_W == 0
    depth = w_mod.shape[0]
    mods = _mods(c, c_ctx, w_mod, b_mod)
    xs = jnp.concatenate([ctx[0], x[0]], axis=0)
    rw = jnp.pad(router_w, ((0, 0), (0, 128 - N_EXPERTS)))
    rw1 = rw.astype(bf16)
    rw2 = (rw - rw1.astype(f32)).astype(bf16)
    moe = None
    rope = _rope_tables(xs.shape[0])
    for l in range(depth):
        xs, moe = _layer(l, rope, xs, moe, mods, norm1_g, norm2_g, w_in, dn_conv, dn_a_log, dn_dt_bias,
                         dn_norm_g, q_norm_g, k_norm_g, ml_i_bias, ml_f_bias, ml_norm_g, w_out, rw1, rw2,
                         router_bias, w_gate, w_up, w_down)
    out = _final(xs, moe, mods[depth - 1], final_norm_g)
    return out.reshape(b, seq, d)
```

```python
import functools

import jax
import jax.numpy as jnp
from jax import lax
from jax.experimental import pallas as pl
from jax.experimental.pallas import tpu as pltpu

f32 = jnp.float32
bf16 = jnp.bfloat16

D = 2048
HD = 128
N_HEADS_SCAN = 4
H_AT = 8
H_KV = 2
CHUNK = 64
BLK = 256
CPB = BLK // CHUNK
GRID_W = 64
ROPE_THETA = 10000.0
QK_SCALE = HD ** -0.5
LOG2E = 1.4426950408889634
N_EXPERTS = 16
EXPERTS_PER_GROUP = 4
D_EXPERT = D // 2
MOE_BM = 256
EPS = 1e-6
NEG = -1e30
CONV_W = 5

C_DNQKV, C_DNZ, C_ATQ, C_ATK, C_ATV, C_MLQ, C_MLK, C_MLV, C_MLO, P_COLS = (
    0, 1536, 2048, 3072, 3328, 3584, 4096, 4608, 5120, 5632)

V7X_VMEM_LIMIT_MB = 56


def _cparams(sems, vmem_mb=None):
    return pltpu.CompilerParams(
        dimension_semantics=sems,
        vmem_limit_bytes=None if vmem_mb is None else vmem_mb << 20)


def _mm(a, b):
    return jnp.dot(a.astype(bf16), b.astype(bf16), preferred_element_type=f32)


def _mm_nt(a, b):
    return lax.dot_general(a.astype(bf16), b.astype(bf16), (((1,), (1,)), ((), ())),
                           preferred_element_type=f32)


def _mm_tn(a, b):
    return lax.dot_general(a.astype(bf16), b.astype(bf16), (((0,), (0,)), ((), ())),
                           preferred_element_type=f32)


def _split3(x):
    x1 = x.astype(bf16)
    r1 = x - x1.astype(f32)
    x2 = r1.astype(bf16)
    x3 = (r1 - x2.astype(f32)).astype(bf16)
    return x1, x2, x3


def _sigmoid(x):
    return 1.0 / (1.0 + jnp.exp(-x))


def _softplus(x):
    return jnp.maximum(x, 0.0) + jnp.log(1.0 + jnp.exp(-jnp.abs(x)))


def _mod_kernel(s_ref, w_ref, b_ref, o_ref):
    s = s_ref[...]
    s = s * _sigmoid(s)
    o_ref[0] = jnp.dot(s, w_ref[0], preferred_element_type=f32,
                       precision=lax.Precision.HIGHEST) + b_ref[0]


def _mods(c, c_ctx, w_mod, b_mod):
    depth, d, n6 = w_mod.shape
    s = jnp.zeros((8, d), f32).at[0].set(c[0]).at[1].set(c_ctx)
    tn = 1024
    return pl.pallas_call(
        _mod_kernel,
        grid=(depth, n6 // tn),
        in_specs=[pl.BlockSpec((8, d), lambda l, j: (0, 0)),
                  pl.BlockSpec((1, d, tn), lambda l, j: (l, 0, j)),
                  pl.BlockSpec((1, 1, tn), lambda l, j: (l, 0, j))],
        out_specs=pl.BlockSpec((1, 8, tn), lambda l, j: (l, 0, j)),
        out_shape=jax.ShapeDtypeStruct((depth, 8, n6), f32),
        compiler_params=_cparams(("parallel", "parallel"), 40),
        name="mod_vectors",
    )(s, w_mod, b_mod.reshape(depth, 1, n6))


def _mod_rows(mod_ref, k, is_ctx):
    lat = mod_ref[0:1, k * D:(k + 1) * D]
    ctx = mod_ref[1:2, k * D:(k + 1) * D]
    return jnp.where(is_ctx, ctx, lat)


def _norm1_kernel(has_moe, tm, *refs):
    if has_moe:
        x_ref, y0_ref, y1_ref, modp_ref, mod_ref, g_ref, wg1_ref, wg2_ref, xo_ref, h_ref, gr_ref = refs
    else:
        x_ref, mod_ref, g_ref, wg1_ref, wg2_ref, h_ref, gr_ref = refs
    i = pl.program_id(0)
    rows = i * tm + lax.broadcasted_iota(jnp.int32, (tm, 1), 0)
    is_ctx = rows < BLK
    x = x_ref[...]
    if has_moe:
        x = x + _mod_rows(modp_ref, 5, is_ctx) * (y0_ref[...] + y1_ref[...])
        xo_ref[...] = x
    ms = jnp.mean(x * x, axis=-1, keepdims=True)
    y = x * lax.rsqrt(ms + EPS) * g_ref[...]
    h = y * (1.0 + _mod_rows(mod_ref, 1, is_ctx)) + _mod_rows(mod_ref, 0, is_ctx)
    hh = h.astype(bf16)
    h_ref[...] = hh
    hl = (h - hh.astype(f32)).astype(bf16)
    wg1 = wg1_ref[...]
    gr_ref[...] = (jnp.dot(hh, wg1, preferred_element_type=f32)
                   + jnp.dot(hl, wg1, preferred_element_type=f32)
                   + jnp.dot(hh, wg2_ref[...], preferred_element_type=f32))


def _norm1(x, moe, mod_prev, mod_cur, g, wg1, wg2):
    t = x.shape[0]
    tm = 256
    nrow = t // tm
    row = lambda i: (i, 0)
    full = lambda i: (0, 0)
    in_specs = [pl.BlockSpec((tm, D), row)]
    args = [x]
    if moe is not None:
        in_specs += [pl.BlockSpec((tm, D), row), pl.BlockSpec((tm, D), lambda i: (i + nrow, 0)),
                     pl.BlockSpec((8, 6 * D), full)]
        args += [moe, moe, mod_prev]
    in_specs += [pl.BlockSpec((8, 6 * D), full), pl.BlockSpec((1, D), full),
                 pl.BlockSpec((D, 128), full), pl.BlockSpec((D, 128), full)]
    args += [mod_cur, g.reshape(1, D), wg1, wg2]
    out_specs = [pl.BlockSpec((tm, D), row), pl.BlockSpec((tm, 128), row)]
    out_shape = [jax.ShapeDtypeStruct((t, D), bf16), jax.ShapeDtypeStruct((t, 128), f32)]
    if moe is not None:
        out_specs = [pl.BlockSpec((tm, D), row)] + out_specs
        out_shape = [jax.ShapeDtypeStruct((t, D), f32)] + out_shape
    outs = pl.pallas_call(
        functools.partial(_norm1_kernel, moe is not None, tm),
        grid=(nrow,), in_specs=in_specs, out_specs=out_specs, out_shape=out_shape,
        compiler_params=_cparams(("parallel",), 40),
        name="norm1_modulate",
    )(*args)
    if moe is not None:
        return outs
    return [x] + list(outs)


def _matmul_kernel(a_ref, b_ref, o_ref):
    o_ref[...] = jnp.dot(a_ref[...], b_ref[...], preferred_element_type=f32).astype(o_ref.dtype)


def _row_tile(t, choices):
    for c in choices:
        if t % c == 0:
            return c
    raise ValueError(f"no row tile for {t}")


def _matmul(a, b, out_dtype):
    m, k = a.shape
    n = b.shape[1]
    tm = _row_tile(m, (1280, 768, 512, 256))
    tn = _row_tile(n, (1408, 1024, 512))
    return pl.pallas_call(
        _matmul_kernel,
        grid=(n // tn, m // tm),
        in_specs=[pl.BlockSpec((tm, k), lambda j, i: (i, 0)),
                  pl.BlockSpec((k, tn), lambda j, i: (0, j))],
        out_specs=pl.BlockSpec((tm, tn), lambda j, i: (i, j)),
        out_shape=jax.ShapeDtypeStruct((m, n), out_dtype),
        compiler_params=_cparams(("parallel", "parallel"), 48),
        name="in_projection",
    )(a, b)


def _scan_prep_kernel(nblk, cur_ref, prev_ref, next_ref, cw_ref, graw_ref, gp_ref, q_ref, gc_ref, grow_ref, xs):
    i = pl.program_id(0)
    has_prev = i >= 2
    has_next = jnp.logical_and(i >= 1, i < nblk - 1)
    xs[0:16, :] = jnp.where(has_prev, prev_ref[...].astype(f32), 0.0)
    xs[16:16 + BLK, :] = cur_ref[...].astype(f32)
    xs[16 + BLK:32 + BLK, :] = jnp.where(has_next, next_ref[...].astype(f32), 0.0)
    acc = cw_ref[0:1, :] * xs[pl.ds(16 - CONV_W // 2, BLK), :]
    for j in range(1, CONV_W):
        acc = acc + cw_ref[j:j + 1, :] * xs[pl.ds(16 - CONV_W // 2 + j, BLK), :]
    a = acc * _sigmoid(acc)
    w = N_HEADS_SCAN * HD
    for h in range(2 * N_HEADS_SCAN):
        xh = a[:, h * HD:(h + 1) * HD]
        inv = lax.rsqrt(jnp.sum(xh * xh, axis=-1, keepdims=True) + EPS)
        scale = QK_SCALE if h < N_HEADS_SCAN else 1.0
        q_ref[:, h * HD:(h + 1) * HD] = (xh * (inv * scale)).astype(bf16)
    q_ref[:, 2 * w:3 * w] = a[:, 2 * w:3 * w].astype(bf16)

    z = graw_ref[...] + gp_ref[0:1, :]
    lane = lax.broadcasted_iota(jnp.int32, (BLK, 128), 1)
    sp = _softplus(z)
    vals = jnp.where(lane < 8, _sigmoid(z),
                     jnp.where(lane < 16, -jnp.exp(gp_ref[1:2, :]) * sp,
                               jnp.where(lane < 24, z, z - sp)))
    r = lax.broadcasted_iota(jnp.int32, (BLK, BLK), 0)
    c = lax.broadcasted_iota(jnp.int32, (BLK, BLK), 1)
    same = jnp.right_shift(r, 6) == jnp.right_shift(c, 6)
    tri_lo = jnp.where(jnp.logical_and(same, r >= c), 1.0, 0.0).astype(bf16)
    tri_up = jnp.where(jnp.logical_and(same, r <= c), 1.0, 0.0).astype(bf16)
    v1, v2, v3 = _split3(vals)
    dot = functools.partial(jnp.dot, preferred_element_type=f32)
    prefix = dot(tri_lo, v1) + dot(tri_lo, v2) + dot(tri_lo, v3)
    suffix = dot(tri_up, v1) + dot(tri_up, v2) + dot(tri_up, v3)
    is_cum = jnp.logical_and(jnp.bitwise_and(lane, 8) == 8, lane < 32)
    is_bwd = jnp.bitwise_and(lane, 4) == 4
    out = jnp.where(is_cum, jnp.where(is_bwd, suffix, prefix), vals)
    gc_ref[...] = out
    gt = out.T
    for cc in range(CPB):
        grow_ref[cc] = gt[0:32, cc * CHUNK:(cc + 1) * CHUNK]


def _scan_prep(p, graw, conv_w, gate_params):
    t = p.shape[0]
    nblk = t // BLK
    wq = 3 * N_HEADS_SCAN * HD
    n16 = t // 16
    return pl.pallas_call(
        functools.partial(_scan_prep_kernel, nblk),
        grid=(nblk,),
        in_specs=[pl.BlockSpec((BLK, wq), lambda i: (i, 0)),
                  pl.BlockSpec((16, wq), lambda i: (jnp.maximum(i * (BLK // 16) - 1, 0), 0)),
                  pl.BlockSpec((16, wq), lambda i: (jnp.minimum((i + 1) * (BLK // 16), n16 - 1), 0)),
                  pl.BlockSpec((8, wq), lambda i: (0, 0)),
                  pl.BlockSpec((BLK, 128), lambda i: (i, 0)),
                  pl.BlockSpec((8, 128), lambda i: (0, 0))],
        out_specs=[pl.BlockSpec((BLK, wq), lambda i: (i, 0)),
                   pl.BlockSpec((BLK, 128), lambda i: (i, 0)),
                   pl.BlockSpec((CPB, 32, CHUNK), lambda i: (i, 0, 0))],
        out_shape=[jax.ShapeDtypeStruct((t, wq), bf16),
                   jax.ShapeDtypeStruct((t, 128), f32),
                   jax.ShapeDtypeStruct((t // CHUNK, 32, CHUNK), f32)],
        scratch_shapes=[pltpu.VMEM((BLK + 32, wq), f32)],
        compiler_params=_cparams(("parallel",), 40),
        name="scan_prep",
    )(p, p, p, conv_w, graw, gate_params)


def _tri_masks():
    r = lax.broadcasted_iota(jnp.int32, (CHUNK, CHUNK), 0)
    c = lax.broadcasted_iota(jnp.int32, (CHUNK, CHUNK), 1)
    blk = jnp.right_shift(r, 4) == jnp.right_shift(c, 4)
    eye = jnp.where(r == c, 1.0, 0.0)
    return (r >= c, r <= c), (r > c, r < c), blk, eye


def _gdn_kernel(qf_ref, qb_ref, gcf_ref, gcb_ref, grf_ref, grb_ref, of_ref, ob_ref, s_scr):
    @pl.when(pl.program_id(0) == 0)
    def _():
        s_scr[...] = jnp.zeros_like(s_scr)

    incl, strict, blk, eye = _tri_masks()
    w = N_HEADS_SCAN * HD

    units = [(d, h) for d in range(2) for h in range(N_HEADS_SCAN)]
    rows_cat = lambda a, b: jnp.concatenate([a, b], axis=0)
    cols_cat = lambda a, b: jnp.concatenate([a, b], axis=1)
    C = CHUNK

    def chunk_pair(pp, carry):
        ld, tags = [], []
        for off in range(2):
            for d, h in units:
                cc = 2 * pp + off
                c = cc if d == 0 else CPB - 1 - cc
                q_ref, gc_ref, gr_ref = (qf_ref, gcf_ref, grf_ref) if d == 0 else (qb_ref, gcb_ref, grb_ref)
                rows = pl.ds(pl.multiple_of(c * C, C), C)
                u_idx = d * N_HEADS_SCAN + h
                q = q_ref[rows, h * HD:(h + 1) * HD]
                k = q_ref[rows, w + h * HD:w + (h + 1) * HD]
                v = q_ref[rows, 2 * w + h * HD:2 * w + (h + 1) * HD]
                beta = gc_ref[rows, u_idx:u_idx + 1]
                cum_c = gc_ref[rows, 8 + u_idx:9 + u_idx]
                cum_r = gr_ref[c][8 + u_idx:9 + u_idx, :]
                tot = cum_c[C - 1:C, :] if d == 0 else cum_c[0:1, :]
                ld.append((rows, q, k, v, beta, cum_c, cum_r, tot))
                tags.append((d, h))
        g1 = [_mm_nt(rows_cat(k, q), k) for (_, q, k, *_) in ld]
        st = []
        for (d, h), (rows, q, k, v, beta, cum_c, cum_r, tot), g in zip(tags, ld, g1):
            decay = jnp.exp(jnp.where(incl[d], cum_c - cum_r, NEG))
            nm = jnp.where(strict[d], beta * g[:C] * decay, 0.0)
            dm = jnp.where(blk, nm, 0.0)
            kf = k.astype(f32)
            e_c = jnp.exp(cum_c)
            rhs = cols_cat(cols_cat((beta * e_c) * kf, beta * v.astype(f32)), nm - dm)
            st.append(dict(dm=dm, rhs=rhs, qk=g[C:] * decay, k_dec=kf * jnp.exp(tot - cum_c),
                           q_dec=q.astype(f32) * e_c, g_last=jnp.exp(tot), p1=eye - dm))
        m2 = [_mm(s["dm"], s["dm"]) for s in st]
        r = [_mm(rows_cat(s["p1"], m), m) for s, m in zip(st, m2)]
        p2 = [s["p1"] + x[:C] for s, x in zip(st, r)]
        m4 = [x[C:] for x in r]
        r = [_mm(rows_cat(p, m), m) for p, m in zip(p2, m4)]
        p3 = [p + x[:C] for p, x in zip(p2, r)]
        m8 = [x[C:] for x in r]
        dinv = [p + _mm(p, m) for p, m in zip(p3, m8)]
        r = [_mm(di, s["rhs"]) for di, s in zip(dinv, st)]
        t1 = [x[:, :2 * HD] for x in r]
        qm = [x[:, 2 * HD:] for x in r]
        r = [_mm(qq, cols_cat(t, qq)) for qq, t in zip(qm, t1)]
        a1 = [x[:, :2 * HD] for x in r]
        qm2 = [x[:, 2 * HD:] for x in r]
        b2 = [_mm(q2, t) for q2, t in zip(qm2, t1)]
        c3 = [_mm(qq, b) for qq, b in zip(qm, b2)]
        sol = [t - a + b - c for t, a, b, c in zip(t1, a1, b2, c3)]
        nu = len(units)
        for off in range(2):
            sl = slice(off * nu, (off + 1) * nu)
            s_old = [s_scr[i] for i in range(nu)]
            r = [_mm(rows_cat(x[:, :HD], s["q_dec"]), so) for x, s, so in zip(sol[sl], st[sl], s_old)]
            u = [x[:, HD:] - y[:C] for x, y in zip(sol[sl], r)]
            o_intra = [_mm(s["qk"], uu) for s, uu in zip(st[sl], u)]
            s_add = [_mm_tn(s["k_dec"], uu) for s, uu in zip(st[sl], u)]
            for i, ((d, h), l, s) in enumerate(zip(units, ld[sl], st[sl])):
                o_ref = of_ref if d == 0 else ob_ref
                o_ref[l[0], h * HD:(h + 1) * HD] = r[i][C:] + o_intra[i]
                s_scr[i] = s["g_last"] * s_old[i] + s_add[i]
        return carry

    lax.fori_loop(0, CPB // 2, chunk_pair, 0)


def _bwd_block(nblk):
    return lambda s: (jnp.where(s == 0, 0, nblk - s), 0)


def _gdn_scan(qkv, gcol, grow):
    t = qkv.shape[0]
    nblk = t // BLK
    wq = 3 * N_HEADS_SCAN * HD
    w = N_HEADS_SCAN * HD
    fwd = lambda s: (s, 0)
    bwd = _bwd_block(nblk)
    fwd3 = lambda s: (s, 0, 0)
    bwd3 = lambda s: (jnp.where(s == 0, 0, nblk - s), 0, 0)
    return pl.pallas_call(
        _gdn_kernel,
        grid=(nblk,),
        in_specs=[pl.BlockSpec((BLK, wq), fwd), pl.BlockSpec((BLK, wq), bwd),
                  pl.BlockSpec((BLK, 128), fwd), pl.BlockSpec((BLK, 128), bwd),
                  pl.BlockSpec((CPB, 32, CHUNK), fwd3), pl.BlockSpec((CPB, 32, CHUNK), bwd3)],
        out_specs=[pl.BlockSpec((BLK, w), fwd), pl.BlockSpec((BLK, w), bwd)],
        out_shape=[jax.ShapeDtypeStruct((t, w), f32), jax.ShapeDtypeStruct((t, w), f32)],
        scratch_shapes=[pltpu.VMEM((2 * N_HEADS_SCAN, HD, HD), f32)],
        compiler_params=_cparams(("arbitrary",), 40),
        name="gdn_scan",
    )(qkv, qkv, gcol, gcol, grow, grow)


def _mlstm_kernel(pf_q, pf_k, pf_v, pb_q, pb_k, pb_v, gcf_ref, gcb_ref, grf_ref, grb_ref,
                  of_ref, ob_ref, c_scr, m_scr):
    @pl.when(pl.program_id(0) == 0)
    def _():
        c_scr[...] = jnp.zeros_like(c_scr)
        m_scr[...] = jnp.full_like(m_scr, NEG)

    incl, _, _, _ = _tri_masks()
    ones_col = jnp.where(lax.broadcasted_iota(jnp.int32, (CHUNK, HD), 1) == 0, 1.0, 0.0).astype(bf16)

    units = [(d, h) for d in range(2) for h in range(N_HEADS_SCAN)]

    def chunk(cc, carry):
        ld = []
        for d, h in units:
            c = cc if d == 0 else CPB - 1 - cc
            q_ref, k_ref, v_ref, gc_ref, gr_ref = (
                (pf_q, pf_k, pf_v, gcf_ref, grf_ref) if d == 0 else (pb_q, pb_k, pb_v, gcb_ref, grb_ref))
            rows = pl.ds(pl.multiple_of(c * CHUNK, CHUNK), CHUNK)
            u_idx = d * N_HEADS_SCAN + h
            q = q_ref[rows, h * HD:(h + 1) * HD]
            k = k_ref[rows, h * HD:(h + 1) * HD]
            v = v_ref[rows, h * HD:(h + 1) * HD]
            i_c = gc_ref[rows, 16 + u_idx:17 + u_idx]
            b_c = gc_ref[rows, 24 + u_idx:25 + u_idx]
            grow = gr_ref[c]
            i_r = grow[16 + u_idx:17 + u_idx, :]
            b_r = grow[24 + u_idx:25 + u_idx, :]
            b_last = b_c[CHUNK - 1:CHUNK, :] if d == 0 else b_c[0:1, :]
            w_log = b_last - b_c + i_c
            m_st = jnp.max(w_log, axis=0, keepdims=True)
            e_w = jnp.exp(w_log - m_st)
            ld.append(dict(rows=rows, q=q, k=k, v=v, i_r=i_r, b_c=b_c, b_r=b_r, b_last=b_last, m_st=m_st, e_w=e_w))
        nu = len(units)
        v_aug = [jnp.concatenate([l["v"], ones_col], axis=1) for l in ld]
        qk = [_mm_nt(l["q"], l["k"]) for l in ld]
        kv = [_mm_tn(l["k"], l["e_w"] * va.astype(f32)) for l, va in zip(ld, v_aug)]
        c_old = [c_scr[i] for i in range(nu)]
        qc = [_mm(l["q"], cm) for l, cm in zip(ld, c_old)]
        ps, m_locs = [], []
        for (d, h), l, g in zip(units, ld, qk):
            d_log = jnp.where(incl[d], l["b_c"] - l["b_r"] + l["i_r"], NEG)
            m_loc = jnp.max(d_log, axis=-1, keepdims=True)
            ps.append(jnp.exp(d_log - m_loc) * (g * QK_SCALE))
            m_locs.append(m_loc)
        loc = [_mm(p, va) for p, va in zip(ps, v_aug)]
        m_old = [m_scr[i][0:1, 0:1] for i in range(nu)]
        inter = [l["b_c"] + m for l, m in zip(ld, m_old)]
        m_r = [jnp.maximum(a, b) for a, b in zip(inter, m_locs)]
        a_in = [jnp.exp(a - b) for a, b in zip(inter, m_r)]
        a_lo = [jnp.exp(a - b) for a, b in zip(m_locs, m_r)]
        floor = [jnp.exp(-b) for b in m_r]
        m_new = [jnp.maximum(l["b_last"] + m, l["m_st"]) for l, m in zip(ld, m_old)]
        s_old = [jnp.exp(l["b_last"] + m - mn) for l, m, mn in zip(ld, m_old, m_new)]
        s_new = [jnp.exp(l["m_st"] - mn) * QK_SCALE for l, mn in zip(ld, m_new)]
        for i, ((d, h), l) in enumerate(zip(units, ld)):
            o_ref = of_ref if d == 0 else ob_ref
            num = a_in[i] * qc[i][:, :HD] + a_lo[i] * loc[i][:, :HD]
            den = a_in[i] * qc[i][:, HD:HD + 1] + a_lo[i] * loc[i][:, HD:HD + 1]
            o_ref[l["rows"], h * HD:(h + 1) * HD] = num / jnp.maximum(jnp.abs(den), floor[i])
        for i in range(nu):
            c_scr[i] = s_old[i] * c_old[i] + s_new[i] * kv[i]
            m_scr[i] = jnp.broadcast_to(m_new[i], (8, HD))
        return carry

    lax.fori_loop(0, CPB, chunk, 0)


def _mlstm_scan(p, gcol, grow):
    t = p.shape[0]
    nblk = t // BLK
    w = N_HEADS_SCAN * HD
    nu = 2 * N_HEADS_SCAN
    fwd = lambda s: (s, 0)
    bwd = _bwd_block(nblk)
    fwd3 = lambda s: (s, 0, 0)
    bwd3 = lambda s: (jnp.where(s == 0, 0, nblk - s), 0, 0)

    def col(base, bwd_dir):
        cb = base // w
        if bwd_dir:
            return pl.BlockSpec((BLK, w), lambda s: (jnp.where(s == 0, 0, nblk - s), cb))
        return pl.BlockSpec((BLK, w), lambda s: (s, cb))

    return pl.pallas_call(
        _mlstm_kernel,
        grid=(nblk,),
        in_specs=[col(C_MLQ, False), col(C_MLK, False), col(C_MLV, False),
                  col(C_MLQ, True), col(C_MLK, True), col(C_MLV, True),
                  pl.BlockSpec((BLK, 128), fwd), pl.BlockSpec((BLK, 128), bwd),
                  pl.BlockSpec((CPB, 32, CHUNK), fwd3), pl.BlockSpec((CPB, 32, CHUNK), bwd3)],
        out_specs=[pl.BlockSpec((BLK, w), fwd), pl.BlockSpec((BLK, w), bwd)],
        out_shape=[jax.ShapeDtypeStruct((t, w), f32), jax.ShapeDtypeStruct((t, w), f32)],
        scratch_shapes=[pltpu.VMEM((nu, HD, 2 * HD), f32), pltpu.VMEM((nu, 8, HD), f32)],
        compiler_params=_cparams(("arbitrary",), 40),
        name="mlstm_scan",
    )(p, p, p, p, p, p, gcol, gcol, grow, grow)


def _rope_kernel(cos_ref, sin_ref):
    i = pl.program_id(0)
    r = lax.broadcasted_iota(jnp.int32, (BLK, HD), 0)
    lane = lax.broadcasted_iota(jnp.int32, (BLK, HD), 1)
    tok = (i - 1) * BLK + r
    pos = jnp.where(lane < HD // 2, jnp.right_shift(tok, 6), jnp.bitwise_and(tok, GRID_W - 1)).astype(f32)
    pair = jnp.bitwise_and(lane, HD // 4 - 1).astype(f32)
    inv_freq = jnp.exp(pair * (-jnp.log(ROPE_THETA) / (HD // 4)))
    ang = pos * inv_freq
    is_ctx = i == 0
    sin = jnp.where(is_ctx, 0.0, jnp.sin(ang))
    first = jnp.bitwise_and(lane, HD // 4) == 0
    cos_ref[...] = jnp.where(is_ctx, 1.0, jnp.cos(ang))
    sin_ref[...] = jnp.where(first, -sin, sin)


def _rope_tables(t):
    spec = pl.BlockSpec((BLK, HD), lambda i: (i, 0))
    return pl.pallas_call(
        _rope_kernel, grid=(t // BLK,), in_specs=[], out_specs=[spec, spec],
        out_shape=[jax.ShapeDtypeStruct((t, HD), f32)] * 2,
        compiler_params=_cparams(("parallel",)), name="rope_tables",
    )()


def _attn_prep_kernel(q_ref, k_ref, v_ref, qg_ref, kg_ref, cos_ref, sin_ref, qo_ref, ko_ref, vo_ref):
    lane = lax.broadcasted_iota(jnp.int32, (BLK, HD), 1)
    first = jnp.bitwise_and(lane, HD // 4) == 0
    cos = cos_ref[...]
    sin_signed = sin_ref[...]

    def norm_rope(x, g, scale):
        y = x * lax.rsqrt(jnp.mean(x * x, axis=-1, keepdims=True) + EPS) * g
        partner = jnp.where(first, pltpu.roll(y, HD - HD // 4, 1), pltpu.roll(y, HD // 4, 1))
        return (y * cos + partner * sin_signed) * scale

    for h in range(H_AT):
        x = q_ref[:, h * HD:(h + 1) * HD].astype(f32)
        qo_ref[:, h * HD:(h + 1) * HD] = norm_rope(x, qg_ref[...], QK_SCALE * LOG2E).astype(bf16)
    ones_col = jnp.where(lax.broadcasted_iota(jnp.int32, (BLK, HD), 1) == 0, 1.0, 0.0).astype(bf16)
    for h in range(H_KV):
        x = k_ref[:, h * HD:(h + 1) * HD].astype(f32)
        ko_ref[:, h * HD:(h + 1) * HD] = norm_rope(x, kg_ref[...], 1.0).astype(bf16)
        vo_ref[:, 2 * h * HD:(2 * h + 1) * HD] = v_ref[:, h * HD:(h + 1) * HD]
        vo_ref[:, (2 * h + 1) * HD:(2 * h + 2) * HD] = ones_col


def _attn_prep(p, q_g, k_g, rope):
    t = p.shape[0]
    wq, wk = H_AT * HD, H_KV * HD
    return pl.pallas_call(
        _attn_prep_kernel,
        grid=(t // BLK,),
        in_specs=[pl.BlockSpec((BLK, wq), lambda i: (i, C_ATQ // wq)),
                  pl.BlockSpec((BLK, wk), lambda i: (i, C_ATK // wk)),
                  pl.BlockSpec((BLK, wk), lambda i: (i, C_ATV // wk)),
                  pl.BlockSpec((1, HD), lambda i: (0, 0)),
                  pl.BlockSpec((1, HD), lambda i: (0, 0)),
                  pl.BlockSpec((BLK, HD), lambda i: (i, 0)),
                  pl.BlockSpec((BLK, HD), lambda i: (i, 0))],
        out_specs=[pl.BlockSpec((BLK, wq), lambda i: (i, 0)), pl.BlockSpec((BLK, wk), lambda i: (i, 0)),
                   pl.BlockSpec((BLK, 2 * wk), lambda i: (i, 0))],
        out_shape=[jax.ShapeDtypeStruct((t, wq), bf16), jax.ShapeDtypeStruct((t, wk), bf16),
                   jax.ShapeDtypeStruct((t, 2 * wk), bf16)],
        compiler_params=_cparams(("parallel",), 40),
        name="attn_prep",
    )(p, p, p, q_g.reshape(1, HD), k_g.reshape(1, HD), *rope)


def _attn_kernel(tq, tk, n_ctx_tiles, n_main, q_ref, k_ref, v_ref, o_ref, m_scr, acc_scr, sa_scr, sb_scr):
    qi = pl.program_id(1)
    grp = H_AT // H_KV
    m_scr[...] = jnp.full_like(m_scr, NEG)
    acc_scr[...] = jnp.zeros_like(acc_scr)

    def scores(rows):
        kt = k_ref[rows, :]
        return [_mm_nt(q_ref[:, h * HD:(h + 1) * HD], kt) for h in range(grp)]

    def softmax_pv(get_s, rows, width):
        va = v_ref[rows, :]
        ps = []
        for h in range(grp):
            mx = get_s(h, 0)
            for c in range(1, width // HD):
                mx = jnp.maximum(mx, get_s(h, c))
            m_prev = m_scr[h]
            m_new = jnp.maximum(m_prev, jnp.max(mx, axis=-1, keepdims=True))
            alpha = jnp.exp2(m_prev - m_new)
            p = jnp.concatenate([jnp.exp2(get_s(h, c) - m_new).astype(bf16) for c in range(width // HD)], axis=1)
            m_scr[h] = m_new
            ps.append((alpha, p))
        for h in range(grp):
            alpha, p = ps[h]
            acc = acc_scr[h]
            pv = jnp.dot(p, va, preferred_element_type=f32)
            acc_scr[h] = jnp.concatenate([alpha * acc[:, :HD], alpha * acc[:, HD:]], axis=1) + pv

    @pl.when(qi < n_ctx_tiles)
    def _():
        ctx_rows = pl.ds(0, BLK)
        ss = scores(ctx_rows)
        softmax_pv(lambda h, c: ss[h][:, c * HD:(c + 1) * HD], ctx_rows, BLK)

    def main_rows(j):
        return pl.ds(pl.multiple_of(j * tk, HD), tk)

    def store_scores(s_ref, j):
        for h, s in enumerate(scores(main_rows(j))):
            s_ref[h] = s

    def pipelined_step(cur_ref, nxt_ref, j):
        store_scores(nxt_ref, jnp.minimum(j + 1, n_main - 1))
        softmax_pv(lambda h, c: cur_ref[h, :, c * HD:(c + 1) * HD], main_rows(j), tk)

    @pl.when(qi >= n_ctx_tiles)
    def _():
        store_scores(sa_scr, 0)

        def body(i, carry):
            pipelined_step(sa_scr, sb_scr, 2 * i)
            pipelined_step(sb_scr, sa_scr, 2 * i + 1)
            return carry
        lax.fori_loop(0, n_main // 2, body, 0)
        if n_main % 2:
            pipelined_step(sa_scr, sb_scr, n_main - 1)

    for h in range(grp):
        acc = acc_scr[h]
        o_ref[:, h * HD:(h + 1) * HD] = (acc[:, :HD] / acc[:, HD:HD + 1]).astype(o_ref.dtype)


def _attention(qr, kr, va):
    t = qr.shape[0]
    tq = 256
    tk = _row_tile(t, (1280, 768, 256))
    grp = H_AT // H_KV
    wg = grp * HD
    kern = functools.partial(_attn_kernel, tq, tk, BLK // tq, t // tk)
    return pl.pallas_call(
        kern,
        grid=(H_KV, t // tq),
        in_specs=[pl.BlockSpec((tq, wg), lambda g, i: (i, g)),
                  pl.BlockSpec((t, HD), lambda g, i: (0, g)),
                  pl.BlockSpec((t, 2 * HD), lambda g, i: (0, g))],
        out_specs=pl.BlockSpec((tq, wg), lambda g, i: (i, g)),
        out_shape=jax.ShapeDtypeStruct((t, H_AT * HD), bf16),
        scratch_shapes=[pltpu.VMEM((grp, tq, HD), f32), pltpu.VMEM((grp, tq, 2 * HD), f32),
                        pltpu.VMEM((grp, tq, tk), f32), pltpu.VMEM((grp, tq, tk), f32)],
        compiler_params=_cparams(("parallel", "arbitrary"), 52),
        name="flash_attention",
    )(qr, kr, va)


def _merge_kernel(tm, dnf_ref, dnb_ref, z_ref, at_ref, mlf_ref, mlb_ref, og_ref, x_ref, mod_ref,
                  dng_ref, mlg_ref, wo_ref, n2g_ref, rw1_ref, rw2_ref, xo_ref, h2_ref, lg_ref):
    i = pl.program_id(0)
    rows = i * tm + lax.broadcasted_iota(jnp.int32, (tm, 1), 0)
    is_ctx = rows < BLK
    w = N_HEADS_SCAN * HD

    def head_norm(x, g):
        return x * lax.rsqrt(jnp.mean(x * x, axis=-1, keepdims=True) + EPS) * g

    acc = jnp.dot(at_ref[...], wo_ref[w:w + H_AT * HD, :], preferred_element_type=f32)
    dn_parts, ml_parts = [], []
    for h in range(N_HEADS_SCAN):
        sl = slice(h * HD, (h + 1) * HD)
        z = z_ref[:, sl].astype(f32)
        dn_parts.append(head_norm(dnf_ref[:, sl] + dnb_ref[:, sl], dng_ref[...]) * (z * _sigmoid(z)))
        ml_parts.append(head_norm(mlf_ref[:, sl] + mlb_ref[:, sl], mlg_ref[...]) * _sigmoid(og_ref[:, sl].astype(f32)))
    dn = jnp.concatenate(dn_parts, axis=1).astype(bf16)
    ml = jnp.concatenate(ml_parts, axis=1).astype(bf16)
    acc = acc + jnp.dot(dn, wo_ref[0:w, :], preferred_element_type=f32)
    acc = acc + jnp.dot(ml, wo_ref[w + H_AT * HD:, :], preferred_element_type=f32)
    x = x_ref[...] + _mod_rows(mod_ref, 2, is_ctx) * acc
    xo_ref[...] = x
    y = x * lax.rsqrt(jnp.mean(x * x, axis=-1, keepdims=True) + EPS) * n2g_ref[...]
    h2 = y * (1.0 + _mod_rows(mod_ref, 4, is_ctx)) + _mod_rows(mod_ref, 3, is_ctx)
    hh = h2.astype(bf16)
    hb = lax.bitcast_convert_type(hh.astype(f32), jnp.uint32)
    h2_ref[...] = jnp.bitwise_or(jnp.bitwise_and(hb[:, D // 2:], jnp.uint32(0xFFFF0000)),
                                 jnp.right_shift(hb[:, :D // 2], jnp.uint32(16)))
    hl = (h2 - hh.astype(f32)).astype(bf16)
    rw1 = rw1_ref[...]
    lg_ref[...] = (jnp.dot(hh, rw1, preferred_element_type=f32) + jnp.dot(hl, rw1, preferred_element_type=f32)
                   + jnp.dot(hh, rw2_ref[...], preferred_element_type=f32))


def _merge_outproj(dnf, dnb, p, at, mlf, mlb, x, mod, dn_g, ml_g, w_out, n2g, rw1, rw2):
    t = x.shape[0]
    tm = 256
    w = N_HEADS_SCAN * HD
    row = lambda i: (i, 0)
    full = lambda i: (0, 0)
    sw = pl.BlockSpec((tm, w), row)
    return pl.pallas_call(
        functools.partial(_merge_kernel, tm),
        grid=(t // tm,),
        in_specs=[sw, sw, pl.BlockSpec((tm, w), lambda i: (i, C_DNZ // w)),
                  pl.BlockSpec((tm, H_AT * HD), row), sw, sw,
                  pl.BlockSpec((tm, w), lambda i: (i, C_MLO // w)),
                  pl.BlockSpec((tm, D), row), pl.BlockSpec((8, 6 * D), full),
                  pl.BlockSpec((1, HD), full), pl.BlockSpec((1, HD), full),
                  pl.BlockSpec((D, D), full), pl.BlockSpec((1, D), full),
                  pl.BlockSpec((D, 128), full), pl.BlockSpec((D, 128), full)],
        out_specs=[pl.BlockSpec((tm, D), row), pl.BlockSpec((tm, D // 2), row), pl.BlockSpec((tm, 128), row)],
        out_shape=[jax.ShapeDtypeStruct((t, D), f32), jax.ShapeDtypeStruct((t, D // 2), jnp.uint32),
                   jax.ShapeDtypeStruct((t, 128), f32)],
        compiler_params=_cparams(("parallel",), 48),
        name="merge_outproj",
    )(dnf, dnb, p, at, mlf, mlb, p, x, mod, dn_g.reshape(1, HD), ml_g.reshape(1, HD), w_out,
      n2g.reshape(1, D), rw1, rw2)


def _route_kernel(lg_ref, bias_ref, e_ref, g_ref):
    lt = lg_ref[...].T
    sc = [_sigmoid(lt[e:e + 1, :]) for e in range(N_EXPERTS)]
    bi = [sc[e] + bias_ref[e:e + 1, 0:1] for e in range(N_EXPERTS)]
    n_groups = N_EXPERTS // EXPERTS_PER_GROUP
    best, best_g = None, None
    for g in range(n_groups):
        a, b, c, d = bi[4 * g:4 * g + 4]
        gs = jnp.maximum(jnp.maximum(jnp.maximum(a + b, a + c), jnp.maximum(a + d, b + c)),
                         jnp.maximum(b + d, c + d))
        if g == 0:
            best, best_g = gs, jnp.zeros_like(gs, dtype=jnp.int32)
        else:
            better = gs > best
            best = jnp.where(better, gs, best)
            best_g = jnp.where(better, g, best_g)
    t1 = jnp.full_like(best, -jnp.inf)
    t2 = jnp.full_like(best, -jnp.inf)
    i1 = jnp.zeros_like(best_g)
    i2 = jnp.zeros_like(best_g)
    s1 = jnp.zeros_like(best)
    s2 = jnp.zeros_like(best)
    for e in range(N_EXPERTS):
        v = jnp.where(best_g == e // EXPERTS_PER_GROUP, bi[e], -jnp.inf)
        gt1 = v > t1
        gt2 = jnp.logical_and(jnp.logical_not(gt1), v > t2)
        t2 = jnp.where(gt1, t1, jnp.where(gt2, v, t2))
        i2 = jnp.where(gt1, i1, jnp.where(gt2, e, i2))
        s2 = jnp.where(gt1, s1, jnp.where(gt2, sc[e], s2))
        t1 = jnp.where(gt1, v, t1)
        i1 = jnp.where(gt1, e, i1)
        s1 = jnp.where(gt1, sc[e], s1)
    tot = s1 + s2
    zi = jnp.zeros_like(i1)
    zf = jnp.zeros_like(s1)
    e_ref[...] = jnp.concatenate([i1, i2, zi, zi, zi, zi, zi, zi], axis=0)
    g_ref[...] = jnp.concatenate([s1 / tot, s2 / tot, zf, zf, zf, zf, zf, zf], axis=0)


def _route(logits, router_bias):
    t = logits.shape[0]
    tm = 256
    bias = jnp.zeros((N_EXPERTS, 128), f32).at[:, 0].set(router_bias)
    return pl.pallas_call(
        _route_kernel,
        grid=(t // tm,),
        in_specs=[pl.BlockSpec((tm, 128), lambda i: (i, 0)), pl.BlockSpec((N_EXPERTS, 128), lambda i: (0, 0))],
        out_specs=[pl.BlockSpec((8, tm), lambda i: (0, i)), pl.BlockSpec((8, tm), lambda i: (0, i))],
        out_shape=[jax.ShapeDtypeStruct((8, t), jnp.int32), jax.ShapeDtypeStruct((8, t), f32)],
        compiler_params=_cparams(("parallel",)),
        name="route_top2",
    )(logits, bias)


def _dispatch(e_rows, g_rows, n_blocks):
    t = e_rows.shape[1]
    n = 2 * t
    flat_e = e_rows[0:2].reshape(n)
    flat_w = g_rows[0:2].reshape(n)
    order = jnp.argsort(flat_e, stable=True).astype(jnp.int32)
    experts = jnp.arange(N_EXPERTS, dtype=jnp.int32)
    counts = jnp.sum(flat_e[:, None] == experts[None, :], axis=0).astype(jnp.int32)
    starts = jnp.cumsum(counts) - counts
    padded = (counts + MOE_BM - 1) // MOE_BM * MOE_BM
    p_ends = jnp.cumsum(padded)
    p_starts = p_ends - padded
    blk_start = jnp.arange(n_blocks, dtype=jnp.int32) * MOE_BM
    blk_e = jnp.minimum(jnp.sum(p_ends[None, :] <= blk_start[:, None], axis=1), N_EXPERTS - 1).astype(jnp.int32)
    sel = (blk_e[:, None] == experts[None, :]).astype(jnp.int32)
    pick = lambda v: jnp.sum(sel * v[None, :], axis=1)[:, None]
    rank = blk_start[:, None] + jnp.arange(MOE_BM, dtype=jnp.int32)[None, :] - pick(p_starts)
    valid = jnp.logical_and(rank >= 0, rank < pick(counts))
    src = order[jnp.clip(pick(starts) + rank, 0, n - 1)]
    tok = jnp.where(valid, jnp.where(src >= t, src - t, src), 0).astype(jnp.int32).reshape(n_blocks, 1, MOE_BM)
    dst = jnp.where(valid, src, -1).astype(jnp.int32).reshape(n_blocks, 1, MOE_BM)
    wt = jnp.where(valid, flat_w[src], 0.0).reshape(n_blocks, MOE_BM, 1)
    n_steps = n_blocks + 2
    tok_s = jnp.concatenate([tok, jnp.zeros((2, 1, MOE_BM), jnp.int32)], axis=0)
    e_s = jnp.concatenate([blk_e[:1], blk_e, blk_e[-1:]], axis=0)
    wt_s = jnp.concatenate([jnp.zeros((1, MOE_BM, 1), f32), wt, jnp.zeros((1, MOE_BM, 1), f32)], axis=0)
    dst_s = jnp.concatenate([jnp.full((2, 1, MOE_BM), -1, jnp.int32), dst], axis=0)
    is_pad = (dst_s < 0).reshape(-1)
    pad_rank = (jnp.cumsum(is_pad.astype(jnp.int32)) - 1).reshape(n_steps, 1, MOE_BM)
    dst_s = jnp.where(dst_s >= 0, dst_s, 2 * t + pad_rank)
    return e_s, tok_s, dst_s, wt_s


def _moe_kernel(n_steps, e_ref, tok_ref, dst_ref, wt_ref, h_hbm, wg_ref, wu_ref, wd_ref, out_hbm,
                xb0, xb1, yb0, yb1, sems):
    del e_ref
    s = pl.program_id(0)
    xbs, ybs = (xb0, xb1), (yb0, yb1)

    def wait_step_dmas():
        pltpu.make_async_copy(h_hbm.at[pl.ds(0, MOE_BM)], xb0, sems.at[0]).wait()
        pltpu.make_async_copy(yb0, out_hbm.at[pl.ds(0, MOE_BM)], sems.at[1]).wait()

    @pl.when(s == 0)
    def _():
        xb1[...] = jnp.zeros_like(xb1)
        yb1[...] = jnp.zeros_like(yb1)

    @pl.when(s > 0)
    def _():
        wait_step_dmas()

    def step(par):
        x_in, x_cur = xbs[par], xbs[1 - par]
        y_cur, y_out = ybs[par], ybs[1 - par]
        for r in range(MOE_BM):
            pltpu.make_async_copy(h_hbm.at[pl.ds(tok_ref[0, 0, r], 1)], x_in.at[pl.ds(r, 1)], sems.at[0]).start()
        w = x_cur[...]
        x_lo = lax.bitcast_convert_type(jnp.left_shift(w, jnp.uint32(16)), f32).astype(bf16)
        x_hi = lax.bitcast_convert_type(jnp.bitwise_and(w, jnp.uint32(0xFFFF0000)), f32).astype(bf16)
        dh = D // 2
        g = (jnp.dot(x_lo, wg_ref[0, :dh, :], preferred_element_type=f32)
             + jnp.dot(x_hi, wg_ref[0, dh:, :], preferred_element_type=f32))
        u = (jnp.dot(x_lo, wu_ref[0, :dh, :], preferred_element_type=f32)
             + jnp.dot(x_hi, wu_ref[0, dh:, :], preferred_element_type=f32))
        a = (g * _sigmoid(g) * u).astype(bf16)
        y_cur[...] = jnp.dot(a, wd_ref[0], preferred_element_type=f32) * wt_ref[0]
        for r in range(MOE_BM):
            pltpu.make_async_copy(y_out.at[pl.ds(r, 1)], out_hbm.at[pl.ds(dst_ref[0, 0, r], 1)], sems.at[1]).start()

    @pl.when(lax.rem(s, 2) == 0)
    def _():
        step(0)

    @pl.when(lax.rem(s, 2) == 1)
    def _():
        step(1)

    @pl.when(s == n_steps - 1)
    def _():
        wait_step_dmas()


def _moe(h2, e_s, tok_s, dst_s, wt_s, wg, wu, wd):
    n_steps = tok_s.shape[0]
    grid_spec = pltpu.PrefetchScalarGridSpec(
        num_scalar_prefetch=1,
        grid=(n_steps,),
        in_specs=[pl.BlockSpec((1, 1, MOE_BM), lambda s, e: (s, 0, 0), memory_space=pltpu.SMEM),
                  pl.BlockSpec((1, 1, MOE_BM), lambda s, e: (s, 0, 0), memory_space=pltpu.SMEM),
                  pl.BlockSpec((1, MOE_BM, 1), lambda s, e: (s, 0, 0)),
                  pl.BlockSpec(memory_space=pl.ANY),
                  pl.BlockSpec((1, D, D_EXPERT), lambda s, e: (e[s], 0, 0)),
                  pl.BlockSpec((1, D, D_EXPERT), lambda s, e: (e[s], 0, 0)),
                  pl.BlockSpec((1, D_EXPERT, D), lambda s, e: (e[s], 0, 0))],
        out_specs=pl.BlockSpec(memory_space=pl.ANY),
        scratch_shapes=[pltpu.VMEM((MOE_BM, D // 2), jnp.uint32)] * 2 + [pltpu.VMEM((MOE_BM, D), f32)] * 2
        + [pltpu.SemaphoreType.DMA((2,))],
    )
    return pl.pallas_call(
        functools.partial(_moe_kernel, n_steps),
        grid_spec=grid_spec,
        out_shape=jax.ShapeDtypeStruct((n_steps * MOE_BM, D), f32),
        compiler_params=_cparams(("arbitrary",), 52),
        name="moe_experts",
    )(e_s, tok_s, dst_s, wt_s, h2, wg, wu, wd)


def _final_kernel(x_ref, y0_ref, y1_ref, mod_ref, g_ref, o_ref):
    x = x_ref[...] + mod_ref[0:1, 5 * D:6 * D] * (y0_ref[...] + y1_ref[...])
    o_ref[...] = x * lax.rsqrt(jnp.mean(x * x, axis=-1, keepdims=True) + EPS) * g_ref[...]


def _final(x, moe, mod, g):
    t = x.shape[0]
    tm = 256
    nrow = t // tm
    nctx = BLK // tm
    return pl.pallas_call(
        _final_kernel,
        grid=(nrow - nctx,),
        in_specs=[pl.BlockSpec((tm, D), lambda i: (i + nctx, 0)),
                  pl.BlockSpec((tm, D), lambda i: (i + nctx, 0)),
                  pl.BlockSpec((tm, D), lambda i: (i + nctx + nrow, 0)),
                  pl.BlockSpec((8, 6 * D), lambda i: (0, 0)),
                  pl.BlockSpec((1, D), lambda i: (0, 0))],
        out_specs=pl.BlockSpec((tm, D), lambda i: (i, 0)),
        out_shape=jax.ShapeDtypeStruct((t - BLK, D), f32),
        compiler_params=_cparams(("parallel",), 40),
        name="final_norm",
    )(x, moe, moe, mod, g.reshape(1, D))


def _cast_kernel(x_ref, o_ref):
    o_ref[...] = x_ref[...].astype(o_ref.dtype)


def _to_bf16(w, layer):
    shape = w.shape[1:]
    w2 = w.reshape(-1, shape[-1])
    cols = shape[-1]
    rows = w2.shape[0] // w.shape[0]
    tr = _row_tile(rows, (1024, 512, 256))
    off = layer * (rows // tr)
    out = pl.pallas_call(
        _cast_kernel,
        grid=(rows // tr,),
        in_specs=[pl.BlockSpec((tr, cols), lambda i: (i + off, 0))],
        out_specs=pl.BlockSpec((tr, cols), lambda i: (i, 0)),
        out_shape=jax.ShapeDtypeStruct((rows, cols), bf16),
        compiler_params=_cparams(("parallel",), 48),
        name="cast_bf16",
    )(w2)
    return out.reshape(shape)


def _prep_in_weights(w_in):
    splits = (1536, 512, 8, 8, 1024, 256, 256, 512, 512, 512, 512, 8, 8)
    offs = [0]
    for s in splits:
        offs.append(offs[-1] + s)
    part = lambda i: w_in[:, offs[i]:offs[i + 1]]
    main = jnp.concatenate([part(i) for i in (0, 1, 4, 5, 6, 7, 8, 9, 10)], axis=1).astype(bf16)
    gates = jnp.concatenate([part(i) for i in (2, 3, 11, 12)], axis=1)
    gates = jnp.pad(gates, ((0, 0), (0, 128 - gates.shape[1])))
    g1 = gates.astype(bf16)
    g2 = (gates - g1.astype(f32)).astype(bf16)
    return main, g1, g2


def _layer(l, rope, x, moe_prev, mods, norm1_g, norm2_g, w_in, dn_conv, dn_a_log, dn_dt_bias, dn_norm_g,
           q_norm_g, k_norm_g, ml_i_bias, ml_f_bias, ml_norm_g, w_out, rw1, rw2, router_bias,
           w_gate, w_up, w_down):
    t = x.shape[0]
    w_main, wg1, wg2 = _prep_in_weights(w_in[l])
    x, h, graw = _norm1(x, moe_prev, mods[l - 1] if l else None, mods[l], norm1_g[l], wg1, wg2)
    p = _matmul(h, w_main, bf16)
    conv_w = jnp.pad(dn_conv[l], ((0, 8 - CONV_W), (0, 0)))
    gate_params = jnp.zeros((8, 128), f32)
    gate_params = gate_params.at[0, 8:16].set(dn_dt_bias[l].reshape(8))
    gate_params = gate_params.at[0, 16:24].set(ml_i_bias[l].reshape(8))
    gate_params = gate_params.at[0, 24:32].set(ml_f_bias[l].reshape(8))
    gate_params = gate_params.at[1, 8:16].set(dn_a_log[l].reshape(8))
    dnq, gcol, grow = _scan_prep(p, graw, conv_w, gate_params)
    dnf, dnb = _gdn_scan(dnq, gcol, grow)
    mlf, mlb = _mlstm_scan(p, gcol, grow)
    qr, kr, va = _attn_prep(p, q_norm_g[l], k_norm_g[l], rope)
    at = _attention(qr, kr, va)
    x, h2, logits = _merge_outproj(dnf, dnb, p, at, mlf, mlb, x, mods[l], dn_norm_g[l], ml_norm_g[l],
                                   w_out[l].astype(bf16), norm2_g[l], rw1, rw2)
    e_rows, g_rows = _route(logits, router_bias)
    n_blocks = (2 * t + N_EXPERTS * (MOE_BM - 1) + MOE_BM - 1) // MOE_BM
    blk_e, tok, dst, wt = _dispatch(e_rows, g_rows, n_blocks)
    moe = _moe(h2, blk_e, tok, dst, wt, _to_bf16(w_gate, l), _to_bf16(w_up, l), _to_bf16(w_down, l))
    return x, moe


def kernel(x, c, ctx, c_ctx, w_mod, b_mod, norm1_g, norm2_g, w_in, dn_conv, dn_a_log, dn_dt_bias, dn_norm_g, q_norm_g, k_norm_g, ml_i_bias, ml_f_bias, ml_norm_g, w_out, router_w, router_bias, w_gate, w_up, w_down, final_norm_g):
    b, seq, d = x.shape
    assert b == 1 and d == D and ctx.shape[1] == BLK and seq % BLK == 0 and seq % GRID_W == 0
    depth = w_mod.shape[0]
    mods = _mods(c, c_ctx, w_mod, b_mod)
    xs = jnp.concatenate([ctx[0], x[0]], axis=0)
    rw = jnp.pad(router_w, ((0, 0), (0, 128 - N_EXPERTS)))
    rw1 = rw.astype(bf16)
    rw2 = (rw - rw1.astype(f32)).astype(bf16)
    moe = None
    rope = _rope_tables(xs.shape[0])
    for l in range(depth):
        xs, moe = _layer(l, rope, xs, moe, mods, norm1_g, norm2_g, w_in, dn_conv, dn_a_log, dn_dt_bias,
                         dn_norm_g, q_norm_g, k_norm_g, ml_i_bias, ml_f_bias, ml_norm_g, w_out, rw1, rw2,
                         router_bias, w_gate, w_up, w_down)
    out = _final(xs, moe, mods[depth - 1], final_norm_g)
    return out.reshape(b, seq, d)
```

```python
import functools

import jax
import jax.numpy as jnp
from jax import lax
from jax.experimental import pallas as pl
from jax.experimental.pallas import tpu as pltpu

f32 = jnp.float32
bf16 = jnp.bfloat16

D = 2048
HD = 128
N_HEADS_SCAN = 4
H_AT = 8
H_KV = 2
CHUNK = 64
BLK = 256
CPB = BLK // CHUNK
GRID_W = 64
ROPE_THETA = 10000.0
QK_SCALE = HD ** -0.5
LOG2E = 1.4426950408889634
N_EXPERTS = 16
EXPERTS_PER_GROUP = 4
D_EXPERT = D // 2
MOE_BM = 256
EPS = 1e-6
NEG = -1e30
CONV_W = 5

C_DNQKV, C_DNZ, C_ATQ, C_ATK, C_ATV, C_MLQ, C_MLK, C_MLV, C_MLO, P_COLS = (
    0, 1536, 2048, 3072, 3328, 3584, 4096, 4608, 5120, 5632)

VMEM_STREAM_MB = 40
VMEM_MATMUL_MB = 48
VMEM_RESIDENT_MB = 52


def _cparams(sems, vmem_mb=None):
    return pltpu.CompilerParams(
        dimension_semantics=sems,
        vmem_limit_bytes=None if vmem_mb is None else vmem_mb << 20)


def _mm(a, b):
    return jnp.dot(a.astype(bf16), b.astype(bf16), preferred_element_type=f32)


def _mm_nt(a, b):
    return lax.dot_general(a.astype(bf16), b.astype(bf16), (((1,), (1,)), ((), ())),
                           preferred_element_type=f32)


def _mm_tn(a, b):
    return lax.dot_general(a.astype(bf16), b.astype(bf16), (((0,), (0,)), ((), ())),
                           preferred_element_type=f32)


def _split3(x):
    x1 = x.astype(bf16)
    r1 = x - x1.astype(f32)
    x2 = r1.astype(bf16)
    x3 = (r1 - x2.astype(f32)).astype(bf16)
    return x1, x2, x3


def _sigmoid(x):
    return 1.0 / (1.0 + jnp.exp(-x))


def _softplus(x):
    return jnp.maximum(x, 0.0) + jnp.log(1.0 + jnp.exp(-jnp.abs(x)))


def _mod_kernel(s_ref, w_ref, b_ref, o_ref):
    s = s_ref[...]
    s = s * _sigmoid(s)
    o_ref[0] = jnp.dot(s, w_ref[0], preferred_element_type=f32,
                       precision=lax.Precision.HIGHEST) + b_ref[0]


def _mods(c, c_ctx, w_mod, b_mod):
    depth, d, n6 = w_mod.shape
    s = jnp.zeros((8, d), f32).at[0].set(c[0]).at[1].set(c_ctx)
    tn = 1024
    return pl.pallas_call(
        _mod_kernel,
        grid=(depth, n6 // tn),
        in_specs=[pl.BlockSpec((8, d), lambda l, j: (0, 0)),
                  pl.BlockSpec((1, d, tn), lambda l, j: (l, 0, j)),
                  pl.BlockSpec((1, 1, tn), lambda l, j: (l, 0, j))],
        out_specs=pl.BlockSpec((1, 8, tn), lambda l, j: (l, 0, j)),
        out_shape=jax.ShapeDtypeStruct((depth, 8, n6), f32),
        compiler_params=_cparams(("parallel", "parallel"), VMEM_STREAM_MB),
        name="mod_vectors",
    )(s, w_mod, b_mod.reshape(depth, 1, n6))


def _mod_rows(mod_ref, k, is_ctx):
    lat = mod_ref[0:1, k * D:(k + 1) * D]
    ctx = mod_ref[1:2, k * D:(k + 1) * D]
    return jnp.where(is_ctx, ctx, lat)


def _norm1_kernel(has_moe, tm, *refs):
    if has_moe:
        x_ref, y0_ref, y1_ref, modp_ref, mod_ref, g_ref, wg1_ref, wg2_ref, xo_ref, h_ref, gr_ref = refs
    else:
        x_ref, mod_ref, g_ref, wg1_ref, wg2_ref, h_ref, gr_ref = refs
    i = pl.program_id(0)
    rows = i * tm + lax.broadcasted_iota(jnp.int32, (tm, 1), 0)
    is_ctx = rows < BLK
    x = x_ref[...]
    if has_moe:
        x = x + _mod_rows(modp_ref, 5, is_ctx) * (y0_ref[...] + y1_ref[...])
        xo_ref[...] = x
    ms = jnp.mean(x * x, axis=-1, keepdims=True)
    y = x * lax.rsqrt(ms + EPS) * g_ref[...]
    h = y * (1.0 + _mod_rows(mod_ref, 1, is_ctx)) + _mod_rows(mod_ref, 0, is_ctx)
    hh = h.astype(bf16)
    h_ref[...] = hh
    hl = (h - hh.astype(f32)).astype(bf16)
    wg1 = wg1_ref[...]
    gr_ref[...] = (jnp.dot(hh, wg1, preferred_element_type=f32)
                   + jnp.dot(hl, wg1, preferred_element_type=f32)
                   + jnp.dot(hh, wg2_ref[...], preferred_element_type=f32))


def _norm1(x, moe, mod_prev, mod_cur, g, wg1, wg2):
    t = x.shape[0]
    tm = 256
    nrow = t // tm
    row = lambda i: (i, 0)
    full = lambda i: (0, 0)
    in_specs = [pl.BlockSpec((tm, D), row)]
    args = [x]
    if moe is not None:
        in_specs += [pl.BlockSpec((tm, D), row), pl.BlockSpec((tm, D), lambda i: (i + nrow, 0)),
                     pl.BlockSpec((8, 6 * D), full)]
        args += [moe, moe, mod_prev]
    in_specs += [pl.BlockSpec((8, 6 * D), full), pl.BlockSpec((1, D), full),
                 pl.BlockSpec((D, 128), full), pl.BlockSpec((D, 128), full)]
    args += [mod_cur, g.reshape(1, D), wg1, wg2]
    out_specs = [pl.BlockSpec((tm, D), row), pl.BlockSpec((tm, 128), row)]
    out_shape = [jax.ShapeDtypeStruct((t, D), bf16), jax.ShapeDtypeStruct((t, 128), f32)]
    if moe is not None:
        out_specs = [pl.BlockSpec((tm, D), row)] + out_specs
        out_shape = [jax.ShapeDtypeStruct((t, D), f32)] + out_shape
    outs = pl.pallas_call(
        functools.partial(_norm1_kernel, moe is not None, tm),
        grid=(nrow,), in_specs=in_specs, out_specs=out_specs, out_shape=out_shape,
        compiler_params=_cparams(("parallel",), VMEM_STREAM_MB),
        name="norm1_modulate",
    )(*args)
    if moe is not None:
        return outs
    return [x] + list(outs)


def _matmul_kernel(a_ref, b_ref, o_ref):
    o_ref[...] = jnp.dot(a_ref[...], b_ref[...], preferred_element_type=f32).astype(o_ref.dtype)


def _row_tile(t, choices):
    for c in choices:
        if t % c == 0:
            return c
    raise ValueError(f"no row tile for {t}")


def _matmul(a, b, out_dtype):
    m, k = a.shape
    n = b.shape[1]
    tm = _row_tile(m, (1280, 768, 512, 256))
    tn = _row_tile(n, (1408, 1024, 512))
    return pl.pallas_call(
        _matmul_kernel,
        grid=(n // tn, m // tm),
        in_specs=[pl.BlockSpec((tm, k), lambda j, i: (i, 0)),
                  pl.BlockSpec((k, tn), lambda j, i: (0, j))],
        out_specs=pl.BlockSpec((tm, tn), lambda j, i: (i, j)),
        out_shape=jax.ShapeDtypeStruct((m, n), out_dtype),
        compiler_params=_cparams(("parallel", "parallel"), VMEM_MATMUL_MB),
        name="in_projection",
    )(a, b)


def _scan_prep_kernel(nblk, cur_ref, prev_ref, next_ref, cw_ref, graw_ref, gp_ref, q_ref, gc_ref, grow_ref, xs):
    i = pl.program_id(0)
    has_prev = i >= 2
    has_next = jnp.logical_and(i >= 1, i < nblk - 1)
    xs[0:16, :] = jnp.where(has_prev, prev_ref[...].astype(f32), 0.0)
    xs[16:16 + BLK, :] = cur_ref[...].astype(f32)
    xs[16 + BLK:32 + BLK, :] = jnp.where(has_next, next_ref[...].astype(f32), 0.0)
    acc = cw_ref[0:1, :] * xs[pl.ds(16 - CONV_W // 2, BLK), :]
    for j in range(1, CONV_W):
        acc = acc + cw_ref[j:j + 1, :] * xs[pl.ds(16 - CONV_W // 2 + j, BLK), :]
    a = acc * _sigmoid(acc)
    w = N_HEADS_SCAN * HD
    for h in range(2 * N_HEADS_SCAN):
        xh = a[:, h * HD:(h + 1) * HD]
        inv = lax.rsqrt(jnp.sum(xh * xh, axis=-1, keepdims=True) + EPS)
        scale = QK_SCALE if h < N_HEADS_SCAN else 1.0
        q_ref[:, h * HD:(h + 1) * HD] = (xh * (inv * scale)).astype(bf16)
    q_ref[:, 2 * w:3 * w] = a[:, 2 * w:3 * w].astype(bf16)

    z = graw_ref[...] + gp_ref[0:1, :]
    lane = lax.broadcasted_iota(jnp.int32, (BLK, 128), 1)
    sp = _softplus(z)
    vals = jnp.where(lane < 8, _sigmoid(z),
                     jnp.where(lane < 16, -jnp.exp(gp_ref[1:2, :]) * sp,
                               jnp.where(lane < 24, z, z - sp)))
    r = lax.broadcasted_iota(jnp.int32, (BLK, BLK), 0)
    c = lax.broadcasted_iota(jnp.int32, (BLK, BLK), 1)
    same = jnp.right_shift(r, 6) == jnp.right_shift(c, 6)
    tri_lo = jnp.where(jnp.logical_and(same, r >= c), 1.0, 0.0).astype(bf16)
    tri_up = jnp.where(jnp.logical_and(same, r <= c), 1.0, 0.0).astype(bf16)
    v1, v2, v3 = _split3(vals)
    dot = functools.partial(jnp.dot, preferred_element_type=f32)
    prefix = dot(tri_lo, v1) + dot(tri_lo, v2) + dot(tri_lo, v3)
    suffix = dot(tri_up, v1) + dot(tri_up, v2) + dot(tri_up, v3)
    is_cum = jnp.logical_and(jnp.bitwise_and(lane, 8) == 8, lane < 32)
    is_bwd = jnp.bitwise_and(lane, 4) == 4
    out = jnp.where(is_cum, jnp.where(is_bwd, suffix, prefix), vals)
    gc_ref[...] = out
    gt = out.T
    for cc in range(CPB):
        grow_ref[cc] = gt[0:32, cc * CHUNK:(cc + 1) * CHUNK]


def _scan_prep(p, graw, conv_w, gate_params):
    t = p.shape[0]
    nblk = t // BLK
    wq = 3 * N_HEADS_SCAN * HD
    n16 = t // 16
    return pl.pallas_call(
        functools.partial(_scan_prep_kernel, nblk),
        grid=(nblk,),
        in_specs=[pl.BlockSpec((BLK, wq), lambda i: (i, 0)),
                  pl.BlockSpec((16, wq), lambda i: (jnp.maximum(i * (BLK // 16) - 1, 0), 0)),
                  pl.BlockSpec((16, wq), lambda i: (jnp.minimum((i + 1) * (BLK // 16), n16 - 1), 0)),
                  pl.BlockSpec((8, wq), lambda i: (0, 0)),
                  pl.BlockSpec((BLK, 128), lambda i: (i, 0)),
                  pl.BlockSpec((8, 128), lambda i: (0, 0))],
        out_specs=[pl.BlockSpec((BLK, wq), lambda i: (i, 0)),
                   pl.BlockSpec((BLK, 128), lambda i: (i, 0)),
                   pl.BlockSpec((CPB, 32, CHUNK), lambda i: (i, 0, 0))],
        out_shape=[jax.ShapeDtypeStruct((t, wq), bf16),
                   jax.ShapeDtypeStruct((t, 128), f32),
                   jax.ShapeDtypeStruct((t // CHUNK, 32, CHUNK), f32)],
        scratch_shapes=[pltpu.VMEM((BLK + 32, wq), f32)],
        compiler_params=_cparams(("parallel",), VMEM_STREAM_MB),
        name="scan_prep",
    )(p, p, p, conv_w, graw, gate_params)


def _tri_masks():
    r = lax.broadcasted_iota(jnp.int32, (CHUNK, CHUNK), 0)
    c = lax.broadcasted_iota(jnp.int32, (CHUNK, CHUNK), 1)
    blk = jnp.right_shift(r, 4) == jnp.right_shift(c, 4)
    eye = jnp.where(r == c, 1.0, 0.0)
    return (r >= c, r <= c), (r > c, r < c), blk, eye


def _gdn_kernel(qf_ref, qb_ref, gcf_ref, gcb_ref, grf_ref, grb_ref, of_ref, ob_ref, s_scr):
    @pl.when(pl.program_id(0) == 0)
    def _():
        s_scr[...] = jnp.zeros_like(s_scr)

    incl, strict, blk, eye = _tri_masks()
    w = N_HEADS_SCAN * HD

    units = [(d, h) for d in range(2) for h in range(N_HEADS_SCAN)]
    rows_cat = lambda a, b: jnp.concatenate([a, b], axis=0)
    cols_cat = lambda a, b: jnp.concatenate([a, b], axis=1)
    C = CHUNK

    def chunk_pair(pp, carry):
        ld, tags = [], []
        for off in range(2):
            for d, h in units:
                cc = 2 * pp + off
                c = cc if d == 0 else CPB - 1 - cc
                q_ref, gc_ref, gr_ref = (qf_ref, gcf_ref, grf_ref) if d == 0 else (qb_ref, gcb_ref, grb_ref)
                rows = pl.ds(pl.multiple_of(c * C, C), C)
                u_idx = d * N_HEADS_SCAN + h
                q = q_ref[rows, h * HD:(h + 1) * HD]
                k = q_ref[rows, w + h * HD:w + (h + 1) * HD]
                v = q_ref[rows, 2 * w + h * HD:2 * w + (h + 1) * HD]
                beta = gc_ref[rows, u_idx:u_idx + 1]
                cum_c = gc_ref[rows, 8 + u_idx:9 + u_idx]
                cum_r = gr_ref[c][8 + u_idx:9 + u_idx, :]
                tot = cum_c[C - 1:C, :] if d == 0 else cum_c[0:1, :]
                ld.append((rows, q, k, v, beta, cum_c, cum_r, tot))
                tags.append((d, h))
        g1 = [_mm_nt(rows_cat(k, q), k) for (_, q, k, *_) in ld]
        st = []
        for (d, h), (rows, q, k, v, beta, cum_c, cum_r, tot), g in zip(tags, ld, g1):
            decay = jnp.exp(jnp.where(incl[d], cum_c - cum_r, NEG))
            nm = jnp.where(strict[d], beta * g[:C] * decay, 0.0)
            dm = jnp.where(blk, nm, 0.0)
            kf = k.astype(f32)
            e_c = jnp.exp(cum_c)
            rhs = cols_cat(cols_cat((beta * e_c) * kf, beta * v.astype(f32)), nm - dm)
            st.append(dict(dm=dm, rhs=rhs, qk=g[C:] * decay, k_dec=kf * jnp.exp(tot - cum_c),
                           q_dec=q.astype(f32) * e_c, g_last=jnp.exp(tot), p1=eye - dm))
        m2 = [_mm(s["dm"], s["dm"]) for s in st]
        r = [_mm(rows_cat(s["p1"], m), m) for s, m in zip(st, m2)]
        p2 = [s["p1"] + x[:C] for s, x in zip(st, r)]
        m4 = [x[C:] for x in r]
        r = [_mm(rows_cat(p, m), m) for p, m in zip(p2, m4)]
        p3 = [p + x[:C] for p, x in zip(p2, r)]
        m8 = [x[C:] for x in r]
        dinv = [p + _mm(p, m) for p, m in zip(p3, m8)]
        r = [_mm(di, s["rhs"]) for di, s in zip(dinv, st)]
        t1 = [x[:, :2 * HD] for x in r]
        qm = [x[:, 2 * HD:] for x in r]
        r = [_mm(qq, cols_cat(t, qq)) for qq, t in zip(qm, t1)]
        a1 = [x[:, :2 * HD] for x in r]
        qm2 = [x[:, 2 * HD:] for x in r]
        b2 = [_mm(q2, t) for q2, t in zip(qm2, t1)]
        c3 = [_mm(qq, b) for qq, b in zip(qm, b2)]
        sol = [t - a + b - c for t, a, b, c in zip(t1, a1, b2, c3)]
        nu = len(units)
        for off in range(2):
            sl = slice(off * nu, (off + 1) * nu)
            s_old = [s_scr[i] for i in range(nu)]
            r = [_mm(rows_cat(x[:, :HD], s["q_dec"]), so) for x, s, so in zip(sol[sl], st[sl], s_old)]
            u = [x[:, HD:] - y[:C] for x, y in zip(sol[sl], r)]
            o_intra = [_mm(s["qk"], uu) for s, uu in zip(st[sl], u)]
            s_add = [_mm_tn(s["k_dec"], uu) for s, uu in zip(st[sl], u)]
            for i, ((d, h), l, s) in enumerate(zip(units, ld[sl], st[sl])):
                o_ref = of_ref if d == 0 else ob_ref
                o_ref[l[0], h * HD:(h + 1) * HD] = r[i][C:] + o_intra[i]
                s_scr[i] = s["g_last"] * s_old[i] + s_add[i]
        return carry

    lax.fori_loop(0, CPB // 2, chunk_pair, 0)


def _bwd_block(nblk):
    return lambda s: (jnp.where(s == 0, 0, nblk - s), 0)


def _gdn_scan(qkv, gcol, grow):
    t = qkv.shape[0]
    nblk = t // BLK
    wq = 3 * N_HEADS_SCAN * HD
    w = N_HEADS_SCAN * HD
    fwd = lambda s: (s, 0)
    bwd = _bwd_block(nblk)
    fwd3 = lambda s: (s, 0, 0)
    bwd3 = lambda s: (jnp.where(s == 0, 0, nblk - s), 0, 0)
    return pl.pallas_call(
        _gdn_kernel,
        grid=(nblk,),
        in_specs=[pl.BlockSpec((BLK, wq), fwd), pl.BlockSpec((BLK, wq), bwd),
                  pl.BlockSpec((BLK, 128), fwd), pl.BlockSpec((BLK, 128), bwd),
                  pl.BlockSpec((CPB, 32, CHUNK), fwd3), pl.BlockSpec((CPB, 32, CHUNK), bwd3)],
        out_specs=[pl.BlockSpec((BLK, w), fwd), pl.BlockSpec((BLK, w), bwd)],
        out_shape=[jax.ShapeDtypeStruct((t, w), f32), jax.ShapeDtypeStruct((t, w), f32)],
        scratch_shapes=[pltpu.VMEM((2 * N_HEADS_SCAN, HD, HD), f32)],
        compiler_params=_cparams(("arbitrary",), VMEM_STREAM_MB),
        name="gdn_scan",
    )(qkv, qkv, gcol, gcol, grow, grow)


def _mlstm_kernel(pf_q, pf_k, pf_v, pb_q, pb_k, pb_v, gcf_ref, gcb_ref, grf_ref, grb_ref,
                  of_ref, ob_ref, c_scr, m_scr):
    @pl.when(pl.program_id(0) == 0)
    def _():
        c_scr[...] = jnp.zeros_like(c_scr)
        m_scr[...] = jnp.full_like(m_scr, NEG)

    incl, _, _, _ = _tri_masks()
    ones_col = jnp.where(lax.broadcasted_iota(jnp.int32, (CHUNK, HD), 1) == 0, 1.0, 0.0).astype(bf16)

    units = [(d, h) for d in range(2) for h in range(N_HEADS_SCAN)]

    def chunk(cc, carry):
        ld = []
        for d, h in units:
            c = cc if d == 0 else CPB - 1 - cc
            q_ref, k_ref, v_ref, gc_ref, gr_ref = (
                (pf_q, pf_k, pf_v, gcf_ref, grf_ref) if d == 0 else (pb_q, pb_k, pb_v, gcb_ref, grb_ref))
            rows = pl.ds(pl.multiple_of(c * CHUNK, CHUNK), CHUNK)
            u_idx = d * N_HEADS_SCAN + h
            q = q_ref[rows, h * HD:(h + 1) * HD]
            k = k_ref[rows, h * HD:(h + 1) * HD]
            v = v_ref[rows, h * HD:(h + 1) * HD]
            i_c = gc_ref[rows, 16 + u_idx:17 + u_idx]
            b_c = gc_ref[rows, 24 + u_idx:25 + u_idx]
            grow = gr_ref[c]
            i_r = grow[16 + u_idx:17 + u_idx, :]
            b_r = grow[24 + u_idx:25 + u_idx, :]
            b_last = b_c[CHUNK - 1:CHUNK, :] if d == 0 else b_c[0:1, :]
            w_log = b_last - b_c + i_c
            m_st = jnp.max(w_log, axis=0, keepdims=True)
            e_w = jnp.exp(w_log - m_st)
            ld.append(dict(rows=rows, q=q, k=k, v=v, i_r=i_r, b_c=b_c, b_r=b_r, b_last=b_last, m_st=m_st, e_w=e_w))
        nu = len(units)
        v_aug = [jnp.concatenate([l["v"], ones_col], axis=1) for l in ld]
        qk = [_mm_nt(l["q"], l["k"]) for l in ld]
        kv = [_mm_tn(l["k"], l["e_w"] * va.astype(f32)) for l, va in zip(ld, v_aug)]
        c_old = [c_scr[i] for i in range(nu)]
        qc = [_mm(l["q"], cm) for l, cm in zip(ld, c_old)]
        ps, m_locs = [], []
        for (d, h), l, g in zip(units, ld, qk):
            d_log = jnp.where(incl[d], l["b_c"] - l["b_r"] + l["i_r"], NEG)
            m_loc = jnp.max(d_log, axis=-1, keepdims=True)
            ps.append(jnp.exp(d_log - m_loc) * (g * QK_SCALE))
            m_locs.append(m_loc)
        loc = [_mm(p, va) for p, va in zip(ps, v_aug)]
        m_old = [m_scr[i][0:1, 0:1] for i in range(nu)]
        inter = [l["b_c"] + m for l, m in zip(ld, m_old)]
        m_r = [jnp.maximum(a, b) for a, b in zip(inter, m_locs)]
        a_in = [jnp.exp(a - b) for a, b in zip(inter, m_r)]
        a_lo = [jnp.exp(a - b) for a, b in zip(m_locs, m_r)]
        floor = [jnp.exp(-b) for b in m_r]
        m_new = [jnp.maximum(l["b_last"] + m, l["m_st"]) for l, m in zip(ld, m_old)]
        s_old = [jnp.exp(l["b_last"] + m - mn) for l, m, mn in zip(ld, m_old, m_new)]
        s_new = [jnp.exp(l["m_st"] - mn) * QK_SCALE for l, mn in zip(ld, m_new)]
        for i, ((d, h), l) in enumerate(zip(units, ld)):
            o_ref = of_ref if d == 0 else ob_ref
            num = a_in[i] * qc[i][:, :HD] + a_lo[i] * loc[i][:, :HD]
            den = a_in[i] * qc[i][:, HD:HD + 1] + a_lo[i] * loc[i][:, HD:HD + 1]
            o_ref[l["rows"], h * HD:(h + 1) * HD] = num / jnp.maximum(jnp.abs(den), floor[i])
        for i in range(nu):
            c_scr[i] = s_old[i] * c_old[i] + s_new[i] * kv[i]
            m_scr[i] = jnp.broadcast_to(m_new[i], (8, HD))
        return carry

    lax.fori_loop(0, CPB, chunk, 0)


def _mlstm_scan(p, gcol, grow):
    t = p.shape[0]
    nblk = t // BLK
    w = N_HEADS_SCAN * HD
    nu = 2 * N_HEADS_SCAN
    fwd = lambda s: (s, 0)
    bwd = _bwd_block(nblk)
    fwd3 = lambda s: (s, 0, 0)
    bwd3 = lambda s: (jnp.where(s == 0, 0, nblk - s), 0, 0)

    def col(base, bwd_dir):
        cb = base // w
        if bwd_dir:
            return pl.BlockSpec((BLK, w), lambda s: (jnp.where(s == 0, 0, nblk - s), cb))
        return pl.BlockSpec((BLK, w), lambda s: (s, cb))

    return pl.pallas_call(
        _mlstm_kernel,
        grid=(nblk,),
        in_specs=[col(C_MLQ, False), col(C_MLK, False), col(C_MLV, False),
                  col(C_MLQ, True), col(C_MLK, True), col(C_MLV, True),
                  pl.BlockSpec((BLK, 128), fwd), pl.BlockSpec((BLK, 128), bwd),
                  pl.BlockSpec((CPB, 32, CHUNK), fwd3), pl.BlockSpec((CPB, 32, CHUNK), bwd3)],
        out_specs=[pl.BlockSpec((BLK, w), fwd), pl.BlockSpec((BLK, w), bwd)],
        out_shape=[jax.ShapeDtypeStruct((t, w), f32), jax.ShapeDtypeStruct((t, w), f32)],
        scratch_shapes=[pltpu.VMEM((nu, HD, 2 * HD), f32), pltpu.VMEM((nu, 8, HD), f32)],
        compiler_params=_cparams(("arbitrary",), VMEM_STREAM_MB),
        name="mlstm_scan",
    )(p, p, p, p, p, p, gcol, gcol, grow, grow)


def _rope_kernel(cos_ref, sin_ref):
    i = pl.program_id(0)
    r = lax.broadcasted_iota(jnp.int32, (BLK, HD), 0)
    lane = lax.broadcasted_iota(jnp.int32, (BLK, HD), 1)
    tok = (i - 1) * BLK + r
    pos = jnp.where(lane < HD // 2, jnp.right_shift(tok, 6), jnp.bitwise_and(tok, GRID_W - 1)).astype(f32)
    pair = jnp.bitwise_and(lane, HD // 4 - 1).astype(f32)
    inv_freq = jnp.exp(pair * (-jnp.log(ROPE_THETA) / (HD // 4)))
    ang = pos * inv_freq
    is_ctx = i == 0
    sin = jnp.where(is_ctx, 0.0, jnp.sin(ang))
    first = jnp.bitwise_and(lane, HD // 4) == 0
    cos_ref[...] = jnp.where(is_ctx, 1.0, jnp.cos(ang))
    sin_ref[...] = jnp.where(first, -sin, sin)


def _rope_tables(t):
    spec = pl.BlockSpec((BLK, HD), lambda i: (i, 0))
    return pl.pallas_call(
        _rope_kernel, grid=(t // BLK,), in_specs=[], out_specs=[spec, spec],
        out_shape=[jax.ShapeDtypeStruct((t, HD), f32)] * 2,
        compiler_params=_cparams(("parallel",)), name="rope_tables",
    )()


def _attn_prep_kernel(q_ref, k_ref, v_ref, qg_ref, kg_ref, cos_ref, sin_ref, qo_ref, ko_ref, vo_ref):
    lane = lax.broadcasted_iota(jnp.int32, (BLK, HD), 1)
    first = jnp.bitwise_and(lane, HD // 4) == 0
    cos = cos_ref[...]
    sin_signed = sin_ref[...]

    def norm_rope(x, g, scale):
        y = x * lax.rsqrt(jnp.mean(x * x, axis=-1, keepdims=True) + EPS) * g
        partner = jnp.where(first, pltpu.roll(y, HD - HD // 4, 1), pltpu.roll(y, HD // 4, 1))
        return (y * cos + partner * sin_signed) * scale

    for h in range(H_AT):
        x = q_ref[:, h * HD:(h + 1) * HD].astype(f32)
        qo_ref[:, h * HD:(h + 1) * HD] = norm_rope(x, qg_ref[...], QK_SCALE * LOG2E).astype(bf16)
    ones_col = jnp.where(lax.broadcasted_iota(jnp.int32, (BLK, HD), 1) == 0, 1.0, 0.0).astype(bf16)
    for h in range(H_KV):
        x = k_ref[:, h * HD:(h + 1) * HD].astype(f32)
        ko_ref[:, h * HD:(h + 1) * HD] = norm_rope(x, kg_ref[...], 1.0).astype(bf16)
        vo_ref[:, 2 * h * HD:(2 * h + 1) * HD] = v_ref[:, h * HD:(h + 1) * HD]
        vo_ref[:, (2 * h + 1) * HD:(2 * h + 2) * HD] = ones_col


def _attn_prep(p, q_g, k_g, rope):
    t = p.shape[0]
    wq, wk = H_AT * HD, H_KV * HD
    return pl.pallas_call(
        _attn_prep_kernel,
        grid=(t // BLK,),
        in_specs=[pl.BlockSpec((BLK, wq), lambda i: (i, C_ATQ // wq)),
                  pl.BlockSpec((BLK, wk), lambda i: (i, C_ATK // wk)),
                  pl.BlockSpec((BLK, wk), lambda i: (i, C_ATV // wk)),
                  pl.BlockSpec((1, HD), lambda i: (0, 0)),
                  pl.BlockSpec((1, HD), lambda i: (0, 0)),
                  pl.BlockSpec((BLK, HD), lambda i: (i, 0)),
                  pl.BlockSpec((BLK, HD), lambda i: (i, 0))],
        out_specs=[pl.BlockSpec((BLK, wq), lambda i: (i, 0)), pl.BlockSpec((BLK, wk), lambda i: (i, 0)),
                   pl.BlockSpec((BLK, 2 * wk), lambda i: (i, 0))],
        out_shape=[jax.ShapeDtypeStruct((t, wq), bf16), jax.ShapeDtypeStruct((t, wk), bf16),
                   jax.ShapeDtypeStruct((t, 2 * wk), bf16)],
        compiler_params=_cparams(("parallel",), VMEM_STREAM_MB),
        name="attn_prep",
    )(p, p, p, q_g.reshape(1, HD), k_g.reshape(1, HD), *rope)


def _attn_kernel(tq, tk, n_ctx_tiles, n_main, q_ref, k_ref, v_ref, o_ref, m_scr, acc_scr, sa_scr, sb_scr):
    qi = pl.program_id(1)
    grp = H_AT // H_KV
    m_scr[...] = jnp.full_like(m_scr, NEG)
    acc_scr[...] = jnp.zeros_like(acc_scr)

    def scores(rows):
        kt = k_ref[rows, :]
        return [_mm_nt(q_ref[:, h * HD:(h + 1) * HD], kt) for h in range(grp)]

    def softmax_pv(get_s, rows, width):
        va = v_ref[rows, :]
        ps = []
        for h in range(grp):
            mx = get_s(h, 0)
            for c in range(1, width // HD):
                mx = jnp.maximum(mx, get_s(h, c))
            m_prev = m_scr[h]
            m_new = jnp.maximum(m_prev, jnp.max(mx, axis=-1, keepdims=True))
            alpha = jnp.exp2(m_prev - m_new)
            p = jnp.concatenate([jnp.exp2(get_s(h, c) - m_new).astype(bf16) for c in range(width // HD)], axis=1)
            m_scr[h] = m_new
            ps.append((alpha, p))
        for h in range(grp):
            alpha, p = ps[h]
            acc = acc_scr[h]
            pv = jnp.dot(p, va, preferred_element_type=f32)
            acc_scr[h] = jnp.concatenate([alpha * acc[:, :HD], alpha * acc[:, HD:]], axis=1) + pv

    @pl.when(qi < n_ctx_tiles)
    def _():
        ctx_rows = pl.ds(0, BLK)
        ss = scores(ctx_rows)
        softmax_pv(lambda h, c: ss[h][:, c * HD:(c + 1) * HD], ctx_rows, BLK)

    def main_rows(j):
        return pl.ds(pl.multiple_of(j * tk, HD), tk)

    def store_scores(s_ref, j):
        for h, s in enumerate(scores(main_rows(j))):
            s_ref[h] = s

    def pipelined_step(cur_ref, nxt_ref, j):
        store_scores(nxt_ref, jnp.minimum(j + 1, n_main - 1))
        softmax_pv(lambda h, c: cur_ref[h, :, c * HD:(c + 1) * HD], main_rows(j), tk)

    @pl.when(qi >= n_ctx_tiles)
    def _():
        store_scores(sa_scr, 0)

        def body(i, carry):
            pipelined_step(sa_scr, sb_scr, 2 * i)
            pipelined_step(sb_scr, sa_scr, 2 * i + 1)
            return carry
        lax.fori_loop(0, n_main // 2, body, 0)
        if n_main % 2:
            pipelined_step(sa_scr, sb_scr, n_main - 1)

    for h in range(grp):
        acc = acc_scr[h]
        o_ref[:, h * HD:(h + 1) * HD] = (acc[:, :HD] / acc[:, HD:HD + 1]).astype(o_ref.dtype)


def _attention(qr, kr, va):
    t = qr.shape[0]
    tq = 256
    tk = _row_tile(t, (1280, 768, 256))
    grp = H_AT // H_KV
    wg = grp * HD
    kern = functools.partial(_attn_kernel, tq, tk, BLK // tq, t // tk)
    return pl.pallas_call(
        kern,
        grid=(H_KV, t // tq),
        in_specs=[pl.BlockSpec((tq, wg), lambda g, i: (i, g)),
                  pl.BlockSpec((t, HD), lambda g, i: (0, g)),
                  pl.BlockSpec((t, 2 * HD), lambda g, i: (0, g))],
        out_specs=pl.BlockSpec((tq, wg), lambda g, i: (i, g)),
        out_shape=jax.ShapeDtypeStruct((t, H_AT * HD), bf16),
        scratch_shapes=[pltpu.VMEM((grp, tq, HD), f32), pltpu.VMEM((grp, tq, 2 * HD), f32),
                        pltpu.VMEM((grp, tq, tk), f32), pltpu.VMEM((grp, tq, tk), f32)],
        compiler_params=_cparams(("parallel", "arbitrary"), VMEM_RESIDENT_MB),
        name="flash_attention",
    )(qr, kr, va)


def _merge_kernel(tm, dnf_ref, dnb_ref, z_ref, at_ref, mlf_ref, mlb_ref, og_ref, x_ref, mod_ref,
                  dng_ref, mlg_ref, wo_ref, n2g_ref, rw1_ref, rw2_ref, xo_ref, h2_ref, lg_ref):
    i = pl.program_id(0)
    rows = i * tm + lax.broadcasted_iota(jnp.int32, (tm, 1), 0)
    is_ctx = rows < BLK
    w = N_HEADS_SCAN * HD

    def head_norm(x, g):
        return x * lax.rsqrt(jnp.mean(x * x, axis=-1, keepdims=True) + EPS) * g

    acc = jnp.dot(at_ref[...], wo_ref[w:w + H_AT * HD, :], preferred_element_type=f32)
    dn_parts, ml_parts = [], []
    for h in range(N_HEADS_SCAN):
        sl = slice(h * HD, (h + 1) * HD)
        z = z_ref[:, sl].astype(f32)
        dn_parts.append(head_norm(dnf_ref[:, sl] + dnb_ref[:, sl], dng_ref[...]) * (z * _sigmoid(z)))
        ml_parts.append(head_norm(mlf_ref[:, sl] + mlb_ref[:, sl], mlg_ref[...]) * _sigmoid(og_ref[:, sl].astype(f32)))
    dn = jnp.concatenate(dn_parts, axis=1).astype(bf16)
    ml = jnp.concatenate(ml_parts, axis=1).astype(bf16)
    acc = acc + jnp.dot(dn, wo_ref[0:w, :], preferred_element_type=f32)
    acc = acc + jnp.dot(ml, wo_ref[w + H_AT * HD:, :], preferred_element_type=f32)
    x = x_ref[...] + _mod_rows(mod_ref, 2, is_ctx) * acc
    xo_ref[...] = x
    y = x * lax.rsqrt(jnp.mean(x * x, axis=-1, keepdims=True) + EPS) * n2g_ref[...]
    h2 = y * (1.0 + _mod_rows(mod_ref, 4, is_ctx)) + _mod_rows(mod_ref, 3, is_ctx)
    hh = h2.astype(bf16)
    hb = lax.bitcast_convert_type(hh.astype(f32), jnp.uint32)
    h2_ref[...] = jnp.bitwise_or(jnp.bitwise_and(hb[:, D // 2:], jnp.uint32(0xFFFF0000)),
                                 jnp.right_shift(hb[:, :D // 2], jnp.uint32(16)))
    hl = (h2 - hh.astype(f32)).astype(bf16)
    r = jnp.dot(jnp.concatenate([hh, hl], axis=0), jnp.concatenate([rw1_ref[...], rw2_ref[...]], axis=1),
                preferred_element_type=f32)
    lg_ref[...] = (r[:tm, :128] + r[tm:, :128]) + (r[:tm, 128:] + r[tm:, 128:])


def _merge_outproj(dnf, dnb, p, at, mlf, mlb, x, mod, dn_g, ml_g, w_out, n2g, rw1, rw2):
    t = x.shape[0]
    tm = 256
    w = N_HEADS_SCAN * HD
    row = lambda i: (i, 0)
    full = lambda i: (0, 0)
    sw = pl.BlockSpec((tm, w), row)
    return pl.pallas_call(
        functools.partial(_merge_kernel, tm),
        grid=(t // tm,),
        in_specs=[sw, sw, pl.BlockSpec((tm, w), lambda i: (i, C_DNZ // w)),
                  pl.BlockSpec((tm, H_AT * HD), row), sw, sw,
                  pl.BlockSpec((tm, w), lambda i: (i, C_MLO // w)),
                  pl.BlockSpec((tm, D), row), pl.BlockSpec((8, 6 * D), full),
                  pl.BlockSpec((1, HD), full), pl.BlockSpec((1, HD), full),
                  pl.BlockSpec((D, D), full), pl.BlockSpec((1, D), full),
                  pl.BlockSpec((D, 128), full), pl.BlockSpec((D, 128), full)],
        out_specs=[pl.BlockSpec((tm, D), row), pl.BlockSpec((tm, D // 2), row), pl.BlockSpec((tm, 128), row)],
        out_shape=[jax.ShapeDtypeStruct((t, D), f32), jax.ShapeDtypeStruct((t, D // 2), jnp.uint32),
                   jax.ShapeDtypeStruct((t, 128), f32)],
        compiler_params=_cparams(("parallel",), VMEM_MATMUL_MB),
        name="merge_outproj",
    )(dnf, dnb, p, at, mlf, mlb, p, x, mod, dn_g.reshape(1, HD), ml_g.reshape(1, HD), w_out,
      n2g.reshape(1, D), rw1, rw2)


def _route_kernel(lg_ref, bias_ref, e_ref, g_ref):
    lt = lg_ref[...].T
    sc = [_sigmoid(lt[e:e + 1, :]) for e in range(N_EXPERTS)]
    bi = [sc[e] + bias_ref[e:e + 1, 0:1] for e in range(N_EXPERTS)]
    n_groups = N_EXPERTS // EXPERTS_PER_GROUP
    best, best_g = None, None
    for g in range(n_groups):
        a, b, c, d = bi[4 * g:4 * g + 4]
        gs = jnp.maximum(jnp.maximum(jnp.maximum(a + b, a + c), jnp.maximum(a + d, b + c)),
                         jnp.maximum(b + d, c + d))
        if g == 0:
            best, best_g = gs, jnp.zeros_like(gs, dtype=jnp.int32)
        else:
            better = gs > best
            best = jnp.where(better, gs, best)
            best_g = jnp.where(better, g, best_g)
    t1 = jnp.full_like(best, -jnp.inf)
    t2 = jnp.full_like(best, -jnp.inf)
    i1 = jnp.zeros_like(best_g)
    i2 = jnp.zeros_like(best_g)
    s1 = jnp.zeros_like(best)
    s2 = jnp.zeros_like(best)
    for e in range(N_EXPERTS):
        v = jnp.where(best_g == e // EXPERTS_PER_GROUP, bi[e], -jnp.inf)
        gt1 = v > t1
        gt2 = jnp.logical_and(jnp.logical_not(gt1), v > t2)
        t2 = jnp.where(gt1, t1, jnp.where(gt2, v, t2))
        i2 = jnp.where(gt1, i1, jnp.where(gt2, e, i2))
        s2 = jnp.where(gt1, s1, jnp.where(gt2, sc[e], s2))
        t1 = jnp.where(gt1, v, t1)
        i1 = jnp.where(gt1, e, i1)
        s1 = jnp.where(gt1, sc[e], s1)
    tot = s1 + s2
    zi = jnp.zeros_like(i1)
    zf = jnp.zeros_like(s1)
    e_ref[...] = jnp.concatenate([i1, i2, zi, zi, zi, zi, zi, zi], axis=0)
    g_ref[...] = jnp.concatenate([s1 / tot, s2 / tot, zf, zf, zf, zf, zf, zf], axis=0)


def _route(logits, router_bias):
    t = logits.shape[0]
    tm = 256
    bias = jnp.zeros((N_EXPERTS, 128), f32).at[:, 0].set(router_bias)
    return pl.pallas_call(
        _route_kernel,
        grid=(t // tm,),
        in_specs=[pl.BlockSpec((tm, 128), lambda i: (i, 0)), pl.BlockSpec((N_EXPERTS, 128), lambda i: (0, 0))],
        out_specs=[pl.BlockSpec((8, tm), lambda i: (0, i)), pl.BlockSpec((8, tm), lambda i: (0, i))],
        out_shape=[jax.ShapeDtypeStruct((8, t), jnp.int32), jax.ShapeDtypeStruct((8, t), f32)],
        compiler_params=_cparams(("parallel",)),
        name="route_top2",
    )(logits, bias)


def _dispatch(e_rows, g_rows, n_blocks):
    t = e_rows.shape[1]
    n = 2 * t
    flat_e = e_rows[0:2].reshape(n)
    flat_w = g_rows[0:2].reshape(n)
    order = jnp.argsort(flat_e, stable=True).astype(jnp.int32)
    experts = jnp.arange(N_EXPERTS, dtype=jnp.int32)
    counts = jnp.sum(flat_e[:, None] == experts[None, :], axis=0).astype(jnp.int32)
    starts = jnp.cumsum(counts) - counts
    padded = (counts + MOE_BM - 1) // MOE_BM * MOE_BM
    p_ends = jnp.cumsum(padded)
    p_starts = p_ends - padded
    blk_start = jnp.arange(n_blocks, dtype=jnp.int32) * MOE_BM
    blk_e = jnp.minimum(jnp.sum(p_ends[None, :] <= blk_start[:, None], axis=1), N_EXPERTS - 1).astype(jnp.int32)
    sel = (blk_e[:, None] == experts[None, :]).astype(jnp.int32)
    pick = lambda v: jnp.sum(sel * v[None, :], axis=1)[:, None]
    rank = blk_start[:, None] + jnp.arange(MOE_BM, dtype=jnp.int32)[None, :] - pick(p_starts)
    valid = jnp.logical_and(rank >= 0, rank < pick(counts))
    src = order[jnp.clip(pick(starts) + rank, 0, n - 1)]
    tok = jnp.where(valid, jnp.where(src >= t, src - t, src), 0).astype(jnp.int32).reshape(n_blocks, 1, MOE_BM)
    dst = jnp.where(valid, src, -1).astype(jnp.int32).reshape(n_blocks, 1, MOE_BM)
    wt = jnp.where(valid, flat_w[src], 0.0).reshape(n_blocks, MOE_BM, 1)
    n_steps = n_blocks + 2
    tok_s = jnp.concatenate([tok, jnp.zeros((2, 1, MOE_BM), jnp.int32)], axis=0)
    e_s = jnp.concatenate([blk_e[:1], blk_e, blk_e[-1:]], axis=0)
    wt_s = jnp.concatenate([jnp.zeros((1, MOE_BM, 1), f32), wt, jnp.zeros((1, MOE_BM, 1), f32)], axis=0)
    dst_s = jnp.concatenate([jnp.full((2, 1, MOE_BM), -1, jnp.int32), dst], axis=0)
    is_pad = (dst_s < 0).reshape(-1)
    pad_rank = (jnp.cumsum(is_pad.astype(jnp.int32)) - 1).reshape(n_steps, 1, MOE_BM)
    dst_s = jnp.where(dst_s >= 0, dst_s, 2 * t + pad_rank)
    return e_s, tok_s, dst_s, wt_s


def _moe_kernel(n_steps, e_ref, tok_ref, dst_ref, wt_ref, h_hbm, wg_ref, wu_ref, wd_ref, out_hbm,
                xb0, xb1, yb0, yb1, sems):
    del e_ref
    s = pl.program_id(0)
    xbs, ybs = (xb0, xb1), (yb0, yb1)

    def wait_step_dmas():
        pltpu.make_async_copy(h_hbm.at[pl.ds(0, MOE_BM)], xb0, sems.at[0]).wait()
        pltpu.make_async_copy(yb0, out_hbm.at[pl.ds(0, MOE_BM)], sems.at[1]).wait()

    @pl.when(s == 0)
    def _():
        xb1[...] = jnp.zeros_like(xb1)
        yb1[...] = jnp.zeros_like(yb1)

    @pl.when(s > 0)
    def _():
        wait_step_dmas()

    def step(par):
        x_in, x_cur = xbs[par], xbs[1 - par]
        y_cur, y_out = ybs[par], ybs[1 - par]
        for r in range(MOE_BM):
            pltpu.make_async_copy(h_hbm.at[pl.ds(tok_ref[0, 0, r], 1)], x_in.at[pl.ds(r, 1)], sems.at[0]).start()
        w = x_cur[...]
        x_lo = lax.bitcast_convert_type(jnp.left_shift(w, jnp.uint32(16)), f32).astype(bf16)
        x_hi = lax.bitcast_convert_type(jnp.bitwise_and(w, jnp.uint32(0xFFFF0000)), f32).astype(bf16)
        dh = D // 2
        g = (jnp.dot(x_lo, wg_ref[0, :dh, :], preferred_element_type=f32)
             + jnp.dot(x_hi, wg_ref[0, dh:, :], preferred_element_type=f32))
        u = (jnp.dot(x_lo, wu_ref[0, :dh, :], preferred_element_type=f32)
             + jnp.dot(x_hi, wu_ref[0, dh:, :], preferred_element_type=f32))
        a = (g * _sigmoid(g) * u).astype(bf16)
        y_cur[...] = jnp.dot(a, wd_ref[0], preferred_element_type=f32) * wt_ref[0]
        for r in range(MOE_BM):
            pltpu.make_async_copy(y_out.at[pl.ds(r, 1)], out_hbm.at[pl.ds(dst_ref[0, 0, r], 1)], sems.at[1]).start()

    @pl.when(lax.rem(s, 2) == 0)
    def _():
        step(0)

    @pl.when(lax.rem(s, 2) == 1)
    def _():
        step(1)

    @pl.when(s == n_steps - 1)
    def _():
        wait_step_dmas()


def _moe(h2, e_s, tok_s, dst_s, wt_s, wg, wu, wd):
    n_steps = tok_s.shape[0]
    grid_spec = pltpu.PrefetchScalarGridSpec(
        num_scalar_prefetch=1,
        grid=(n_steps,),
        in_specs=[pl.BlockSpec((1, 1, MOE_BM), lambda s, e: (s, 0, 0), memory_space=pltpu.SMEM),
                  pl.BlockSpec((1, 1, MOE_BM), lambda s, e: (s, 0, 0), memory_space=pltpu.SMEM),
                  pl.BlockSpec((1, MOE_BM, 1), lambda s, e: (s, 0, 0)),
                  pl.BlockSpec(memory_space=pl.ANY),
                  pl.BlockSpec((1, D, D_EXPERT), lambda s, e: (e[s], 0, 0)),
                  pl.BlockSpec((1, D, D_EXPERT), lambda s, e: (e[s], 0, 0)),
                  pl.BlockSpec((1, D_EXPERT, D), lambda s, e: (e[s], 0, 0))],
        out_specs=pl.BlockSpec(memory_space=pl.ANY),
        scratch_shapes=[pltpu.VMEM((MOE_BM, D // 2), jnp.uint32)] * 2 + [pltpu.VMEM((MOE_BM, D), f32)] * 2
        + [pltpu.SemaphoreType.DMA((2,))],
    )
    return pl.pallas_call(
        functools.partial(_moe_kernel, n_steps),
        grid_spec=grid_spec,
        out_shape=jax.ShapeDtypeStruct((n_steps * MOE_BM, D), f32),
        compiler_params=_cparams(("arbitrary",), VMEM_RESIDENT_MB),
        name="moe_experts",
    )(e_s, tok_s, dst_s, wt_s, h2, wg, wu, wd)


def _final_kernel(x_ref, y0_ref, y1_ref, mod_ref, g_ref, o_ref):
    x = x_ref[...] + mod_ref[0:1, 5 * D:6 * D] * (y0_ref[...] + y1_ref[...])
    o_ref[...] = x * lax.rsqrt(jnp.mean(x * x, axis=-1, keepdims=True) + EPS) * g_ref[...]


def _final(x, moe, mod, g):
    t = x.shape[0]
    tm = 256
    nrow = t // tm
    nctx = BLK // tm
    return pl.pallas_call(
        _final_kernel,
        grid=(nrow - nctx,),
        in_specs=[pl.BlockSpec((tm, D), lambda i: (i + nctx, 0)),
                  pl.BlockSpec((tm, D), lambda i: (i + nctx, 0)),
                  pl.BlockSpec((tm, D), lambda i: (i + nctx + nrow, 0)),
                  pl.BlockSpec((8, 6 * D), lambda i: (0, 0)),
                  pl.BlockSpec((1, D), lambda i: (0, 0))],
        out_specs=pl.BlockSpec((tm, D), lambda i: (i, 0)),
        out_shape=jax.ShapeDtypeStruct((t - BLK, D), f32),
        compiler_params=_cparams(("parallel",), VMEM_STREAM_MB),
        name="final_norm",
    )(x, moe, moe, mod, g.reshape(1, D))


def _cast_kernel(x_ref, o_ref):
    o_ref[...] = x_ref[...].astype(o_ref.dtype)


def _to_bf16(w, layer):
    shape = w.shape[1:]
    w2 = w.reshape(-1, shape[-1])
    cols = shape[-1]
    rows = w2.shape[0] // w.shape[0]
    tr = _row_tile(rows, (1024, 512, 256))
    off = layer * (rows // tr)
    out = pl.pallas_call(
        _cast_kernel,
        grid=(rows // tr,),
        in_specs=[pl.BlockSpec((tr, cols), lambda i: (i + off, 0))],
        out_specs=pl.BlockSpec((tr, cols), lambda i: (i, 0)),
        out_shape=jax.ShapeDtypeStruct((rows, cols), bf16),
        compiler_params=_cparams(("parallel",), VMEM_MATMUL_MB),
        name="cast_bf16",
    )(w2)
    return out.reshape(shape)


def _prep_in_weights(w_in):
    splits = (1536, 512, 8, 8, 1024, 256, 256, 512, 512, 512, 512, 8, 8)
    offs = [0]
    for s in splits:
        offs.append(offs[-1] + s)
    part = lambda i: w_in[:, offs[i]:offs[i + 1]]
    main = jnp.concatenate([part(i) for i in (0, 1, 4, 5, 6, 7, 8, 9, 10)], axis=1).astype(bf16)
    gates = jnp.concatenate([part(i) for i in (2, 3, 11, 12)], axis=1)
    gates = jnp.pad(gates, ((0, 0), (0, 128 - gates.shape[1])))
    g1 = gates.astype(bf16)
    g2 = (gates - g1.astype(f32)).astype(bf16)
    return main, g1, g2


def _layer(l, rope, x, moe_prev, mods, norm1_g, norm2_g, w_in, dn_conv, dn_a_log, dn_dt_bias, dn_norm_g,
           q_norm_g, k_norm_g, ml_i_bias, ml_f_bias, ml_norm_g, w_out, rw1, rw2, router_bias,
           w_gate, w_up, w_down):
    t = x.shape[0]
    w_main, wg1, wg2 = _prep_in_weights(w_in[l])
    x, h, graw = _norm1(x, moe_prev, mods[l - 1] if l else None, mods[l], norm1_g[l], wg1, wg2)
    p = _matmul(h, w_main, bf16)
    conv_w = jnp.pad(dn_conv[l], ((0, 8 - CONV_W), (0, 0)))
    gate_params = jnp.zeros((8, 128), f32)
    gate_params = gate_params.at[0, 8:16].set(dn_dt_bias[l].reshape(8))
    gate_params = gate_params.at[0, 16:24].set(ml_i_bias[l].reshape(8))
    gate_params = gate_params.at[0, 24:32].set(ml_f_bias[l].reshape(8))
    gate_params = gate_params.at[1, 8:16].set(dn_a_log[l].reshape(8))
    dnq, gcol, grow = _scan_prep(p, graw, conv_w, gate_params)
    dnf, dnb = _gdn_scan(dnq, gcol, grow)
    mlf, mlb = _mlstm_scan(p, gcol, grow)
    qr, kr, va = _attn_prep(p, q_norm_g[l], k_norm_g[l], rope)
    at = _attention(qr, kr, va)
    x, h2, logits = _merge_outproj(dnf, dnb, p, at, mlf, mlb, x, mods[l], dn_norm_g[l], ml_norm_g[l],
                                   w_out[l].astype(bf16), norm2_g[l], rw1, rw2)
    e_rows, g_rows = _route(logits, router_bias)
    n_blocks = (2 * t + N_EXPERTS * (MOE_BM - 1) + MOE_BM - 1) // MOE_BM
    blk_e, tok, dst, wt = _dispatch(e_rows, g_rows, n_blocks)
    moe = _moe(h2, blk_e, tok, dst, wt, _to_bf16(w_gate, l), _to_bf16(w_up, l), _to_bf16(w_down, l))
    return x, moe


def kernel(x, c, ctx, c_ctx, w_mod, b_mod, norm1_g, norm2_g, w_in, dn_conv, dn_a_log, dn_dt_bias, dn_norm_g, q_norm_g, k_norm_g, ml_i_bias, ml_f_bias, ml_norm_g, w_out, router_w, router_bias, w_gate, w_up, w_down, final_norm_g):
    b, seq, d = x.shape
    assert b == 1 and d == D and ctx.shape[1] == BLK and seq % BLK == 0 and seq % GRID_W == 0
    depth = w_mod.shape[0]
    mods = _mods(c, c_ctx, w_mod, b_mod)
    xs = jnp.concatenate([ctx[0], x[0]], axis=0)
    rw = jnp.pad(router_w, ((0, 0), (0, 128 - N_EXPERTS)))
    rw1 = rw.astype(bf16)
    rw2 = (rw - rw1.astype(f32)).astype(bf16)
    moe = None
    rope = _rope_tables(xs.shape[0])
    for l in range(depth):
        xs, moe = _layer(l, rope, xs, moe, mods, norm1_g, norm2_g, w_in, dn_conv, dn_a_log, dn_dt_bias,
                         dn_norm_g, q_norm_g, k_norm_g, ml_i_bias, ml_f_bias, ml_norm_g, w_out, rw1, rw2,
                         router_bias, w_gate, w_up, w_down)
    out = _final(xs, moe, mods[depth - 1], final_norm_g)
    return out.reshape(b, seq, d)
```

```python
import functools

import jax
import jax.numpy as jnp
from jax import lax
from jax.experimental import pallas as pl
from jax.experimental.pallas import tpu as pltpu

f32 = jnp.float32
bf16 = jnp.bfloat16

D = 2048
HD = 128
N_HEADS_SCAN = 4
H_AT = 8
H_KV = 2
CHUNK = 64
BLK = 256
CPB = BLK // CHUNK
GRID_W = 64
ROPE_THETA = 10000.0
QK_SCALE = HD ** -0.5
LOG2E = 1.4426950408889634
N_EXPERTS = 16
EXPERTS_PER_GROUP = 4
D_EXPERT = D // 2
MOE_BM = 256
EPS = 1e-6
NEG = -1e30
CONV_W = 5

C_DNQKV, C_DNZ, C_ATQ, C_ATK, C_ATV, C_MLQ, C_MLK, C_MLV, C_MLO, P_COLS = (
    0, 1536, 2048, 3072, 3328, 3584, 4096, 4608, 5120, 5632)

VMEM_STREAM_MB = 40
VMEM_MATMUL_MB = 48
VMEM_RESIDENT_MB = 52


def _cparams(sems, vmem_mb=None):
    return pltpu.CompilerParams(
        dimension_semantics=sems,
        vmem_limit_bytes=None if vmem_mb is None else vmem_mb << 20)


def _mm(a, b):
    return jnp.dot(a.astype(bf16), b.astype(bf16), preferred_element_type=f32)


def _mm_nt(a, b):
    return lax.dot_general(a.astype(bf16), b.astype(bf16), (((1,), (1,)), ((), ())),
                           preferred_element_type=f32)


def _mm_tn(a, b):
    return lax.dot_general(a.astype(bf16), b.astype(bf16), (((0,), (0,)), ((), ())),
                           preferred_element_type=f32)


def _split3(x):
    x1 = x.astype(bf16)
    r1 = x - x1.astype(f32)
    x2 = r1.astype(bf16)
    x3 = (r1 - x2.astype(f32)).astype(bf16)
    return x1, x2, x3


def _sigmoid(x):
    return 1.0 / (1.0 + jnp.exp(-x))


def _softplus(x):
    return jnp.maximum(x, 0.0) + jnp.log(1.0 + jnp.exp(-jnp.abs(x)))


def _mod_kernel(s_ref, w_ref, b_ref, o_ref):
    s = s_ref[...]
    s = s * _sigmoid(s)
    o_ref[0] = jnp.dot(s, w_ref[0], preferred_element_type=f32,
                       precision=lax.Precision.HIGHEST) + b_ref[0]


def _mods(c, c_ctx, w_mod, b_mod):
    depth, d, n6 = w_mod.shape
    s = jnp.zeros((8, d), f32).at[0].set(c[0]).at[1].set(c_ctx)
    tn = 1024
    return pl.pallas_call(
        _mod_kernel,
        grid=(depth, n6 // tn),
        in_specs=[pl.BlockSpec((8, d), lambda l, j: (0, 0)),
                  pl.BlockSpec((1, d, tn), lambda l, j: (l, 0, j)),
                  pl.BlockSpec((1, 1, tn), lambda l, j: (l, 0, j))],
        out_specs=pl.BlockSpec((1, 8, tn), lambda l, j: (l, 0, j)),
        out_shape=jax.ShapeDtypeStruct((depth, 8, n6), f32),
        compiler_params=_cparams(("parallel", "parallel"), VMEM_STREAM_MB),
        name="mod_vectors",
    )(s, w_mod, b_mod.reshape(depth, 1, n6))


def _mod_rows(mod_ref, k, is_ctx):
    lat = mod_ref[0:1, k * D:(k + 1) * D]
    ctx = mod_ref[1:2, k * D:(k + 1) * D]
    return jnp.where(is_ctx, ctx, lat)


def _stream_rows(i, ctx_ref, x_ref):
    return jnp.where(i == 0, ctx_ref[...], x_ref[...])


def _stream_specs(x, tm):
    if isinstance(x, tuple):
        assert tm == BLK
        ctx2, lat2 = x
        specs = [pl.BlockSpec((BLK, D), lambda i: (0, 0)),
                 pl.BlockSpec((tm, D), lambda i: (jnp.maximum(i - 1, 0), 0))]
        return specs, [ctx2, lat2], ctx2.shape[0] + lat2.shape[0]
    return [pl.BlockSpec((tm, D), lambda i: (i, 0))], [x], x.shape[0]


def _norm1_kernel(has_moe, split, tm, *refs):
    i = pl.program_id(0)
    if split:
        x = _stream_rows(i, refs[0], refs[1])
        refs = refs[1:]
    else:
        x = refs[0][...]
    if has_moe:
        _, y0_ref, y1_ref, modp_ref, mod_ref, g_ref, wg1_ref, wg2_ref, xo_ref, h_ref, gr_ref = refs
    else:
        _, mod_ref, g_ref, wg1_ref, wg2_ref, h_ref, gr_ref = refs
    rows = i * tm + lax.broadcasted_iota(jnp.int32, (tm, 1), 0)
    is_ctx = rows < BLK
    if has_moe:
        x = x + _mod_rows(modp_ref, 5, is_ctx) * (y0_ref[...] + y1_ref[...])
        xo_ref[...] = x
    ms = jnp.mean(x * x, axis=-1, keepdims=True)
    y = x * lax.rsqrt(ms + EPS) * g_ref[...]
    h = y * (1.0 + _mod_rows(mod_ref, 1, is_ctx)) + _mod_rows(mod_ref, 0, is_ctx)
    hh = h.astype(bf16)
    h_ref[...] = hh
    hl = (h - hh.astype(f32)).astype(bf16)
    wg1 = wg1_ref[...]
    gr_ref[...] = (jnp.dot(hh, wg1, preferred_element_type=f32)
                   + jnp.dot(hl, wg1, preferred_element_type=f32)
                   + jnp.dot(hh, wg2_ref[...], preferred_element_type=f32))


def _norm1(x, moe, mod_prev, mod_cur, g, wg1, wg2):
    tm = 256
    in_specs, args, t = _stream_specs(x, tm)
    nrow = t // tm
    row = lambda i: (i, 0)
    full = lambda i: (0, 0)
    if moe is not None:
        in_specs += [pl.BlockSpec((tm, D), row), pl.BlockSpec((tm, D), lambda i: (i + nrow, 0)),
                     pl.BlockSpec((8, 6 * D), full)]
        args += [moe, moe, mod_prev]
    in_specs += [pl.BlockSpec((8, 6 * D), full), pl.BlockSpec((1, D), full),
                 pl.BlockSpec((D, 128), full), pl.BlockSpec((D, 128), full)]
    args += [mod_cur, g.reshape(1, D), wg1, wg2]
    out_specs = [pl.BlockSpec((tm, D), row), pl.BlockSpec((tm, 128), row)]
    out_shape = [jax.ShapeDtypeStruct((t, D), bf16), jax.ShapeDtypeStruct((t, 128), f32)]
    if moe is not None:
        out_specs = [pl.BlockSpec((tm, D), row)] + out_specs
        out_shape = [jax.ShapeDtypeStruct((t, D), f32)] + out_shape
    outs = pl.pallas_call(
        functools.partial(_norm1_kernel, moe is not None, isinstance(x, tuple), tm),
        grid=(nrow,), in_specs=in_specs, out_specs=out_specs, out_shape=out_shape,
        compiler_params=_cparams(("parallel",), VMEM_STREAM_MB),
        name="norm1_modulate",
    )(*args)
    if moe is not None:
        return outs
    return [x] + list(outs)


def _matmul_kernel(a_ref, b_ref, o_ref):
    o_ref[...] = jnp.dot(a_ref[...], b_ref[...], preferred_element_type=f32).astype(o_ref.dtype)


def _row_tile(t, choices):
    for c in choices:
        if t % c == 0:
            return c
    raise ValueError(f"no row tile for {t}")


def _matmul(a, b, out_dtype):
    m, k = a.shape
    n = b.shape[1]
    tm = _row_tile(m, (1280, 768, 512, 256))
    tn = _row_tile(n, (1408, 1024, 512))
    return pl.pallas_call(
        _matmul_kernel,
        grid=(n // tn, m // tm),
        in_specs=[pl.BlockSpec((tm, k), lambda j, i: (i, 0)),
                  pl.BlockSpec((k, tn), lambda j, i: (0, j))],
        out_specs=pl.BlockSpec((tm, tn), lambda j, i: (i, j)),
        out_shape=jax.ShapeDtypeStruct((m, n), out_dtype),
        compiler_params=_cparams(("parallel", "parallel"), VMEM_MATMUL_MB),
        name="in_projection",
    )(a, b)


def _scan_prep_kernel(nblk, cur_ref, prev_ref, next_ref, cw_ref, graw_ref, gp_ref, q_ref, gc_ref, grow_ref, xs):
    i = pl.program_id(0)
    has_prev = i >= 2
    has_next = jnp.logical_and(i >= 1, i < nblk - 1)
    xs[0:16, :] = jnp.where(has_prev, prev_ref[...].astype(f32), 0.0)
    xs[16:16 + BLK, :] = cur_ref[...].astype(f32)
    xs[16 + BLK:32 + BLK, :] = jnp.where(has_next, next_ref[...].astype(f32), 0.0)
    acc = cw_ref[0:1, :] * xs[pl.ds(16 - CONV_W // 2, BLK), :]
    for j in range(1, CONV_W):
        acc = acc + cw_ref[j:j + 1, :] * xs[pl.ds(16 - CONV_W // 2 + j, BLK), :]
    a = acc * _sigmoid(acc)
    w = N_HEADS_SCAN * HD
    for h in range(2 * N_HEADS_SCAN):
        xh = a[:, h * HD:(h + 1) * HD]
        inv = lax.rsqrt(jnp.sum(xh * xh, axis=-1, keepdims=True) + EPS)
        scale = QK_SCALE if h < N_HEADS_SCAN else 1.0
        q_ref[:, h * HD:(h + 1) * HD] = (xh * (inv * scale)).astype(bf16)
    q_ref[:, 2 * w:3 * w] = a[:, 2 * w:3 * w].astype(bf16)

    z = graw_ref[...] + gp_ref[0:1, :]
    lane = lax.broadcasted_iota(jnp.int32, (BLK, 128), 1)
    sp = _softplus(z)
    vals = jnp.where(lane < 8, _sigmoid(z),
                     jnp.where(lane < 16, -jnp.exp(gp_ref[1:2, :]) * sp,
                               jnp.where(lane < 24, z, z - sp)))
    r = lax.broadcasted_iota(jnp.int32, (BLK, BLK), 0)
    c = lax.broadcasted_iota(jnp.int32, (BLK, BLK), 1)
    same = jnp.right_shift(r, 6) == jnp.right_shift(c, 6)
    tri_lo = jnp.where(jnp.logical_and(same, r >= c), 1.0, 0.0).astype(bf16)
    tri_up = jnp.where(jnp.logical_and(same, r <= c), 1.0, 0.0).astype(bf16)
    v1, v2, v3 = _split3(vals)
    dot = functools.partial(jnp.dot, preferred_element_type=f32)
    prefix = dot(tri_lo, v1) + dot(tri_lo, v2) + dot(tri_lo, v3)
    suffix = dot(tri_up, v1) + dot(tri_up, v2) + dot(tri_up, v3)
    is_cum = jnp.logical_and(jnp.bitwise_and(lane, 8) == 8, lane < 32)
    is_bwd = jnp.bitwise_and(lane, 4) == 4
    out = jnp.where(is_cum, jnp.where(is_bwd, suffix, prefix), vals)
    gc_ref[...] = out
    gt = out.T
    for cc in range(CPB):
        grow_ref[cc] = gt[0:32, cc * CHUNK:(cc + 1) * CHUNK]


def _scan_prep(p, graw, conv_w, gate_params):
    t = p.shape[0]
    nblk = t // BLK
    wq = 3 * N_HEADS_SCAN * HD
    n16 = t // 16
    return pl.pallas_call(
        functools.partial(_scan_prep_kernel, nblk),
        grid=(nblk,),
        in_specs=[pl.BlockSpec((BLK, wq), lambda i: (i, 0)),
                  pl.BlockSpec((16, wq), lambda i: (jnp.maximum(i * (BLK // 16) - 1, 0), 0)),
                  pl.BlockSpec((16, wq), lambda i: (jnp.minimum((i + 1) * (BLK // 16), n16 - 1), 0)),
                  pl.BlockSpec((8, wq), lambda i: (0, 0)),
                  pl.BlockSpec((BLK, 128), lambda i: (i, 0)),
                  pl.BlockSpec((8, 128), lambda i: (0, 0))],
        out_specs=[pl.BlockSpec((BLK, wq), lambda i: (i, 0)),
                   pl.BlockSpec((BLK, 128), lambda i: (i, 0)),
                   pl.BlockSpec((CPB, 32, CHUNK), lambda i: (i, 0, 0))],
        out_shape=[jax.ShapeDtypeStruct((t, wq), bf16),
                   jax.ShapeDtypeStruct((t, 128), f32),
                   jax.ShapeDtypeStruct((t // CHUNK, 32, CHUNK), f32)],
        scratch_shapes=[pltpu.VMEM((BLK + 32, wq), f32)],
        compiler_params=_cparams(("parallel",), VMEM_STREAM_MB),
        name="scan_prep",
    )(p, p, p, conv_w, graw, gate_params)


def _tri_masks():
    r = lax.broadcasted_iota(jnp.int32, (CHUNK, CHUNK), 0)
    c = lax.broadcasted_iota(jnp.int32, (CHUNK, CHUNK), 1)
    blk = jnp.right_shift(r, 4) == jnp.right_shift(c, 4)
    eye = jnp.where(r == c, 1.0, 0.0)
    return (r >= c, r <= c), (r > c, r < c), blk, eye


def _gdn_kernel(qf_ref, qb_ref, gcf_ref, gcb_ref, grf_ref, grb_ref, of_ref, ob_ref, s_scr):
    @pl.when(pl.program_id(0) == 0)
    def _():
        s_scr[...] = jnp.zeros_like(s_scr)

    incl, strict, blk, eye = _tri_masks()
    w = N_HEADS_SCAN * HD

    units = [(d, h) for d in range(2) for h in range(N_HEADS_SCAN)]
    rows_cat = lambda a, b: jnp.concatenate([a, b], axis=0)
    cols_cat = lambda a, b: jnp.concatenate([a, b], axis=1)
    C = CHUNK

    def chunk_pair(pp, carry):
        ld, tags = [], []
        for off in range(2):
            for d, h in units:
                cc = 2 * pp + off
                c = cc if d == 0 else CPB - 1 - cc
                q_ref, gc_ref, gr_ref = (qf_ref, gcf_ref, grf_ref) if d == 0 else (qb_ref, gcb_ref, grb_ref)
                rows = pl.ds(pl.multiple_of(c * C, C), C)
                u_idx = d * N_HEADS_SCAN + h
                q = q_ref[rows, h * HD:(h + 1) * HD]
                k = q_ref[rows, w + h * HD:w + (h + 1) * HD]
                v = q_ref[rows, 2 * w + h * HD:2 * w + (h + 1) * HD]
                beta = gc_ref[rows, u_idx:u_idx + 1]
                cum_c = gc_ref[rows, 8 + u_idx:9 + u_idx]
                cum_r = gr_ref[c][8 + u_idx:9 + u_idx, :]
                tot = cum_c[C - 1:C, :] if d == 0 else cum_c[0:1, :]
                ld.append((rows, q, k, v, beta, cum_c, cum_r, tot))
                tags.append((d, h))
        g1 = [_mm_nt(rows_cat(k, q), k) for (_, q, k, *_) in ld]
        st = []
        for (d, h), (rows, q, k, v, beta, cum_c, cum_r, tot), g in zip(tags, ld, g1):
            decay = jnp.exp(jnp.where(incl[d], cum_c - cum_r, NEG))
            nm = jnp.where(strict[d], beta * g[:C] * decay, 0.0)
            dm = jnp.where(blk, nm, 0.0)
            kf = k.astype(f32)
            e_c = jnp.exp(cum_c)
            rhs = cols_cat(cols_cat((beta * e_c) * kf, beta * v.astype(f32)), nm - dm)
            st.append(dict(dm=dm, rhs=rhs, qk=g[C:] * decay, k_dec=kf * jnp.exp(tot - cum_c),
                           q_dec=q.astype(f32) * e_c, g_last=jnp.exp(tot), p1=eye - dm))
        m2 = [_mm(s["dm"], s["dm"]) for s in st]
        r = [_mm(rows_cat(s["p1"], m), m) for s, m in zip(st, m2)]
        p2 = [s["p1"] + x[:C] for s, x in zip(st, r)]
        m4 = [x[C:] for x in r]
        r = [_mm(rows_cat(p, m), m) for p, m in zip(p2, m4)]
        p3 = [p + x[:C] for p, x in zip(p2, r)]
        m8 = [x[C:] for x in r]
        dinv = [p + _mm(p, m) for p, m in zip(p3, m8)]
        r = [_mm(di, s["rhs"]) for di, s in zip(dinv, st)]
        t1 = [x[:, :2 * HD] for x in r]
        qm = [x[:, 2 * HD:] for x in r]
        r = [_mm(qq, cols_cat(t, qq)) for qq, t in zip(qm, t1)]
        a1 = [x[:, :2 * HD] for x in r]
        qm2 = [x[:, 2 * HD:] for x in r]
        b2 = [_mm(q2, t) for q2, t in zip(qm2, t1)]
        c3 = [_mm(qq, b) for qq, b in zip(qm, b2)]
        sol = [t - a + b - c for t, a, b, c in zip(t1, a1, b2, c3)]
        nu = len(units)
        for off in range(2):
            sl = slice(off * nu, (off + 1) * nu)
            s_old = [s_scr[i] for i in range(nu)]
            r = [_mm(rows_cat(x[:, :HD], s["q_dec"]), so) for x, s, so in zip(sol[sl], st[sl], s_old)]
            u = [x[:, HD:] - y[:C] for x, y in zip(sol[sl], r)]
            o_intra = [_mm(s["qk"], uu) for s, uu in zip(st[sl], u)]
            s_add = [_mm_tn(s["k_dec"], uu) for s, uu in zip(st[sl], u)]
            for i, ((d, h), l, s) in enumerate(zip(units, ld[sl], st[sl])):
                o_ref = of_ref if d == 0 else ob_ref
                o_ref[l[0], h * HD:(h + 1) * HD] = r[i][C:] + o_intra[i]
                s_scr[i] = s["g_last"] * s_old[i] + s_add[i]
        return carry

    lax.fori_loop(0, CPB // 2, chunk_pair, 0)


def _bwd_block(nblk):
    return lambda s: (jnp.where(s == 0, 0, nblk - s), 0)


def _gdn_scan(qkv, gcol, grow):
    t = qkv.shape[0]
    nblk = t // BLK
    wq = 3 * N_HEADS_SCAN * HD
    w = N_HEADS_SCAN * HD
    fwd = lambda s: (s, 0)
    bwd = _bwd_block(nblk)
    fwd3 = lambda s: (s, 0, 0)
    bwd3 = lambda s: (jnp.where(s == 0, 0, nblk - s), 0, 0)
    return pl.pallas_call(
        _gdn_kernel,
        grid=(nblk,),
        in_specs=[pl.BlockSpec((BLK, wq), fwd), pl.BlockSpec((BLK, wq), bwd),
                  pl.BlockSpec((BLK, 128), fwd), pl.BlockSpec((BLK, 128), bwd),
                  pl.BlockSpec((CPB, 32, CHUNK), fwd3), pl.BlockSpec((CPB, 32, CHUNK), bwd3)],
        out_specs=[pl.BlockSpec((BLK, w), fwd), pl.BlockSpec((BLK, w), bwd)],
        out_shape=[jax.ShapeDtypeStruct((t, w), f32), jax.ShapeDtypeStruct((t, w), f32)],
        scratch_shapes=[pltpu.VMEM((2 * N_HEADS_SCAN, HD, HD), f32)],
        compiler_params=_cparams(("arbitrary",), VMEM_STREAM_MB),
        name="gdn_scan",
    )(qkv, qkv, gcol, gcol, grow, grow)


def _mlstm_kernel(pf_q, pf_k, pf_v, pb_q, pb_k, pb_v, gcf_ref, gcb_ref, grf_ref, grb_ref,
                  of_ref, ob_ref, c_scr, m_scr):
    @pl.when(pl.program_id(0) == 0)
    def _():
        c_scr[...] = jnp.zeros_like(c_scr)
        m_scr[...] = jnp.full_like(m_scr, NEG)

    incl, _, _, _ = _tri_masks()
    ones_col = jnp.where(lax.broadcasted_iota(jnp.int32, (CHUNK, HD), 1) == 0, 1.0, 0.0).astype(bf16)

    units = [(d, h) for d in range(2) for h in range(N_HEADS_SCAN)]

    def chunk(cc, carry):
        ld = []
        for d, h in units:
            c = cc if d == 0 else CPB - 1 - cc
            q_ref, k_ref, v_ref, gc_ref, gr_ref = (
                (pf_q, pf_k, pf_v, gcf_ref, grf_ref) if d == 0 else (pb_q, pb_k, pb_v, gcb_ref, grb_ref))
            rows = pl.ds(pl.multiple_of(c * CHUNK, CHUNK), CHUNK)
            u_idx = d * N_HEADS_SCAN + h
            q = q_ref[rows, h * HD:(h + 1) * HD]
            k = k_ref[rows, h * HD:(h + 1) * HD]
            v = v_ref[rows, h * HD:(h + 1) * HD]
            i_c = gc_ref[rows, 16 + u_idx:17 + u_idx]
            b_c = gc_ref[rows, 24 + u_idx:25 + u_idx]
            grow = gr_ref[c]
            i_r = grow[16 + u_idx:17 + u_idx, :]
            b_r = grow[24 + u_idx:25 + u_idx, :]
            b_last = b_c[CHUNK - 1:CHUNK, :] if d == 0 else b_c[0:1, :]
            w_log = b_last - b_c + i_c
            m_st = jnp.max(w_log, axis=0, keepdims=True)
            e_w = jnp.exp(w_log - m_st)
            ld.append(dict(rows=rows, q=q, k=k, v=v, i_r=i_r, b_c=b_c, b_r=b_r, b_last=b_last, m_st=m_st, e_w=e_w))
        nu = len(units)
        v_aug = [jnp.concatenate([l["v"], ones_col], axis=1) for l in ld]
        qk = [_mm_nt(l["q"], l["k"]) for l in ld]
        kv = [_mm_tn(l["k"], l["e_w"] * va.astype(f32)) for l, va in zip(ld, v_aug)]
        c_old = [c_scr[i] for i in range(nu)]
        qc = [_mm(l["q"], cm) for l, cm in zip(ld, c_old)]
        ps, m_locs = [], []
        for (d, h), l, g in zip(units, ld, qk):
            d_log = jnp.where(incl[d], l["b_c"] - l["b_r"] + l["i_r"], NEG)
            m_loc = jnp.max(d_log, axis=-1, keepdims=True)
            ps.append(jnp.exp(d_log - m_loc) * (g * QK_SCALE))
            m_locs.append(m_loc)
        loc = [_mm(p, va) for p, va in zip(ps, v_aug)]
        m_old = [m_scr[i][0:1, 0:1] for i in range(nu)]
        inter = [l["b_c"] + m for l, m in zip(ld, m_old)]
        m_r = [jnp.maximum(a, b) for a, b in zip(inter, m_locs)]
        a_in = [jnp.exp(a - b) for a, b in zip(inter, m_r)]
        a_lo = [jnp.exp(a - b) for a, b in zip(m_locs, m_r)]
        floor = [jnp.exp(-b) for b in m_r]
        m_new = [jnp.maximum(l["b_last"] + m, l["m_st"]) for l, m in zip(ld, m_old)]
        s_old = [jnp.exp(l["b_last"] + m - mn) for l, m, mn in zip(ld, m_old, m_new)]
        s_new = [jnp.exp(l["m_st"] - mn) * QK_SCALE for l, mn in zip(ld, m_new)]
        for i, ((d, h), l) in enumerate(zip(units, ld)):
            o_ref = of_ref if d == 0 else ob_ref
            num = a_in[i] * qc[i][:, :HD] + a_lo[i] * loc[i][:, :HD]
            den = a_in[i] * qc[i][:, HD:HD + 1] + a_lo[i] * loc[i][:, HD:HD + 1]
            o_ref[l["rows"], h * HD:(h + 1) * HD] = num / jnp.maximum(jnp.abs(den), floor[i])
        for i in range(nu):
            c_scr[i] = s_old[i] * c_old[i] + s_new[i] * kv[i]
            m_scr[i] = jnp.broadcast_to(m_new[i], (8, HD))
        return carry

    lax.fori_loop(0, CPB, chunk, 0)


def _mlstm_scan(p, gcol, grow):
    t = p.shape[0]
    nblk = t // BLK
    w = N_HEADS_SCAN * HD
    nu = 2 * N_HEADS_SCAN
    fwd = lambda s: (s, 0)
    bwd = _bwd_block(nblk)
    fwd3 = lambda s: (s, 0, 0)
    bwd3 = lambda s: (jnp.where(s == 0, 0, nblk - s), 0, 0)

    def col(base, bwd_dir):
        cb = base // w
        if bwd_dir:
            return pl.BlockSpec((BLK, w), lambda s: (jnp.where(s == 0, 0, nblk - s), cb))
        return pl.BlockSpec((BLK, w), lambda s: (s, cb))

    return pl.pallas_call(
        _mlstm_kernel,
        grid=(nblk,),
        in_specs=[col(C_MLQ, False), col(C_MLK, False), col(C_MLV, False),
                  col(C_MLQ, True), col(C_MLK, True), col(C_MLV, True),
                  pl.BlockSpec((BLK, 128), fwd), pl.BlockSpec((BLK, 128), bwd),
                  pl.BlockSpec((CPB, 32, CHUNK), fwd3), pl.BlockSpec((CPB, 32, CHUNK), bwd3)],
        out_specs=[pl.BlockSpec((BLK, w), fwd), pl.BlockSpec((BLK, w), bwd)],
        out_shape=[jax.ShapeDtypeStruct((t, w), f32), jax.ShapeDtypeStruct((t, w), f32)],
        scratch_shapes=[pltpu.VMEM((nu, HD, 2 * HD), f32), pltpu.VMEM((nu, 8, HD), f32)],
        compiler_params=_cparams(("arbitrary",), VMEM_STREAM_MB),
        name="mlstm_scan",
    )(p, p, p, p, p, p, gcol, gcol, grow, grow)


def _rope_kernel(cos_ref, sin_ref):
    i = pl.program_id(0)
    r = lax.broadcasted_iota(jnp.int32, (BLK, HD), 0)
    lane = lax.broadcasted_iota(jnp.int32, (BLK, HD), 1)
    tok = (i - 1) * BLK + r
    pos = jnp.where(lane < HD // 2, jnp.right_shift(tok, 6), jnp.bitwise_and(tok, GRID_W - 1)).astype(f32)
    pair = jnp.bitwise_and(lane, HD // 4 - 1).astype(f32)
    inv_freq = jnp.exp(pair * (-jnp.log(ROPE_THETA) / (HD // 4)))
    ang = pos * inv_freq
    is_ctx = i == 0
    sin = jnp.where(is_ctx, 0.0, jnp.sin(ang))
    first = jnp.bitwise_and(lane, HD // 4) == 0
    cos_ref[...] = jnp.where(is_ctx, 1.0, jnp.cos(ang))
    sin_ref[...] = jnp.where(first, -sin, sin)


def _rope_tables(t):
    spec = pl.BlockSpec((BLK, HD), lambda i: (i, 0))
    return pl.pallas_call(
        _rope_kernel, grid=(t // BLK,), in_specs=[], out_specs=[spec, spec],
        out_shape=[jax.ShapeDtypeStruct((t, HD), f32)] * 2,
        compiler_params=_cparams(("parallel",)), name="rope_tables",
    )()


def _attn_prep_kernel(q_ref, k_ref, v_ref, qg_ref, kg_ref, cos_ref, sin_ref, qo_ref, ko_ref, vo_ref):
    lane = lax.broadcasted_iota(jnp.int32, (BLK, HD), 1)
    first = jnp.bitwise_and(lane, HD // 4) == 0
    cos = cos_ref[...]
    sin_signed = sin_ref[...]

    def norm_rope(x, g, scale):
        y = x * lax.rsqrt(jnp.mean(x * x, axis=-1, keepdims=True) + EPS) * g
        partner = jnp.where(first, pltpu.roll(y, HD - HD // 4, 1), pltpu.roll(y, HD // 4, 1))
        return (y * cos + partner * sin_signed) * scale

    for h in range(H_AT):
        x = q_ref[:, h * HD:(h + 1) * HD].astype(f32)
        qo_ref[:, h * HD:(h + 1) * HD] = norm_rope(x, qg_ref[...], QK_SCALE * LOG2E).astype(bf16)
    ones_col = jnp.where(lax.broadcasted_iota(jnp.int32, (BLK, HD), 1) == 0, 1.0, 0.0).astype(bf16)
    for h in range(H_KV):
        x = k_ref[:, h * HD:(h + 1) * HD].astype(f32)
        ko_ref[:, h * HD:(h + 1) * HD] = norm_rope(x, kg_ref[...], 1.0).astype(bf16)
        vo_ref[:, 2 * h * HD:(2 * h + 1) * HD] = v_ref[:, h * HD:(h + 1) * HD]
        vo_ref[:, (2 * h + 1) * HD:(2 * h + 2) * HD] = ones_col


def _attn_prep(p, q_g, k_g, rope):
    t = p.shape[0]
    wq, wk = H_AT * HD, H_KV * HD
    return pl.pallas_call(
        _attn_prep_kernel,
        grid=(t // BLK,),
        in_specs=[pl.BlockSpec((BLK, wq), lambda i: (i, C_ATQ // wq)),
                  pl.BlockSpec((BLK, wk), lambda i: (i, C_ATK // wk)),
                  pl.BlockSpec((BLK, wk), lambda i: (i, C_ATV // wk)),
                  pl.BlockSpec((1, HD), lambda i: (0, 0)),
                  pl.BlockSpec((1, HD), lambda i: (0, 0)),
                  pl.BlockSpec((BLK, HD), lambda i: (i, 0)),
                  pl.BlockSpec((BLK, HD), lambda i: (i, 0))],
        out_specs=[pl.BlockSpec((BLK, wq), lambda i: (i, 0)), pl.BlockSpec((BLK, wk), lambda i: (i, 0)),
                   pl.BlockSpec((BLK, 2 * wk), lambda i: (i, 0))],
        out_shape=[jax.ShapeDtypeStruct((t, wq), bf16), jax.ShapeDtypeStruct((t, wk), bf16),
                   jax.ShapeDtypeStruct((t, 2 * wk), bf16)],
        compiler_params=_cparams(("parallel",), VMEM_STREAM_MB),
        name="attn_prep",
    )(p, p, p, q_g.reshape(1, HD), k_g.reshape(1, HD), *rope)


def _attn_kernel(tq, tk, n_ctx_tiles, n_main, q_ref, k_ref, v_ref, o_ref, m_scr, acc_scr, sa_scr, sb_scr):
    qi = pl.program_id(1)
    grp = H_AT // H_KV
    m_scr[...] = jnp.full_like(m_scr, NEG)
    acc_scr[...] = jnp.zeros_like(acc_scr)

    def scores(rows):
        kt = k_ref[rows, :]
        return [_mm_nt(q_ref[:, h * HD:(h + 1) * HD], kt) for h in range(grp)]

    def softmax_pv(get_s, rows, width):
        va = v_ref[rows, :]
        ps = []
        for h in range(grp):
            mx = get_s(h, 0)
            for c in range(1, width // HD):
                mx = jnp.maximum(mx, get_s(h, c))
            m_prev = m_scr[h]
            m_new = jnp.maximum(m_prev, jnp.max(mx, axis=-1, keepdims=True))
            alpha = jnp.exp2(m_prev - m_new)
            p = jnp.concatenate([jnp.exp2(get_s(h, c) - m_new).astype(bf16) for c in range(width // HD)], axis=1)
            m_scr[h] = m_new
            ps.append((alpha, p))
        for h in range(grp):
            alpha, p = ps[h]
            acc = acc_scr[h]
            pv = jnp.dot(p, va, preferred_element_type=f32)
            acc_scr[h] = jnp.concatenate([alpha * acc[:, :HD], alpha * acc[:, HD:]], axis=1) + pv

    @pl.when(qi < n_ctx_tiles)
    def _():
        ctx_rows = pl.ds(0, BLK)
        ss = scores(ctx_rows)
        softmax_pv(lambda h, c: ss[h][:, c * HD:(c + 1) * HD], ctx_rows, BLK)

    def main_rows(j):
        return pl.ds(pl.multiple_of(j * tk, HD), tk)

    def store_scores(s_ref, j):
        for h, s in enumerate(scores(main_rows(j))):
            s_ref[h] = s

    def pipelined_step(cur_ref, nxt_ref, j):
        store_scores(nxt_ref, jnp.minimum(j + 1, n_main - 1))
        softmax_pv(lambda h, c: cur_ref[h, :, c * HD:(c + 1) * HD], main_rows(j), tk)

    @pl.when(qi >= n_ctx_tiles)
    def _():
        store_scores(sa_scr, 0)

        def body(i, carry):
            pipelined_step(sa_scr, sb_scr, 2 * i)
            pipelined_step(sb_scr, sa_scr, 2 * i + 1)
            return carry
        lax.fori_loop(0, n_main // 2, body, 0)
        if n_main % 2:
            pipelined_step(sa_scr, sb_scr, n_main - 1)

    for h in range(grp):
        acc = acc_scr[h]
        o_ref[:, h * HD:(h + 1) * HD] = (acc[:, :HD] / acc[:, HD:HD + 1]).astype(o_ref.dtype)


def _attention(qr, kr, va):
    t = qr.shape[0]
    tq = 256
    tk = _row_tile(t, (1280, 768, 256))
    grp = H_AT // H_KV
    wg = grp * HD
    kern = functools.partial(_attn_kernel, tq, tk, BLK // tq, t // tk)
    return pl.pallas_call(
        kern,
        grid=(H_KV, t // tq),
        in_specs=[pl.BlockSpec((tq, wg), lambda g, i: (i, g)),
                  pl.BlockSpec((t, HD), lambda g, i: (0, g)),
                  pl.BlockSpec((t, 2 * HD), lambda g, i: (0, g))],
        out_specs=pl.BlockSpec((tq, wg), lambda g, i: (i, g)),
        out_shape=jax.ShapeDtypeStruct((t, H_AT * HD), bf16),
        scratch_shapes=[pltpu.VMEM((grp, tq, HD), f32), pltpu.VMEM((grp, tq, 2 * HD), f32),
                        pltpu.VMEM((grp, tq, tk), f32), pltpu.VMEM((grp, tq, tk), f32)],
        compiler_params=_cparams(("parallel", "arbitrary"), VMEM_RESIDENT_MB),
        name="flash_attention",
    )(qr, kr, va)


def _merge_kernel(tm, split, dnf_ref, dnb_ref, z_ref, at_ref, mlf_ref, mlb_ref, og_ref, mod_ref,
                  dng_ref, mlg_ref, wo_ref, n2g_ref, rw1_ref, rw2_ref, *rest):
    i = pl.program_id(0)
    x_in = _stream_rows(i, rest[0], rest[1]) if split else rest[0][...]
    xo_ref, h2_ref, lg_ref = rest[-3:]
    rows = i * tm + lax.broadcasted_iota(jnp.int32, (tm, 1), 0)
    is_ctx = rows < BLK
    w = N_HEADS_SCAN * HD

    def head_norm(x, g):
        return x * lax.rsqrt(jnp.mean(x * x, axis=-1, keepdims=True) + EPS) * g

    acc = jnp.dot(at_ref[...], wo_ref[w:w + H_AT * HD, :], preferred_element_type=f32)
    dn_parts, ml_parts = [], []
    for h in range(N_HEADS_SCAN):
        sl = slice(h * HD, (h + 1) * HD)
        z = z_ref[:, sl].astype(f32)
        dn_parts.append(head_norm(dnf_ref[:, sl] + dnb_ref[:, sl], dng_ref[...]) * (z * _sigmoid(z)))
        ml_parts.append(head_norm(mlf_ref[:, sl] + mlb_ref[:, sl], mlg_ref[...]) * _sigmoid(og_ref[:, sl].astype(f32)))
    dn = jnp.concatenate(dn_parts, axis=1).astype(bf16)
    ml = jnp.concatenate(ml_parts, axis=1).astype(bf16)
    acc = acc + jnp.dot(dn, wo_ref[0:w, :], preferred_element_type=f32)
    acc = acc + jnp.dot(ml, wo_ref[w + H_AT * HD:, :], preferred_element_type=f32)
    x = x_in + _mod_rows(mod_ref, 2, is_ctx) * acc
    xo_ref[...] = x
    y = x * lax.rsqrt(jnp.mean(x * x, axis=-1, keepdims=True) + EPS) * n2g_ref[...]
    h2 = y * (1.0 + _mod_rows(mod_ref, 4, is_ctx)) + _mod_rows(mod_ref, 3, is_ctx)
    hh = h2.astype(bf16)
    hb = lax.bitcast_convert_type(hh.astype(f32), jnp.uint32)
    h2_ref[...] = jnp.bitwise_or(jnp.bitwise_and(hb[:, D // 2:], jnp.uint32(0xFFFF0000)),
                                 jnp.right_shift(hb[:, :D // 2], jnp.uint32(16)))
    hl = (h2 - hh.astype(f32)).astype(bf16)
    r = jnp.dot(jnp.concatenate([hh, hl], axis=0), jnp.concatenate([rw1_ref[...], rw2_ref[...]], axis=1),
                preferred_element_type=f32)
    lg_ref[...] = (r[:tm, :128] + r[tm:, :128]) + (r[:tm, 128:] + r[tm:, 128:])


def _merge_outproj(dnf, dnb, p, at, mlf, mlb, x, mod, dn_g, ml_g, w_out, n2g, rw1, rw2):
    tm = 256
    x_specs, x_args, t = _stream_specs(x, tm)
    w = N_HEADS_SCAN * HD
    row = lambda i: (i, 0)
    full = lambda i: (0, 0)
    sw = pl.BlockSpec((tm, w), row)
    return pl.pallas_call(
        functools.partial(_merge_kernel, tm, isinstance(x, tuple)),
        grid=(t // tm,),
        in_specs=[sw, sw, pl.BlockSpec((tm, w), lambda i: (i, C_DNZ // w)),
                  pl.BlockSpec((tm, H_AT * HD), row), sw, sw,
                  pl.BlockSpec((tm, w), lambda i: (i, C_MLO // w)),
                  pl.BlockSpec((8, 6 * D), full),
                  pl.BlockSpec((1, HD), full), pl.BlockSpec((1, HD), full),
                  pl.BlockSpec((D, D), full), pl.BlockSpec((1, D), full),
                  pl.BlockSpec((D, 128), full), pl.BlockSpec((D, 128), full)] + x_specs,
        out_specs=[pl.BlockSpec((tm, D), row), pl.BlockSpec((tm, D // 2), row), pl.BlockSpec((tm, 128), row)],
        out_shape=[jax.ShapeDtypeStruct((t, D), f32), jax.ShapeDtypeStruct((t, D // 2), jnp.uint32),
                   jax.ShapeDtypeStruct((t, 128), f32)],
        compiler_params=_cparams(("parallel",), VMEM_MATMUL_MB),
        name="merge_outproj",
    )(dnf, dnb, p, at, mlf, mlb, p, mod, dn_g.reshape(1, HD), ml_g.reshape(1, HD), w_out,
      n2g.reshape(1, D), rw1, rw2, *x_args)


def _route_kernel(lg_ref, bias_ref, e_ref, g_ref):
    lt = lg_ref[...].T
    sc = [_sigmoid(lt[e:e + 1, :]) for e in range(N_EXPERTS)]
    bi = [sc[e] + bias_ref[e:e + 1, 0:1] for e in range(N_EXPERTS)]
    n_groups = N_EXPERTS // EXPERTS_PER_GROUP
    best, best_g = None, None
    for g in range(n_groups):
        a, b, c, d = bi[4 * g:4 * g + 4]
        gs = jnp.maximum(jnp.maximum(jnp.maximum(a + b, a + c), jnp.maximum(a + d, b + c)),
                         jnp.maximum(b + d, c + d))
        if g == 0:
            best, best_g = gs, jnp.zeros_like(gs, dtype=jnp.int32)
        else:
            better = gs > best
            best = jnp.where(better, gs, best)
            best_g = jnp.where(better, g, best_g)
    t1 = jnp.full_like(best, -jnp.inf)
    t2 = jnp.full_like(best, -jnp.inf)
    i1 = jnp.zeros_like(best_g)
    i2 = jnp.zeros_like(best_g)
    s1 = jnp.zeros_like(best)
    s2 = jnp.zeros_like(best)
    for e in range(N_EXPERTS):
        v = jnp.where(best_g == e // EXPERTS_PER_GROUP, bi[e], -jnp.inf)
        gt1 = v > t1
        gt2 = jnp.logical_and(jnp.logical_not(gt1), v > t2)
        t2 = jnp.where(gt1, t1, jnp.where(gt2, v, t2))
        i2 = jnp.where(gt1, i1, jnp.where(gt2, e, i2))
        s2 = jnp.where(gt1, s1, jnp.where(gt2, sc[e], s2))
        t1 = jnp.where(gt1, v, t1)
        i1 = jnp.where(gt1, e, i1)
        s1 = jnp.where(gt1, sc[e], s1)
    tot = s1 + s2
    zi = jnp.zeros_like(i1)
    zf = jnp.zeros_like(s1)
    e_ref[...] = jnp.concatenate([i1, i2, zi, zi, zi, zi, zi, zi], axis=0)
    g_ref[...] = jnp.concatenate([s1 / tot, s2 / tot, zf, zf, zf, zf, zf, zf], axis=0)


def _route(logits, router_bias):
    t = logits.shape[0]
    tm = 256
    bias = jnp.zeros((N_EXPERTS, 128), f32).at[:, 0].set(router_bias)
    return pl.pallas_call(
        _route_kernel,
        grid=(t // tm,),
        in_specs=[pl.BlockSpec((tm, 128), lambda i: (i, 0)), pl.BlockSpec((N_EXPERTS, 128), lambda i: (0, 0))],
        out_specs=[pl.BlockSpec((8, tm), lambda i: (0, i)), pl.BlockSpec((8, tm), lambda i: (0, i))],
        out_shape=[jax.ShapeDtypeStruct((8, t), jnp.int32), jax.ShapeDtypeStruct((8, t), f32)],
        compiler_params=_cparams(("parallel",)),
        name="route_top2",
    )(logits, bias)


def _dispatch(e_rows, g_rows, n_blocks):
    t = e_rows.shape[1]
    n = 2 * t
    flat_e = e_rows[0:2].reshape(n)
    flat_w = g_rows[0:2].reshape(n)
    order = jnp.argsort(flat_e, stable=True).astype(jnp.int32)
    experts = jnp.arange(N_EXPERTS, dtype=jnp.int32)
    counts = jnp.sum(flat_e[:, None] == experts[None, :], axis=0).astype(jnp.int32)
    starts = jnp.cumsum(counts) - counts
    padded = (counts + MOE_BM - 1) // MOE_BM * MOE_BM
    p_ends = jnp.cumsum(padded)
    p_starts = p_ends - padded
    blk_start = jnp.arange(n_blocks, dtype=jnp.int32) * MOE_BM
    blk_e = jnp.minimum(jnp.sum(p_ends[None, :] <= blk_start[:, None], axis=1), N_EXPERTS - 1).astype(jnp.int32)
    sel = (blk_e[:, None] == experts[None, :]).astype(jnp.int32)
    pick = lambda v: jnp.sum(sel * v[None, :], axis=1)[:, None]
    rank = blk_start[:, None] + jnp.arange(MOE_BM, dtype=jnp.int32)[None, :] - pick(p_starts)
    valid = jnp.logical_and(rank >= 0, rank < pick(counts))
    src = order[jnp.clip(pick(starts) + rank, 0, n - 1)]
    tok = jnp.where(valid, jnp.where(src >= t, src - t, src), 0).astype(jnp.int32).reshape(n_blocks, 1, MOE_BM)
    dst = jnp.where(valid, src, -1).astype(jnp.int32).reshape(n_blocks, 1, MOE_BM)
    wt = jnp.where(valid, flat_w[src], 0.0).reshape(n_blocks, MOE_BM, 1)
    n_steps = n_blocks + 2
    tok_s = jnp.concatenate([tok, jnp.zeros((2, 1, MOE_BM), jnp.int32)], axis=0)
    e_s = jnp.concatenate([blk_e[:1], blk_e, blk_e[-1:]], axis=0)
    wt_s = jnp.concatenate([jnp.zeros((1, MOE_BM, 1), f32), wt, jnp.zeros((1, MOE_BM, 1), f32)], axis=0)
    dst_s = jnp.concatenate([jnp.full((2, 1, MOE_BM), -1, jnp.int32), dst], axis=0)
    is_pad = (dst_s < 0).reshape(-1)
    pad_rank = (jnp.cumsum(is_pad.astype(jnp.int32)) - 1).reshape(n_steps, 1, MOE_BM)
    dst_s = jnp.where(dst_s >= 0, dst_s, 2 * t + pad_rank)
    return e_s, tok_s, dst_s, wt_s


def _moe_kernel(n_steps, e_ref, tok_ref, dst_ref, wt_ref, h_hbm, wg_ref, wu_ref, wd_ref, out_hbm,
                xb0, xb1, yb0, yb1, sems):
    del e_ref
    s = pl.program_id(0)
    xbs, ybs = (xb0, xb1), (yb0, yb1)

    def wait_step_dmas():
        pltpu.make_async_copy(h_hbm.at[pl.ds(0, MOE_BM)], xb0, sems.at[0]).wait()
        pltpu.make_async_copy(yb0, out_hbm.at[pl.ds(0, MOE_BM)], sems.at[1]).wait()

    @pl.when(s == 0)
    def _():
        xb1[...] = jnp.zeros_like(xb1)
        yb1[...] = jnp.zeros_like(yb1)

    @pl.when(s > 0)
    def _():
        wait_step_dmas()

    def step(par):
        x_in, x_cur = xbs[par], xbs[1 - par]
        y_cur, y_out = ybs[par], ybs[1 - par]
        for r in range(MOE_BM):
            pltpu.make_async_copy(h_hbm.at[pl.ds(tok_ref[0, 0, r], 1)], x_in.at[pl.ds(r, 1)], sems.at[0]).start()
        w = x_cur[...]
        x_lo = lax.bitcast_convert_type(jnp.left_shift(w, jnp.uint32(16)), f32).astype(bf16)
        x_hi = lax.bitcast_convert_type(jnp.bitwise_and(w, jnp.uint32(0xFFFF0000)), f32).astype(bf16)
        dh = D // 2
        g = (jnp.dot(x_lo, wg_ref[0, :dh, :], preferred_element_type=f32)
             + jnp.dot(x_hi, wg_ref[0, dh:, :], preferred_element_type=f32))
        u = (jnp.dot(x_lo, wu_ref[0, :dh, :], preferred_element_type=f32)
             + jnp.dot(x_hi, wu_ref[0, dh:, :], preferred_element_type=f32))
        a = (g * _sigmoid(g) * u).astype(bf16)
        y_cur[...] = jnp.dot(a, wd_ref[0], preferred_element_type=f32) * wt_ref[0]
        for r in range(MOE_BM):
            pltpu.make_async_copy(y_out.at[pl.ds(r, 1)], out_hbm.at[pl.ds(dst_ref[0, 0, r], 1)], sems.at[1]).start()

    @pl.when(lax.rem(s, 2) == 0)
    def _():
        step(0)

    @pl.when(lax.rem(s, 2) == 1)
    def _():
        step(1)

    @pl.when(s == n_steps - 1)
    def _():
        wait_step_dmas()


def _moe(h2, e_s, tok_s, dst_s, wt_s, wg, wu, wd):
    n_steps = tok_s.shape[0]
    grid_spec = pltpu.PrefetchScalarGridSpec(
        num_scalar_prefetch=1,
        grid=(n_steps,),
        in_specs=[pl.BlockSpec((1, 1, MOE_BM), lambda s, e: (s, 0, 0), memory_space=pltpu.SMEM),
                  pl.BlockSpec((1, 1, MOE_BM), lambda s, e: (s, 0, 0), memory_space=pltpu.SMEM),
                  pl.BlockSpec((1, MOE_BM, 1), lambda s, e: (s, 0, 0)),
                  pl.BlockSpec(memory_space=pl.ANY),
                  pl.BlockSpec((1, D, D_EXPERT), lambda s, e: (e[s], 0, 0)),
                  pl.BlockSpec((1, D, D_EXPERT), lambda s, e: (e[s], 0, 0)),
                  pl.BlockSpec((1, D_EXPERT, D), lambda s, e: (e[s], 0, 0))],
        out_specs=pl.BlockSpec(memory_space=pl.ANY),
        scratch_shapes=[pltpu.VMEM((MOE_BM, D // 2), jnp.uint32)] * 2 + [pltpu.VMEM((MOE_BM, D), f32)] * 2
        + [pltpu.SemaphoreType.DMA((2,))],
    )
    return pl.pallas_call(
        functools.partial(_moe_kernel, n_steps),
        grid_spec=grid_spec,
        out_shape=jax.ShapeDtypeStruct((n_steps * MOE_BM, D), f32),
        compiler_params=_cparams(("arbitrary",), VMEM_RESIDENT_MB),
        name="moe_experts",
    )(e_s, tok_s, dst_s, wt_s, h2, wg, wu, wd)


def _final_kernel(x_ref, y0_ref, y1_ref, mod_ref, g_ref, o_ref):
    x = x_ref[...] + mod_ref[0:1, 5 * D:6 * D] * (y0_ref[...] + y1_ref[...])
    o_ref[...] = x * lax.rsqrt(jnp.mean(x * x, axis=-1, keepdims=True) + EPS) * g_ref[...]


def _final(x, moe, mod, g):
    t = x.shape[0]
    tm = 256
    nrow = t // tm
    nctx = BLK // tm
    return pl.pallas_call(
        _final_kernel,
        grid=(nrow - nctx,),
        in_specs=[pl.BlockSpec((tm, D), lambda i: (i + nctx, 0)),
                  pl.BlockSpec((tm, D), lambda i: (i + nctx, 0)),
                  pl.BlockSpec((tm, D), lambda i: (i + nctx + nrow, 0)),
                  pl.BlockSpec((8, 6 * D), lambda i: (0, 0)),
                  pl.BlockSpec((1, D), lambda i: (0, 0))],
        out_specs=pl.BlockSpec((tm, D), lambda i: (i, 0)),
        out_shape=jax.ShapeDtypeStruct((t - BLK, D), f32),
        compiler_params=_cparams(("parallel",), VMEM_STREAM_MB),
        name="final_norm",
    )(x, moe, moe, mod, g.reshape(1, D))


def _cast_kernel(x_ref, o_ref):
    o_ref[...] = x_ref[...].astype(o_ref.dtype)


def _to_bf16(w, layer):
    shape = w.shape[1:]
    w2 = w.reshape(-1, shape[-1])
    cols = shape[-1]
    rows = w2.shape[0] // w.shape[0]
    tr = _row_tile(rows, (1024, 512, 256))
    off = layer * (rows // tr)
    out = pl.pallas_call(
        _cast_kernel,
        grid=(rows // tr,),
        in_specs=[pl.BlockSpec((tr, cols), lambda i: (i + off, 0))],
        out_specs=pl.BlockSpec((tr, cols), lambda i: (i, 0)),
        out_shape=jax.ShapeDtypeStruct((rows, cols), bf16),
        compiler_params=_cparams(("parallel",), VMEM_MATMUL_MB),
        name="cast_bf16",
    )(w2)
    return out.reshape(shape)


def _prep_in_weights(w_in):
    splits = (1536, 512, 8, 8, 1024, 256, 256, 512, 512, 512, 512, 8, 8)
    offs = [0]
    for s in splits:
        offs.append(offs[-1] + s)
    part = lambda i: w_in[:, offs[i]:offs[i + 1]]
    main = jnp.concatenate([part(i) for i in (0, 1, 4, 5, 6, 7, 8, 9, 10)], axis=1).astype(bf16)
    gates = jnp.concatenate([part(i) for i in (2, 3, 11, 12)], axis=1)
    gates = jnp.pad(gates, ((0, 0), (0, 128 - gates.shape[1])))
    g1 = gates.astype(bf16)
    g2 = (gates - g1.astype(f32)).astype(bf16)
    return main, g1, g2


def _layer(l, rope, x, moe_prev, mods, norm1_g, norm2_g, w_in, dn_conv, dn_a_log, dn_dt_bias, dn_norm_g,
           q_norm_g, k_norm_g, ml_i_bias, ml_f_bias, ml_norm_g, w_out, rw1, rw2, router_bias,
           w_gate, w_up, w_down):
    w_main, wg1, wg2 = _prep_in_weights(w_in[l])
    x, h, graw = _norm1(x, moe_prev, mods[l - 1] if l else None, mods[l], norm1_g[l], wg1, wg2)
    t = h.shape[0]
    p = _matmul(h, w_main, bf16)
    conv_w = jnp.pad(dn_conv[l], ((0, 8 - CONV_W), (0, 0)))
    gate_params = jnp.zeros((8, 128), f32)
    gate_params = gate_params.at[0, 8:16].set(dn_dt_bias[l].reshape(8))
    gate_params = gate_params.at[0, 16:24].set(ml_i_bias[l].reshape(8))
    gate_params = gate_params.at[0, 24:32].set(ml_f_bias[l].reshape(8))
    gate_params = gate_params.at[1, 8:16].set(dn_a_log[l].reshape(8))
    dnq, gcol, grow = _scan_prep(p, graw, conv_w, gate_params)
    dnf, dnb = _gdn_scan(dnq, gcol, grow)
    mlf, mlb = _mlstm_scan(p, gcol, grow)
    qr, kr, va = _attn_prep(p, q_norm_g[l], k_norm_g[l], rope)
    at = _attention(qr, kr, va)
    x, h2, logits = _merge_outproj(dnf, dnb, p, at, mlf, mlb, x, mods[l], dn_norm_g[l], ml_norm_g[l],
                                   w_out[l].astype(bf16), norm2_g[l], rw1, rw2)
    e_rows, g_rows = _route(logits, router_bias)
    n_blocks = (2 * t + N_EXPERTS * (MOE_BM - 1) + MOE_BM - 1) // MOE_BM
    blk_e, tok, dst, wt = _dispatch(e_rows, g_rows, n_blocks)
    moe = _moe(h2, blk_e, tok, dst, wt, _to_bf16(w_gate, l), _to_bf16(w_up, l), _to_bf16(w_down, l))
    return x, moe


def kernel(x, c, ctx, c_ctx, w_mod, b_mod, norm1_g, norm2_g, w_in, dn_conv, dn_a_log, dn_dt_bias, dn_norm_g, q_norm_g, k_norm_g, ml_i_bias, ml_f_bias, ml_norm_g, w_out, router_w, router_bias, w_gate, w_up, w_down, final_norm_g):
    b, seq, d = x.shape
    assert b == 1 and d == D and ctx.shape[1] == BLK and seq % BLK == 0 and seq % GRID_W == 0
    depth = w_mod.shape[0]
    mods = _mods(c, c_ctx, w_mod, b_mod)
    xs = (ctx[0], x[0])
    rw = jnp.pad(router_w, ((0, 0), (0, 128 - N_EXPERTS)))
    rw1 = rw.astype(bf16)
    rw2 = (rw - rw1.astype(f32)).astype(bf16)
    moe = None
    rope = _rope_tables(BLK + seq)
    for l in range(depth):
        xs, moe = _layer(l, rope, xs, moe, mods, norm1_g, norm2_g, w_in, dn_conv, dn_a_log, dn_dt_bias,
                         dn_norm_g, q_norm_g, k_norm_g, ml_i_bias, ml_f_bias, ml_norm_g, w_out, rw1, rw2,
                         router_bias, w_gate, w_up, w_down)
    out = _final(xs, moe, mods[depth - 1], final_norm_g)
    return out.reshape(b, seq, d)
```

```python
import functools

import jax
import jax.numpy as jnp
from jax import lax
from jax.experimental import pallas as pl
from jax.experimental.pallas import tpu as pltpu

f32 = jnp.float32
bf16 = jnp.bfloat16

D = 2048
HD = 128
N_HEADS_SCAN = 4
H_AT = 8
H_KV = 2
CHUNK = 64
BLK = 256
CPB = BLK // CHUNK
GRID_W = 64
ROPE_THETA = 10000.0
QK_SCALE = HD ** -0.5
LOG2E = 1.4426950408889634
N_EXPERTS = 16
EXPERTS_PER_GROUP = 4
D_EXPERT = D // 2
MOE_BM = 256
EPS = 1e-6
NEG = -1e30
CONV_W = 5

C_DNQKV, C_DNZ, C_ATQ, C_ATK, C_ATV, C_MLQ, C_MLK, C_MLV, C_MLO, P_COLS = (
    0, 1536, 2048, 3072, 3328, 3584, 4096, 4608, 5120, 5632)

VMEM_STREAM_MB = 40
VMEM_MATMUL_MB = 48
VMEM_RESIDENT_MB = 52


def _cparams(sems, vmem_mb=None):
    return pltpu.CompilerParams(
        dimension_semantics=sems,
        vmem_limit_bytes=None if vmem_mb is None else vmem_mb << 20)


def _mm(a, b):
    return jnp.dot(a.astype(bf16), b.astype(bf16), preferred_element_type=f32)


def _mm_nt(a, b):
    return lax.dot_general(a.astype(bf16), b.astype(bf16), (((1,), (1,)), ((), ())),
                           preferred_element_type=f32)


def _mm_tn(a, b):
    return lax.dot_general(a.astype(bf16), b.astype(bf16), (((0,), (0,)), ((), ())),
                           preferred_element_type=f32)


def _split3(x):
    x1 = x.astype(bf16)
    r1 = x - x1.astype(f32)
    x2 = r1.astype(bf16)
    x3 = (r1 - x2.astype(f32)).astype(bf16)
    return x1, x2, x3


def _sigmoid(x):
    return 1.0 / (1.0 + jnp.exp(-x))


def _softplus(x):
    return jnp.maximum(x, 0.0) + jnp.log(1.0 + jnp.exp(-jnp.abs(x)))


def _mod_kernel(s_ref, w_ref, b_ref, o_ref):
    s = s_ref[...]
    s = s * _sigmoid(s)
    o_ref[0] = jnp.dot(s, w_ref[0], preferred_element_type=f32,
                       precision=lax.Precision.HIGHEST) + b_ref[0]


def _mods(c, c_ctx, w_mod, b_mod):
    depth, d, n6 = w_mod.shape
    s = jnp.zeros((8, d), f32).at[0].set(c[0]).at[1].set(c_ctx)
    tn = 1024
    return pl.pallas_call(
        _mod_kernel,
        grid=(depth, n6 // tn),
        in_specs=[pl.BlockSpec((8, d), lambda l, j: (0, 0)),
                  pl.BlockSpec((1, d, tn), lambda l, j: (l, 0, j)),
                  pl.BlockSpec((1, 1, tn), lambda l, j: (l, 0, j))],
        out_specs=pl.BlockSpec((1, 8, tn), lambda l, j: (l, 0, j)),
        out_shape=jax.ShapeDtypeStruct((depth, 8, n6), f32),
        compiler_params=_cparams(("parallel", "parallel"), VMEM_STREAM_MB),
        name="mod_vectors",
    )(s, w_mod, b_mod.reshape(depth, 1, n6))


def _mod_rows(mod_ref, k, is_ctx):
    lat = mod_ref[0:1, k * D:(k + 1) * D]
    ctx = mod_ref[1:2, k * D:(k + 1) * D]
    return jnp.where(is_ctx, ctx, lat)


def _stream_rows(i, ctx_ref, x_ref):
    return jnp.where(i == 0, ctx_ref[...], x_ref[...])


def _stream_specs(x, tm):
    if isinstance(x, tuple):
        assert tm == BLK
        ctx2, lat2 = x
        specs = [pl.BlockSpec((BLK, D), lambda i: (0, 0)),
                 pl.BlockSpec((tm, D), lambda i: (jnp.maximum(i - 1, 0), 0))]
        return specs, [ctx2, lat2], ctx2.shape[0] + lat2.shape[0]
    return [pl.BlockSpec((tm, D), lambda i: (i, 0))], [x], x.shape[0]


def _norm1_kernel(has_moe, split, tm, *refs):
    i = pl.program_id(0)
    if split:
        x = _stream_rows(i, refs[0], refs[1])
        refs = refs[1:]
    else:
        x = refs[0][...]
    if has_moe:
        _, y0_ref, y1_ref, modp_ref, mod_ref, g_ref, wg1_ref, wg2_ref, xo_ref, h_ref, gr_ref = refs
    else:
        _, mod_ref, g_ref, wg1_ref, wg2_ref, h_ref, gr_ref = refs
    rows = i * tm + lax.broadcasted_iota(jnp.int32, (tm, 1), 0)
    is_ctx = rows < BLK
    if has_moe:
        x = x + _mod_rows(modp_ref, 5, is_ctx) * (y0_ref[...] + y1_ref[...])
        xo_ref[...] = x
    ms = jnp.mean(x * x, axis=-1, keepdims=True)
    y = x * lax.rsqrt(ms + EPS) * g_ref[...]
    h = y * (1.0 + _mod_rows(mod_ref, 1, is_ctx)) + _mod_rows(mod_ref, 0, is_ctx)
    hh = h.astype(bf16)
    h_ref[...] = hh
    hl = (h - hh.astype(f32)).astype(bf16)
    wg1 = wg1_ref[...]
    gr_ref[...] = (jnp.dot(hh, wg1, preferred_element_type=f32)
                   + jnp.dot(hl, wg1, preferred_element_type=f32)
                   + jnp.dot(hh, wg2_ref[...], preferred_element_type=f32))


def _norm1(x, moe, mod_prev, mod_cur, g, wg1, wg2):
    tm = 256
    in_specs, args, t = _stream_specs(x, tm)
    nrow = t // tm
    row = lambda i: (i, 0)
    full = lambda i: (0, 0)
    if moe is not None:
        in_specs += [pl.BlockSpec((tm, D), row), pl.BlockSpec((tm, D), lambda i: (i + nrow, 0)),
                     pl.BlockSpec((8, 6 * D), full)]
        args += [moe, moe, mod_prev]
    in_specs += [pl.BlockSpec((8, 6 * D), full), pl.BlockSpec((1, D), full),
                 pl.BlockSpec((D, 128), full), pl.BlockSpec((D, 128), full)]
    args += [mod_cur, g.reshape(1, D), wg1, wg2]
    out_specs = [pl.BlockSpec((tm, D), row), pl.BlockSpec((tm, 128), row)]
    out_shape = [jax.ShapeDtypeStruct((t, D), bf16), jax.ShapeDtypeStruct((t, 128), f32)]
    if moe is not None:
        out_specs = [pl.BlockSpec((tm, D), row)] + out_specs
        out_shape = [jax.ShapeDtypeStruct((t, D), f32)] + out_shape
    outs = pl.pallas_call(
        functools.partial(_norm1_kernel, moe is not None, isinstance(x, tuple), tm),
        grid=(nrow,), in_specs=in_specs, out_specs=out_specs, out_shape=out_shape,
        compiler_params=_cparams(("parallel",), VMEM_STREAM_MB),
        name="norm1_modulate",
    )(*args)
    if moe is not None:
        return outs
    return [x] + list(outs)


def _matmul_kernel(a_ref, b_ref, o_ref):
    o_ref[...] = jnp.dot(a_ref[...], b_ref[...], preferred_element_type=f32).astype(o_ref.dtype)


def _row_tile(t, choices):
    for c in choices:
        if t % c == 0:
            return c
    raise ValueError(f"no row tile for {t}")


def _matmul(a, b, out_dtype):
    m, k = a.shape
    n = b.shape[1]
    tm = _row_tile(m, (1280, 768, 512, 256))
    tn = _row_tile(n, (1408, 1024, 512))
    return pl.pallas_call(
        _matmul_kernel,
        grid=(n // tn, m // tm),
        in_specs=[pl.BlockSpec((tm, k), lambda j, i: (i, 0)),
                  pl.BlockSpec((k, tn), lambda j, i: (0, j))],
        out_specs=pl.BlockSpec((tm, tn), lambda j, i: (i, j)),
        out_shape=jax.ShapeDtypeStruct((m, n), out_dtype),
        compiler_params=_cparams(("parallel", "parallel"), VMEM_MATMUL_MB),
        name="in_projection",
    )(a, b)


def _scan_prep_kernel(nblk, cur_ref, prev_ref, next_ref, cw_ref, graw_ref, gp_ref, q_ref, gc_ref, grow_ref, xs):
    i = pl.program_id(0)
    has_prev = i >= 2
    has_next = jnp.logical_and(i >= 1, i < nblk - 1)
    xs[0:16, :] = jnp.where(has_prev, prev_ref[...].astype(f32), 0.0)
    xs[16:16 + BLK, :] = cur_ref[...].astype(f32)
    xs[16 + BLK:32 + BLK, :] = jnp.where(has_next, next_ref[...].astype(f32), 0.0)
    acc = cw_ref[0:1, :] * xs[pl.ds(16 - CONV_W // 2, BLK), :]
    for j in range(1, CONV_W):
        acc = acc + cw_ref[j:j + 1, :] * xs[pl.ds(16 - CONV_W // 2 + j, BLK), :]
    a = acc * _sigmoid(acc)
    w = N_HEADS_SCAN * HD
    for h in range(2 * N_HEADS_SCAN):
        xh = a[:, h * HD:(h + 1) * HD]
        inv = lax.rsqrt(jnp.sum(xh * xh, axis=-1, keepdims=True) + EPS)
        scale = QK_SCALE if h < N_HEADS_SCAN else 1.0
        q_ref[:, h * HD:(h + 1) * HD] = (xh * (inv * scale)).astype(bf16)
    q_ref[:, 2 * w:3 * w] = a[:, 2 * w:3 * w].astype(bf16)

    z = graw_ref[...] + gp_ref[0:1, :]
    lane = lax.broadcasted_iota(jnp.int32, (BLK, 128), 1)
    sp = _softplus(z)
    vals = jnp.where(lane < 8, _sigmoid(z),
                     jnp.where(lane < 16, -jnp.exp(gp_ref[1:2, :]) * sp,
                               jnp.where(lane < 24, z, z - sp)))
    r = lax.broadcasted_iota(jnp.int32, (BLK, BLK), 0)
    c = lax.broadcasted_iota(jnp.int32, (BLK, BLK), 1)
    same = jnp.right_shift(r, 6) == jnp.right_shift(c, 6)
    tri_lo = jnp.where(jnp.logical_and(same, r >= c), 1.0, 0.0).astype(bf16)
    tri_up = jnp.where(jnp.logical_and(same, r <= c), 1.0, 0.0).astype(bf16)
    v1, v2, v3 = _split3(vals)
    dot = functools.partial(jnp.dot, preferred_element_type=f32)
    prefix = dot(tri_lo, v1) + dot(tri_lo, v2) + dot(tri_lo, v3)
    suffix = dot(tri_up, v1) + dot(tri_up, v2) + dot(tri_up, v3)
    is_cum = jnp.logical_and(jnp.bitwise_and(lane, 8) == 8, lane < 32)
    is_bwd = jnp.bitwise_and(lane, 4) == 4
    out = jnp.where(is_cum, jnp.where(is_bwd, suffix, prefix), vals)
    gc_ref[...] = out
    gt = out.T
    for cc in range(CPB):
        grow_ref[cc] = gt[0:32, cc * CHUNK:(cc + 1) * CHUNK]


def _scan_prep(p, graw, conv_w, gate_params):
    t = p.shape[0]
    nblk = t // BLK
    wq = 3 * N_HEADS_SCAN * HD
    n16 = t // 16
    return pl.pallas_call(
        functools.partial(_scan_prep_kernel, nblk),
        grid=(nblk,),
        in_specs=[pl.BlockSpec((BLK, wq), lambda i: (i, 0)),
                  pl.BlockSpec((16, wq), lambda i: (jnp.maximum(i * (BLK // 16) - 1, 0), 0)),
                  pl.BlockSpec((16, wq), lambda i: (jnp.minimum((i + 1) * (BLK // 16), n16 - 1), 0)),
                  pl.BlockSpec((8, wq), lambda i: (0, 0)),
                  pl.BlockSpec((BLK, 128), lambda i: (i, 0)),
                  pl.BlockSpec((8, 128), lambda i: (0, 0))],
        out_specs=[pl.BlockSpec((BLK, wq), lambda i: (i, 0)),
                   pl.BlockSpec((BLK, 128), lambda i: (i, 0)),
                   pl.BlockSpec((CPB, 32, CHUNK), lambda i: (i, 0, 0))],
        out_shape=[jax.ShapeDtypeStruct((t, wq), bf16),
                   jax.ShapeDtypeStruct((t, 128), f32),
                   jax.ShapeDtypeStruct((t // CHUNK, 32, CHUNK), f32)],
        scratch_shapes=[pltpu.VMEM((BLK + 32, wq), f32)],
        compiler_params=_cparams(("parallel",), VMEM_STREAM_MB),
        name="scan_prep",
    )(p, p, p, conv_w, graw, gate_params)


def _tri_masks():
    r = lax.broadcasted_iota(jnp.int32, (CHUNK, CHUNK), 0)
    c = lax.broadcasted_iota(jnp.int32, (CHUNK, CHUNK), 1)
    blk = jnp.right_shift(r, 4) == jnp.right_shift(c, 4)
    eye = jnp.where(r == c, 1.0, 0.0)
    return (r >= c, r <= c), (r > c, r < c), blk, eye


def _gdn_kernel(qf_ref, qb_ref, gcf_ref, gcb_ref, grf_ref, grb_ref, of_ref, ob_ref, s_scr):
    @pl.when(pl.program_id(0) == 0)
    def _():
        s_scr[...] = jnp.zeros_like(s_scr)

    incl, strict, blk, eye = _tri_masks()
    w = N_HEADS_SCAN * HD

    units = [(d, h) for d in range(2) for h in range(N_HEADS_SCAN)]
    rows_cat = lambda a, b: jnp.concatenate([a, b], axis=0)
    cols_cat = lambda a, b: jnp.concatenate([a, b], axis=1)
    C = CHUNK

    def chunk_pair(pp, carry):
        ld, tags = [], []
        for off in range(2):
            for d, h in units:
                cc = 2 * pp + off
                c = cc if d == 0 else CPB - 1 - cc
                q_ref, gc_ref, gr_ref = (qf_ref, gcf_ref, grf_ref) if d == 0 else (qb_ref, gcb_ref, grb_ref)
                rows = pl.ds(pl.multiple_of(c * C, C), C)
                u_idx = d * N_HEADS_SCAN + h
                q = q_ref[rows, h * HD:(h + 1) * HD]
                k = q_ref[rows, w + h * HD:w + (h + 1) * HD]
                v = q_ref[rows, 2 * w + h * HD:2 * w + (h + 1) * HD]
                beta = gc_ref[rows, u_idx:u_idx + 1]
                cum_c = gc_ref[rows, 8 + u_idx:9 + u_idx]
                cum_r = gr_ref[c][8 + u_idx:9 + u_idx, :]
                tot = cum_c[C - 1:C, :] if d == 0 else cum_c[0:1, :]
                ld.append((rows, q, k, v, beta, cum_c, cum_r, tot))
                tags.append((d, h))
        g1 = [_mm_nt(rows_cat(k, q), k) for (_, q, k, *_) in ld]
        st = []
        for (d, h), (rows, q, k, v, beta, cum_c, cum_r, tot), g in zip(tags, ld, g1):
            decay = jnp.exp(jnp.where(incl[d], cum_c - cum_r, NEG))
            nm = jnp.where(strict[d], beta * g[:C] * decay, 0.0)
            dm = jnp.where(blk, nm, 0.0)
            kf = k.astype(f32)
            e_c = jnp.exp(cum_c)
            rhs = cols_cat(cols_cat((beta * e_c) * kf, beta * v.astype(f32)), nm - dm)
            st.append(dict(dm=dm, rhs=rhs, qk=g[C:] * decay, k_dec=kf * jnp.exp(tot - cum_c),
                           q_dec=q.astype(f32) * e_c, g_last=jnp.exp(tot), p1=eye - dm))
        m2 = [_mm(s["dm"], s["dm"]) for s in st]
        r = [_mm(rows_cat(s["p1"], m), m) for s, m in zip(st, m2)]
        p2 = [s["p1"] + x[:C] for s, x in zip(st, r)]
        m4 = [x[C:] for x in r]
        r = [_mm(rows_cat(p, m), m) for p, m in zip(p2, m4)]
        p3 = [p + x[:C] for p, x in zip(p2, r)]
        m8 = [x[C:] for x in r]
        dinv = [p + _mm(p, m) for p, m in zip(p3, m8)]
        r = [_mm(di, s["rhs"]) for di, s in zip(dinv, st)]
        t1 = [x[:, :2 * HD] for x in r]
        qm = [x[:, 2 * HD:] for x in r]
        r = [_mm(qq, cols_cat(t, qq)) for qq, t in zip(qm, t1)]
        a1 = [x[:, :2 * HD] for x in r]
        qm2 = [x[:, 2 * HD:] for x in r]
        b2 = [_mm(q2, t) for q2, t in zip(qm2, t1)]
        c3 = [_mm(qq, b) for qq, b in zip(qm, b2)]
        sol = [t - a + b - c for t, a, b, c in zip(t1, a1, b2, c3)]
        nu = len(units)
        for off in range(2):
            sl = slice(off * nu, (off + 1) * nu)
            s_old = [s_scr[i] for i in range(nu)]
            r = [_mm(rows_cat(x[:, :HD], s["q_dec"]), so) for x, s, so in zip(sol[sl], st[sl], s_old)]
            u = [x[:, HD:] - y[:C] for x, y in zip(sol[sl], r)]
            o_intra = [_mm(s["qk"], uu) for s, uu in zip(st[sl], u)]
            s_add = [_mm_tn(s["k_dec"], uu) for s, uu in zip(st[sl], u)]
            for i, ((d, h), l, s) in enumerate(zip(units, ld[sl], st[sl])):
                o_ref = of_ref if d == 0 else ob_ref
                o_ref[l[0], h * HD:(h + 1) * HD] = r[i][C:] + o_intra[i]
                s_scr[i] = s["g_last"] * s_old[i] + s_add[i]
        return carry

    lax.fori_loop(0, CPB // 2, chunk_pair, 0)


def _bwd_block(nblk):
    return lambda s: (jnp.where(s == 0, 0, nblk - s), 0)


def _gdn_scan(qkv, gcol, grow):
    t = qkv.shape[0]
    nblk = t // BLK
    wq = 3 * N_HEADS_SCAN * HD
    w = N_HEADS_SCAN * HD
    fwd = lambda s: (s, 0)
    bwd = _bwd_block(nblk)
    fwd3 = lambda s: (s, 0, 0)
    bwd3 = lambda s: (jnp.where(s == 0, 0, nblk - s), 0, 0)
    return pl.pallas_call(
        _gdn_kernel,
        grid=(nblk,),
        in_specs=[pl.BlockSpec((BLK, wq), fwd), pl.BlockSpec((BLK, wq), bwd),
                  pl.BlockSpec((BLK, 128), fwd), pl.BlockSpec((BLK, 128), bwd),
                  pl.BlockSpec((CPB, 32, CHUNK), fwd3), pl.BlockSpec((CPB, 32, CHUNK), bwd3)],
        out_specs=[pl.BlockSpec((BLK, w), fwd), pl.BlockSpec((BLK, w), bwd)],
        out_shape=[jax.ShapeDtypeStruct((t, w), f32), jax.ShapeDtypeStruct((t, w), f32)],
        scratch_shapes=[pltpu.VMEM((2 * N_HEADS_SCAN, HD, HD), f32)],
        compiler_params=_cparams(("arbitrary",), VMEM_STREAM_MB),
        name="gdn_scan",
    )(qkv, qkv, gcol, gcol, grow, grow)


def _mlstm_kernel(pf_q, pf_k, pf_v, pb_q, pb_k, pb_v, gcf_ref, gcb_ref, grf_ref, grb_ref,
                  of_ref, ob_ref, c_scr, m_scr):
    @pl.when(pl.program_id(0) == 0)
    def _():
        c_scr[...] = jnp.zeros_like(c_scr)
        m_scr[...] = jnp.full_like(m_scr, NEG)

    incl, _, _, _ = _tri_masks()
    ones_col = jnp.where(lax.broadcasted_iota(jnp.int32, (CHUNK, HD), 1) == 0, 1.0, 0.0).astype(bf16)

    units = [(d, h) for d in range(2) for h in range(N_HEADS_SCAN)]

    def chunk(cc, carry):
        ld = []
        for d, h in units:
            c = cc if d == 0 else CPB - 1 - cc
            q_ref, k_ref, v_ref, gc_ref, gr_ref = (
                (pf_q, pf_k, pf_v, gcf_ref, grf_ref) if d == 0 else (pb_q, pb_k, pb_v, gcb_ref, grb_ref))
            rows = pl.ds(pl.multiple_of(c * CHUNK, CHUNK), CHUNK)
            u_idx = d * N_HEADS_SCAN + h
            q = q_ref[rows, h * HD:(h + 1) * HD]
            k = k_ref[rows, h * HD:(h + 1) * HD]
            v = v_ref[rows, h * HD:(h + 1) * HD]
            i_c = gc_ref[rows, 16 + u_idx:17 + u_idx]
            b_c = gc_ref[rows, 24 + u_idx:25 + u_idx]
            grow = gr_ref[c]
            i_r = grow[16 + u_idx:17 + u_idx, :]
            b_r = grow[24 + u_idx:25 + u_idx, :]
            b_last = b_c[CHUNK - 1:CHUNK, :] if d == 0 else b_c[0:1, :]
            w_log = b_last - b_c + i_c
            m_st = jnp.max(w_log, axis=0, keepdims=True)
            e_w = jnp.exp(w_log - m_st)
            ld.append(dict(rows=rows, q=q, k=k, v=v, i_r=i_r, b_c=b_c, b_r=b_r, b_last=b_last, m_st=m_st, e_w=e_w))
        nu = len(units)
        v_aug = [jnp.concatenate([l["v"], ones_col], axis=1) for l in ld]
        qk = [_mm_nt(l["q"], l["k"]) for l in ld]
        kv = [_mm_tn(l["k"], l["e_w"] * va.astype(f32)) for l, va in zip(ld, v_aug)]
        c_old = [c_scr[i] for i in range(nu)]
        qc = [_mm(l["q"], cm) for l, cm in zip(ld, c_old)]
        ps, m_locs = [], []
        for (d, h), l, g in zip(units, ld, qk):
            d_log = jnp.where(incl[d], l["b_c"] - l["b_r"] + l["i_r"], NEG)
            m_loc = jnp.max(d_log, axis=-1, keepdims=True)
            ps.append(jnp.exp(d_log - m_loc) * (g * QK_SCALE))
            m_locs.append(m_loc)
        loc = [_mm(p, va) for p, va in zip(ps, v_aug)]
        m_old = [m_scr[i][0:1, 0:1] for i in range(nu)]
        inter = [l["b_c"] + m for l, m in zip(ld, m_old)]
        m_r = [jnp.maximum(a, b) for a, b in zip(inter, m_locs)]
        a_in = [jnp.exp(a - b) for a, b in zip(inter, m_r)]
        a_lo = [jnp.exp(a - b) for a, b in zip(m_locs, m_r)]
        floor = [jnp.exp(-b) for b in m_r]
        m_new = [jnp.maximum(l["b_last"] + m, l["m_st"]) for l, m in zip(ld, m_old)]
        s_old = [jnp.exp(l["b_last"] + m - mn) for l, m, mn in zip(ld, m_old, m_new)]
        s_new = [jnp.exp(l["m_st"] - mn) * QK_SCALE for l, mn in zip(ld, m_new)]
        for i, ((d, h), l) in enumerate(zip(units, ld)):
            o_ref = of_ref if d == 0 else ob_ref
            num = a_in[i] * qc[i][:, :HD] + a_lo[i] * loc[i][:, :HD]
            den = a_in[i] * qc[i][:, HD:HD + 1] + a_lo[i] * loc[i][:, HD:HD + 1]
            o_ref[l["rows"], h * HD:(h + 1) * HD] = num / jnp.maximum(jnp.abs(den), floor[i])
        for i in range(nu):
            c_scr[i] = s_old[i] * c_old[i] + s_new[i] * kv[i]
            m_scr[i] = jnp.broadcast_to(m_new[i], (8, HD))
        return carry

    lax.fori_loop(0, CPB, chunk, 0)


def _mlstm_scan(p, gcol, grow):
    t = p.shape[0]
    nblk = t // BLK
    w = N_HEADS_SCAN * HD
    nu = 2 * N_HEADS_SCAN
    fwd = lambda s: (s, 0)
    bwd = _bwd_block(nblk)
    fwd3 = lambda s: (s, 0, 0)
    bwd3 = lambda s: (jnp.where(s == 0, 0, nblk - s), 0, 0)

    def col(base, bwd_dir):
        cb = base // w
        if bwd_dir:
            return pl.BlockSpec((BLK, w), lambda s: (jnp.where(s == 0, 0, nblk - s), cb))
        return pl.BlockSpec((BLK, w), lambda s: (s, cb))

    return pl.pallas_call(
        _mlstm_kernel,
        grid=(nblk,),
        in_specs=[col(C_MLQ, False), col(C_MLK, False), col(C_MLV, False),
                  col(C_MLQ, True), col(C_MLK, True), col(C_MLV, True),
                  pl.BlockSpec((BLK, 128), fwd), pl.BlockSpec((BLK, 128), bwd),
                  pl.BlockSpec((CPB, 32, CHUNK), fwd3), pl.BlockSpec((CPB, 32, CHUNK), bwd3)],
        out_specs=[pl.BlockSpec((BLK, w), fwd), pl.BlockSpec((BLK, w), bwd)],
        out_shape=[jax.ShapeDtypeStruct((t, w), f32), jax.ShapeDtypeStruct((t, w), f32)],
        scratch_shapes=[pltpu.VMEM((nu, HD, 2 * HD), f32), pltpu.VMEM((nu, 8, HD), f32)],
        compiler_params=_cparams(("arbitrary",), VMEM_STREAM_MB),
        name="mlstm_scan",
    )(p, p, p, p, p, p, gcol, gcol, grow, grow)


def _rope_kernel(cos_ref, sin_ref):
    i = pl.program_id(0)
    r = lax.broadcasted_iota(jnp.int32, (BLK, HD), 0)
    lane = lax.broadcasted_iota(jnp.int32, (BLK, HD), 1)
    tok = (i - 1) * BLK + r
    pos = jnp.where(lane < HD // 2, jnp.right_shift(tok, 6), jnp.bitwise_and(tok, GRID_W - 1)).astype(f32)
    pair = jnp.bitwise_and(lane, HD // 4 - 1).astype(f32)
    inv_freq = jnp.exp(pair * (-jnp.log(ROPE_THETA) / (HD // 4)))
    ang = pos * inv_freq
    is_ctx = i == 0
    sin = jnp.where(is_ctx, 0.0, jnp.sin(ang))
    first = jnp.bitwise_and(lane, HD // 4) == 0
    cos_ref[...] = jnp.where(is_ctx, 1.0, jnp.cos(ang))
    sin_ref[...] = jnp.where(first, -sin, sin)


def _rope_tables(t):
    spec = pl.BlockSpec((BLK, HD), lambda i: (i, 0))
    return pl.pallas_call(
        _rope_kernel, grid=(t // BLK,), in_specs=[], out_specs=[spec, spec],
        out_shape=[jax.ShapeDtypeStruct((t, HD), f32)] * 2,
        compiler_params=_cparams(("parallel",)), name="rope_tables",
    )()


def _attn_prep_kernel(q_ref, k_ref, v_ref, qg_ref, kg_ref, cos_ref, sin_ref, qo_ref, ko_ref, vo_ref):
    lane = lax.broadcasted_iota(jnp.int32, (BLK, HD), 1)
    first = jnp.bitwise_and(lane, HD // 4) == 0
    cos = cos_ref[...]
    sin_signed = sin_ref[...]

    def norm_rope(x, g, scale):
        y = x * lax.rsqrt(jnp.mean(x * x, axis=-1, keepdims=True) + EPS) * g
        partner = jnp.where(first, pltpu.roll(y, HD - HD // 4, 1), pltpu.roll(y, HD // 4, 1))
        return (y * cos + partner * sin_signed) * scale

    for h in range(H_AT):
        x = q_ref[:, h * HD:(h + 1) * HD].astype(f32)
        qo_ref[:, h * HD:(h + 1) * HD] = norm_rope(x, qg_ref[...], QK_SCALE * LOG2E).astype(bf16)
    ones_col = jnp.where(lax.broadcasted_iota(jnp.int32, (BLK, HD), 1) == 0, 1.0, 0.0).astype(bf16)
    for h in range(H_KV):
        x = k_ref[:, h * HD:(h + 1) * HD].astype(f32)
        ko_ref[:, h * HD:(h + 1) * HD] = norm_rope(x, kg_ref[...], 1.0).astype(bf16)
        vo_ref[:, 2 * h * HD:(2 * h + 1) * HD] = v_ref[:, h * HD:(h + 1) * HD]
        vo_ref[:, (2 * h + 1) * HD:(2 * h + 2) * HD] = ones_col


def _attn_prep(p, q_g, k_g, rope):
    t = p.shape[0]
    wq, wk = H_AT * HD, H_KV * HD
    return pl.pallas_call(
        _attn_prep_kernel,
        grid=(t // BLK,),
        in_specs=[pl.BlockSpec((BLK, wq), lambda i: (i, C_ATQ // wq)),
                  pl.BlockSpec((BLK, wk), lambda i: (i, C_ATK // wk)),
                  pl.BlockSpec((BLK, wk), lambda i: (i, C_ATV // wk)),
                  pl.BlockSpec((1, HD), lambda i: (0, 0)),
                  pl.BlockSpec((1, HD), lambda i: (0, 0)),
                  pl.BlockSpec((BLK, HD), lambda i: (i, 0)),
                  pl.BlockSpec((BLK, HD), lambda i: (i, 0))],
        out_specs=[pl.BlockSpec((BLK, wq), lambda i: (i, 0)), pl.BlockSpec((BLK, wk), lambda i: (i, 0)),
                   pl.BlockSpec((BLK, 2 * wk), lambda i: (i, 0))],
        out_shape=[jax.ShapeDtypeStruct((t, wq), bf16), jax.ShapeDtypeStruct((t, wk), bf16),
                   jax.ShapeDtypeStruct((t, 2 * wk), bf16)],
        compiler_params=_cparams(("parallel",), VMEM_STREAM_MB),
        name="attn_prep",
    )(p, p, p, q_g.reshape(1, HD), k_g.reshape(1, HD), *rope)


def _attn_kernel(tq, tk, n_ctx_tiles, n_main, q_ref, k_ref, v_ref, o_ref, m_scr, acc_scr, sa_scr, sb_scr):
    qi = pl.program_id(1)
    grp = H_AT // H_KV
    m_scr[...] = jnp.full_like(m_scr, NEG)
    acc_scr[...] = jnp.zeros_like(acc_scr)

    def scores(rows):
        kt = k_ref[rows, :]
        return [_mm_nt(q_ref[:, h * HD:(h + 1) * HD], kt) for h in range(grp)]

    def softmax_pv(get_s, rows, width):
        va = v_ref[rows, :]
        ps = []
        for h in range(grp):
            mx = get_s(h, 0)
            for c in range(1, width // HD):
                mx = jnp.maximum(mx, get_s(h, c))
            m_prev = m_scr[h]
            m_new = jnp.maximum(m_prev, jnp.max(mx, axis=-1, keepdims=True))
            alpha = jnp.exp2(m_prev - m_new)
            p = jnp.concatenate([jnp.exp2(get_s(h, c) - m_new).astype(bf16) for c in range(width // HD)], axis=1)
            m_scr[h] = m_new
            ps.append((alpha, p))
        for h in range(grp):
            alpha, p = ps[h]
            acc = acc_scr[h]
            pv = jnp.dot(p, va, preferred_element_type=f32)
            acc_scr[h] = jnp.concatenate([alpha * acc[:, :HD], alpha * acc[:, HD:]], axis=1) + pv

    @pl.when(qi < n_ctx_tiles)
    def _():
        ctx_rows = pl.ds(0, BLK)
        ss = scores(ctx_rows)
        softmax_pv(lambda h, c: ss[h][:, c * HD:(c + 1) * HD], ctx_rows, BLK)

    def main_rows(j):
        return pl.ds(pl.multiple_of(j * tk, HD), tk)

    def store_scores(s_ref, j):
        for h, s in enumerate(scores(main_rows(j))):
            s_ref[h] = s

    def pipelined_step(cur_ref, nxt_ref, j, prefetch=True):
        if prefetch:
            store_scores(nxt_ref, jnp.minimum(j + 1, n_main - 1))
        softmax_pv(lambda h, c: cur_ref[h, :, c * HD:(c + 1) * HD], main_rows(j), tk)

    @pl.when(qi >= n_ctx_tiles)
    def _():
        store_scores(sa_scr, 0)

        def body(i, carry):
            pipelined_step(sa_scr, sb_scr, 2 * i)
            pipelined_step(sb_scr, sa_scr, 2 * i + 1)
            return carry
        lax.fori_loop(0, n_main // 2, body, 0)
        if n_main % 2:
            pipelined_step(sa_scr, sb_scr, n_main - 1, prefetch=False)

    for h in range(grp):
        acc = acc_scr[h]
        o_ref[:, h * HD:(h + 1) * HD] = (acc[:, :HD] / acc[:, HD:HD + 1]).astype(o_ref.dtype)


def _attention(qr, kr, va):
    t = qr.shape[0]
    tq = 256
    tk = _row_tile(t, (1280, 768, 256))
    grp = H_AT // H_KV
    wg = grp * HD
    kern = functools.partial(_attn_kernel, tq, tk, BLK // tq, t // tk)
    return pl.pallas_call(
        kern,
        grid=(H_KV, t // tq),
        in_specs=[pl.BlockSpec((tq, wg), lambda g, i: (i, g)),
                  pl.BlockSpec((t, HD), lambda g, i: (0, g)),
                  pl.BlockSpec((t, 2 * HD), lambda g, i: (0, g))],
        out_specs=pl.BlockSpec((tq, wg), lambda g, i: (i, g)),
        out_shape=jax.ShapeDtypeStruct((t, H_AT * HD), bf16),
        scratch_shapes=[pltpu.VMEM((grp, tq, HD), f32), pltpu.VMEM((grp, tq, 2 * HD), f32),
                        pltpu.VMEM((grp, tq, tk), f32), pltpu.VMEM((grp, tq, tk), f32)],
        compiler_params=_cparams(("parallel", "arbitrary"), VMEM_RESIDENT_MB),
        name="flash_attention",
    )(qr, kr, va)


def _merge_kernel(tm, split, dnf_ref, dnb_ref, z_ref, at_ref, mlf_ref, mlb_ref, og_ref, mod_ref,
                  dng_ref, mlg_ref, wo_ref, n2g_ref, rw1_ref, rw2_ref, *rest):
    i = pl.program_id(0)
    x_in = _stream_rows(i, rest[0], rest[1]) if split else rest[0][...]
    xo_ref, h2_ref, lg_ref = rest[-3:]
    rows = i * tm + lax.broadcasted_iota(jnp.int32, (tm, 1), 0)
    is_ctx = rows < BLK
    w = N_HEADS_SCAN * HD

    def head_norm(x, g):
        return x * lax.rsqrt(jnp.mean(x * x, axis=-1, keepdims=True) + EPS) * g

    acc = jnp.dot(at_ref[...], wo_ref[w:w + H_AT * HD, :], preferred_element_type=f32)
    dn_parts, ml_parts = [], []
    for h in range(N_HEADS_SCAN):
        sl = slice(h * HD, (h + 1) * HD)
        z = z_ref[:, sl].astype(f32)
        dn_parts.append(head_norm(dnf_ref[:, sl] + dnb_ref[:, sl], dng_ref[...]) * (z * _sigmoid(z)))
        ml_parts.append(head_norm(mlf_ref[:, sl] + mlb_ref[:, sl], mlg_ref[...]) * _sigmoid(og_ref[:, sl].astype(f32)))
    dn = jnp.concatenate(dn_parts, axis=1).astype(bf16)
    ml = jnp.concatenate(ml_parts, axis=1).astype(bf16)
    acc = acc + jnp.dot(dn, wo_ref[0:w, :], preferred_element_type=f32)
    acc = acc + jnp.dot(ml, wo_ref[w + H_AT * HD:, :], preferred_element_type=f32)
    x = x_in + _mod_rows(mod_ref, 2, is_ctx) * acc
    xo_ref[...] = x
    y = x * lax.rsqrt(jnp.mean(x * x, axis=-1, keepdims=True) + EPS) * n2g_ref[...]
    h2 = y * (1.0 + _mod_rows(mod_ref, 4, is_ctx)) + _mod_rows(mod_ref, 3, is_ctx)
    hh = h2.astype(bf16)
    hb = lax.bitcast_convert_type(hh.astype(f32), jnp.uint32)
    h2_ref[...] = jnp.bitwise_or(jnp.bitwise_and(hb[:, D // 2:], jnp.uint32(0xFFFF0000)),
                                 jnp.right_shift(hb[:, :D // 2], jnp.uint32(16)))
    hl = (h2 - hh.astype(f32)).astype(bf16)
    r = jnp.dot(jnp.concatenate([hh, hl], axis=0), jnp.concatenate([rw1_ref[...], rw2_ref[...]], axis=1),
                preferred_element_type=f32)
    lg_ref[...] = (r[:tm, :128] + r[tm:, :128]) + (r[:tm, 128:] + r[tm:, 128:])


def _merge_outproj(dnf, dnb, p, at, mlf, mlb, x, mod, dn_g, ml_g, w_out, n2g, rw1, rw2):
    tm = 256
    x_specs, x_args, t = _stream_specs(x, tm)
    w = N_HEADS_SCAN * HD
    row = lambda i: (i, 0)
    full = lambda i: (0, 0)
    sw = pl.BlockSpec((tm, w), row)
    return pl.pallas_call(
        functools.partial(_merge_kernel, tm, isinstance(x, tuple)),
        grid=(t // tm,),
        in_specs=[sw, sw, pl.BlockSpec((tm, w), lambda i: (i, C_DNZ // w)),
                  pl.BlockSpec((tm, H_AT * HD), row), sw, sw,
                  pl.BlockSpec((tm, w), lambda i: (i, C_MLO // w)),
                  pl.BlockSpec((8, 6 * D), full),
                  pl.BlockSpec((1, HD), full), pl.BlockSpec((1, HD), full),
                  pl.BlockSpec((D, D), full), pl.BlockSpec((1, D), full),
                  pl.BlockSpec((D, 128), full), pl.BlockSpec((D, 128), full)] + x_specs,
        out_specs=[pl.BlockSpec((tm, D), row), pl.BlockSpec((tm, D // 2), row), pl.BlockSpec((tm, 128), row)],
        out_shape=[jax.ShapeDtypeStruct((t, D), f32), jax.ShapeDtypeStruct((t, D // 2), jnp.uint32),
                   jax.ShapeDtypeStruct((t, 128), f32)],
        compiler_params=_cparams(("parallel",), VMEM_MATMUL_MB),
        name="merge_outproj",
    )(dnf, dnb, p, at, mlf, mlb, p, mod, dn_g.reshape(1, HD), ml_g.reshape(1, HD), w_out,
      n2g.reshape(1, D), rw1, rw2, *x_args)


def _route_kernel(lg_ref, bias_ref, e_ref, g_ref):
    lt = lg_ref[...].T
    sc = [_sigmoid(lt[e:e + 1, :]) for e in range(N_EXPERTS)]
    bi = [sc[e] + bias_ref[e:e + 1, 0:1] for e in range(N_EXPERTS)]
    n_groups = N_EXPERTS // EXPERTS_PER_GROUP
    best, best_g = None, None
    for g in range(n_groups):
        a, b, c, d = bi[4 * g:4 * g + 4]
        gs = jnp.maximum(jnp.maximum(jnp.maximum(a + b, a + c), jnp.maximum(a + d, b + c)),
                         jnp.maximum(b + d, c + d))
        if g == 0:
            best, best_g = gs, jnp.zeros_like(gs, dtype=jnp.int32)
        else:
            better = gs > best
            best = jnp.where(better, gs, best)
            best_g = jnp.where(better, g, best_g)
    t1 = jnp.full_like(best, -jnp.inf)
    t2 = jnp.full_like(best, -jnp.inf)
    i1 = jnp.zeros_like(best_g)
    i2 = jnp.zeros_like(best_g)
    s1 = jnp.zeros_like(best)
    s2 = jnp.zeros_like(best)
    for e in range(N_EXPERTS):
        v = jnp.where(best_g == e // EXPERTS_PER_GROUP, bi[e], -jnp.inf)
        gt1 = v > t1
        gt2 = jnp.logical_and(jnp.logical_not(gt1), v > t2)
        t2 = jnp.where(gt1, t1, jnp.where(gt2, v, t2))
        i2 = jnp.where(gt1, i1, jnp.where(gt2, e, i2))
        s2 = jnp.where(gt1, s1, jnp.where(gt2, sc[e], s2))
        t1 = jnp.where(gt1, v, t1)
        i1 = jnp.where(gt1, e, i1)
        s1 = jnp.where(gt1, sc[e], s1)
    tot = s1 + s2
    zi = jnp.zeros_like(i1)
    zf = jnp.zeros_like(s1)
    e_ref[...] = jnp.concatenate([i1, i2, zi, zi, zi, zi, zi, zi], axis=0)
    g_ref[...] = jnp.concatenate([s1 / tot, s2 / tot, zf, zf, zf, zf, zf, zf], axis=0)


def _route(logits, router_bias):
    t = logits.shape[0]
    tm = 256
    bias = jnp.zeros((N_EXPERTS, 128), f32).at[:, 0].set(router_bias)
    return pl.pallas_call(
        _route_kernel,
        grid=(t // tm,),
        in_specs=[pl.BlockSpec((tm, 128), lambda i: (i, 0)), pl.BlockSpec((N_EXPERTS, 128), lambda i: (0, 0))],
        out_specs=[pl.BlockSpec((8, tm), lambda i: (0, i)), pl.BlockSpec((8, tm), lambda i: (0, i))],
        out_shape=[jax.ShapeDtypeStruct((8, t), jnp.int32), jax.ShapeDtypeStruct((8, t), f32)],
        compiler_params=_cparams(("parallel",)),
        name="route_top2",
    )(logits, bias)


def _dispatch(e_rows, g_rows, n_blocks):
    t = e_rows.shape[1]
    n = 2 * t
    flat_e = e_rows[0:2].reshape(n)
    flat_w = g_rows[0:2].reshape(n)
    order = jnp.argsort(flat_e, stable=True).astype(jnp.int32)
    experts = jnp.arange(N_EXPERTS, dtype=jnp.int32)
    counts = jnp.sum(flat_e[:, None] == experts[None, :], axis=0).astype(jnp.int32)
    starts = jnp.cumsum(counts) - counts
    padded = (counts + MOE_BM - 1) // MOE_BM * MOE_BM
    p_ends = jnp.cumsum(padded)
    p_starts = p_ends - padded
    blk_start = jnp.arange(n_blocks, dtype=jnp.int32) * MOE_BM
    blk_e = jnp.minimum(jnp.sum(p_ends[None, :] <= blk_start[:, None], axis=1), N_EXPERTS - 1).astype(jnp.int32)
    sel = (blk_e[:, None] == experts[None, :]).astype(jnp.int32)
    pick = lambda v: jnp.sum(sel * v[None, :], axis=1)[:, None]
    rank = blk_start[:, None] + jnp.arange(MOE_BM, dtype=jnp.int32)[None, :] - pick(p_starts)
    valid = jnp.logical_and(rank >= 0, rank < pick(counts))
    src = order[jnp.clip(pick(starts) + rank, 0, n - 1)]
    tok = jnp.where(valid, jnp.where(src >= t, src - t, src), 0).astype(jnp.int32).reshape(n_blocks, 1, MOE_BM)
    dst = jnp.where(valid, src, -1).astype(jnp.int32).reshape(n_blocks, 1, MOE_BM)
    wt = jnp.where(valid, flat_w[src], 0.0).reshape(n_blocks, MOE_BM, 1)
    n_steps = n_blocks + 2
    tok_s = jnp.concatenate([tok, jnp.zeros((2, 1, MOE_BM), jnp.int32)], axis=0)
    e_s = jnp.concatenate([blk_e[:1], blk_e, blk_e[-1:]], axis=0)
    wt_s = jnp.concatenate([jnp.zeros((1, MOE_BM, 1), f32), wt, jnp.zeros((1, MOE_BM, 1), f32)], axis=0)
    dst_s = jnp.concatenate([jnp.full((2, 1, MOE_BM), -1, jnp.int32), dst], axis=0)
    is_pad = (dst_s < 0).reshape(-1)
    pad_rank = (jnp.cumsum(is_pad.astype(jnp.int32)) - 1).reshape(n_steps, 1, MOE_BM)
    dst_s = jnp.where(dst_s >= 0, dst_s, 2 * t + pad_rank)
    return e_s, tok_s, dst_s, wt_s


def _moe_kernel(n_steps, e_ref, tok_ref, dst_ref, wt_ref, h_hbm, wg_ref, wu_ref, wd_ref, out_hbm,
                xb0, xb1, yb0, yb1, sems):
    del e_ref
    s = pl.program_id(0)
    xbs, ybs = (xb0, xb1), (yb0, yb1)

    def wait_step_dmas():
        pltpu.make_async_copy(h_hbm.at[pl.ds(0, MOE_BM)], xb0, sems.at[0]).wait()
        pltpu.make_async_copy(yb0, out_hbm.at[pl.ds(0, MOE_BM)], sems.at[1]).wait()

    @pl.when(s == 0)
    def _():
        xb1[...] = jnp.zeros_like(xb1)
        yb1[...] = jnp.zeros_like(yb1)

    @pl.when(s > 0)
    def _():
        wait_step_dmas()

    def step(par):
        x_in, x_cur = xbs[par], xbs[1 - par]
        y_cur, y_out = ybs[par], ybs[1 - par]
        for r in range(MOE_BM):
            pltpu.make_async_copy(h_hbm.at[pl.ds(tok_ref[0, 0, r], 1)], x_in.at[pl.ds(r, 1)], sems.at[0]).start()
        w = x_cur[...]
        x_lo = lax.bitcast_convert_type(jnp.left_shift(w, jnp.uint32(16)), f32).astype(bf16)
        x_hi = lax.bitcast_convert_type(jnp.bitwise_and(w, jnp.uint32(0xFFFF0000)), f32).astype(bf16)
        dh = D // 2
        g = (jnp.dot(x_lo, wg_ref[0, :dh, :], preferred_element_type=f32)
             + jnp.dot(x_hi, wg_ref[0, dh:, :], preferred_element_type=f32))
        u = (jnp.dot(x_lo, wu_ref[0, :dh, :], preferred_element_type=f32)
             + jnp.dot(x_hi, wu_ref[0, dh:, :], preferred_element_type=f32))
        a = (g * _sigmoid(g) * u).astype(bf16)
        y_cur[...] = jnp.dot(a, wd_ref[0], preferred_element_type=f32) * wt_ref[0]
        for r in range(MOE_BM):
            pltpu.make_async_copy(y_out.at[pl.ds(r, 1)], out_hbm.at[pl.ds(dst_ref[0, 0, r], 1)], sems.at[1]).start()

    @pl.when(lax.rem(s, 2) == 0)
    def _():
        step(0)

    @pl.when(lax.rem(s, 2) == 1)
    def _():
        step(1)

    @pl.when(s == n_steps - 1)
    def _():
        wait_step_dmas()


def _moe(h2, e_s, tok_s, dst_s, wt_s, wg, wu, wd):
    n_steps = tok_s.shape[0]
    grid_spec = pltpu.PrefetchScalarGridSpec(
        num_scalar_prefetch=1,
        grid=(n_steps,),
        in_specs=[pl.BlockSpec((1, 1, MOE_BM), lambda s, e: (s, 0, 0), memory_space=pltpu.SMEM),
                  pl.BlockSpec((1, 1, MOE_BM), lambda s, e: (s, 0, 0), memory_space=pltpu.SMEM),
                  pl.BlockSpec((1, MOE_BM, 1), lambda s, e: (s, 0, 0)),
                  pl.BlockSpec(memory_space=pl.ANY),
                  pl.BlockSpec((1, D, D_EXPERT), lambda s, e: (e[s], 0, 0)),
                  pl.BlockSpec((1, D, D_EXPERT), lambda s, e: (e[s], 0, 0)),
                  pl.BlockSpec((1, D_EXPERT, D), lambda s, e: (e[s], 0, 0))],
        out_specs=pl.BlockSpec(memory_space=pl.ANY),
        scratch_shapes=[pltpu.VMEM((MOE_BM, D // 2), jnp.uint32)] * 2 + [pltpu.VMEM((MOE_BM, D), f32)] * 2
        + [pltpu.SemaphoreType.DMA((2,))],
    )
    return pl.pallas_call(
        functools.partial(_moe_kernel, n_steps),
        grid_spec=grid_spec,
        out_shape=jax.ShapeDtypeStruct((n_steps * MOE_BM, D), f32),
        compiler_params=_cparams(("arbitrary",), VMEM_RESIDENT_MB),
        name="moe_experts",
    )(e_s, tok_s, dst_s, wt_s, h2, wg, wu, wd)


def _final_kernel(x_ref, y0_ref, y1_ref, mod_ref, g_ref, o_ref):
    x = x_ref[...] + mod_ref[0:1, 5 * D:6 * D] * (y0_ref[...] + y1_ref[...])
    o_ref[...] = x * lax.rsqrt(jnp.mean(x * x, axis=-1, keepdims=True) + EPS) * g_ref[...]


def _final(x, moe, mod, g):
    t = x.shape[0]
    tm = 256
    nrow = t // tm
    nctx = BLK // tm
    return pl.pallas_call(
        _final_kernel,
        grid=(nrow - nctx,),
        in_specs=[pl.BlockSpec((tm, D), lambda i: (i + nctx, 0)),
                  pl.BlockSpec((tm, D), lambda i: (i + nctx, 0)),
                  pl.BlockSpec((tm, D), lambda i: (i + nctx + nrow, 0)),
                  pl.BlockSpec((8, 6 * D), lambda i: (0, 0)),
                  pl.BlockSpec((1, D), lambda i: (0, 0))],
        out_specs=pl.BlockSpec((tm, D), lambda i: (i, 0)),
        out_shape=jax.ShapeDtypeStruct((t - BLK, D), f32),
        compiler_params=_cparams(("parallel",), VMEM_STREAM_MB),
        name="final_norm",
    )(x, moe, moe, mod, g.reshape(1, D))


def _cast_kernel(x_ref, o_ref):
    o_ref[...] = x_ref[...].astype(o_ref.dtype)


def _to_bf16(w, layer):
    shape = w.shape[1:]
    w2 = w.reshape(-1, shape[-1])
    cols = shape[-1]
    rows = w2.shape[0] // w.shape[0]
    tr = _row_tile(rows, (1024, 512, 256))
    off = layer * (rows // tr)
    out = pl.pallas_call(
        _cast_kernel,
        grid=(rows // tr,),
        in_specs=[pl.BlockSpec((tr, cols), lambda i: (i + off, 0))],
        out_specs=pl.BlockSpec((tr, cols), lambda i: (i, 0)),
        out_shape=jax.ShapeDtypeStruct((rows, cols), bf16),
        compiler_params=_cparams(("parallel",), VMEM_MATMUL_MB),
        name="cast_bf16",
    )(w2)
    return out.reshape(shape)


def _prep_in_weights(w_in):
    splits = (1536, 512, 8, 8, 1024, 256, 256, 512, 512, 512, 512, 8, 8)
    offs = [0]
    for s in splits:
        offs.append(offs[-1] + s)
    part = lambda i: w_in[:, offs[i]:offs[i + 1]]
    main = jnp.concatenate([part(i) for i in (0, 1, 4, 5, 6, 7, 8, 9, 10)], axis=1).astype(bf16)
    gates = jnp.concatenate([part(i) for i in (2, 3, 11, 12)], axis=1)
    gates = jnp.pad(gates, ((0, 0), (0, 128 - gates.shape[1])))
    g1 = gates.astype(bf16)
    g2 = (gates - g1.astype(f32)).astype(bf16)
    return main, g1, g2


def _layer(l, rope, x, moe_prev, mods, norm1_g, norm2_g, w_in, dn_conv, dn_a_log, dn_dt_bias, dn_norm_g,
           q_norm_g, k_norm_g, ml_i_bias, ml_f_bias, ml_norm_g, w_out, rw1, rw2, router_bias,
           w_gate, w_up, w_down):
    w_main, wg1, wg2 = _prep_in_weights(w_in[l])
    x, h, graw = _norm1(x, moe_prev, mods[l - 1] if l else None, mods[l], norm1_g[l], wg1, wg2)
    t = h.shape[0]
    p = _matmul(h, w_main, bf16)
    conv_w = jnp.pad(dn_conv[l], ((0, 8 - CONV_W), (0, 0)))
    gate_params = jnp.zeros((8, 128), f32)
    gate_params = gate_params.at[0, 8:16].set(dn_dt_bias[l].reshape(8))
    gate_params = gate_params.at[0, 16:24].set(ml_i_bias[l].reshape(8))
    gate_params = gate_params.at[0, 24:32].set(ml_f_bias[l].reshape(8))
    gate_params = gate_params.at[1, 8:16].set(dn_a_log[l].reshape(8))
    dnq, gcol, grow = _scan_prep(p, graw, conv_w, gate_params)
    dnf, dnb = _gdn_scan(dnq, gcol, grow)
    mlf, mlb = _mlstm_scan(p, gcol, grow)
    qr, kr, va = _attn_prep(p, q_norm_g[l], k_norm_g[l], rope)
    at = _attention(qr, kr, va)
    x, h2, logits = _merge_outproj(dnf, dnb, p, at, mlf, mlb, x, mods[l], dn_norm_g[l], ml_norm_g[l],
                                   w_out[l].astype(bf16), norm2_g[l], rw1, rw2)
    e_rows, g_rows = _route(logits, router_bias)
    n_blocks = (2 * t + N_EXPERTS * (MOE_BM - 1) + MOE_BM - 1) // MOE_BM
    blk_e, tok, dst, wt = _dispatch(e_rows, g_rows, n_blocks)
    moe = _moe(h2, blk_e, tok, dst, wt, _to_bf16(w_gate, l), _to_bf16(w_up, l), _to_bf16(w_down, l))
    return x, moe


def kernel(x, c, ctx, c_ctx, w_mod, b_mod, norm1_g, norm2_g, w_in, dn_conv, dn_a_log, dn_dt_bias, dn_norm_g, q_norm_g, k_norm_g, ml_i_bias, ml_f_bias, ml_norm_g, w_out, router_w, router_bias, w_gate, w_up, w_down, final_norm_g):
    b, seq, d = x.shape
    assert b == 1 and d == D and ctx.shape[1] == BLK and seq % BLK == 0 and seq % GRID_W == 0
    depth = w_mod.shape[0]
    mods = _mods(c, c_ctx, w_mod, b_mod)
    xs = (ctx[0], x[0])
    rw = jnp.pad(router_w, ((0, 0), (0, 128 - N_EXPERTS)))
    rw1 = rw.astype(bf16)
    rw2 = (rw - rw1.astype(f32)).astype(bf16)
    moe = None
    rope = _rope_tables(BLK + seq)
    for l in range(depth):
        xs, moe = _layer(l, rope, xs, moe, mods, norm1_g, norm2_g, w_in, dn_conv, dn_a_log, dn_dt_bias,
                         dn_norm_g, q_norm_g, k_norm_g, ml_i_bias, ml_f_bias, ml_norm_g, w_out, rw1, rw2,
                         router_bias, w_gate, w_up, w_down)
    out = _final(xs, moe, mods[depth - 1], final_norm_g)
    return out.reshape(b, seq, d)
```
